```python
import math
import jax, jax.numpy as jnp
from jax import lax
import numpy as np

D_MODEL = 2048
BATCH = 4
SEQ = 4096
DEPTH = 2

DA_HEADS = 4
DA_HEAD_DIM = 128
DA_V_DIM = 2 * DA_HEAD_DIM
DA_WIDTH = DA_HEADS * DA_V_DIM
Q_BLOCK = 128
GLA_HEADS = 4
GLA_DK = 128
GLA_DV = 256
GLA_WIDTH = GLA_HEADS * GLA_DV
GLA_GATE_RANK = 16
GLA_TAU = 16.0
GLA_CHUNK = 64
SSD_HEADS = 16
SSD_HEAD_DIM = 64
SSD_STATE = 128
SSD_GROUPS = 4
SSD_CONV = 4
SSD_CHUNK = 128
SSD_WIDTH = SSD_HEADS * SSD_HEAD_DIM
SSD_CONV_DIM = SSD_WIDTH + 2 * SSD_GROUPS * SSD_STATE
N_BRANCH = 3
BRANCH_WIDTH = 1024
N_EXPERTS = 64
TOP_K = 8
EXPERT_DIM = 512
N_EXPERT_GROUPS = 8
TOPK_GROUPS = 4
ROUTED_SCALE = 2.5
MOE_BLOCK = 256
DEEPNORM_ALPHA = (2 * DEPTH) ** 0.25
DEEPNORM_BETA = (8 * DEPTH) ** -0.25
EPS = 1e-5

IN_SIZES = (
    DA_HEADS * 2 * DA_HEAD_DIM,
    DA_HEADS * 2 * DA_HEAD_DIM,
    DA_WIDTH,
    GLA_HEADS * GLA_DK,
    GLA_HEADS * GLA_DK,
    GLA_WIDTH,
    GLA_GATE_RANK,
    GLA_WIDTH,
    SSD_WIDTH,
    SSD_CONV_DIM,
    SSD_HEADS,
    N_BRANCH * D_MODEL,
)
IN_WIDTH = 15392

kernel_name = "hybrid_diffattn_gla_ssd_moe_deepnorm"


def _rmsnorm(x, g):
    xf = x.astype(jnp.float32)
    y = xf * lax.rsqrt(jnp.mean(xf * xf, axis=-1, keepdims=True) + EPS)
    return (y * g).astype(x.dtype)


def _layernorm(x, g, b):
    xf = x.astype(jnp.float32)
    mu = jnp.mean(xf, axis=-1, keepdims=True)
    var = jnp.mean((xf - mu) ** 2, axis=-1, keepdims=True)
    y = (xf - mu) * lax.rsqrt(var + EPS)
    return (y * g + b).astype(x.dtype)


def _diff_attention(q, k, v, lam_params, norm_g, layer_idx):
    B, S, _ = q.shape
    H, d = DA_HEADS, DA_HEAD_DIM
    q = q.reshape(B, S, H, 2, d).transpose(0, 2, 3, 1, 4) * (d ** -0.5)
    k = k.reshape(B, S, H, 2, d).transpose(0, 2, 3, 1, 4)
    v = v.reshape(B, S, H, DA_V_DIM).transpose(0, 2, 1, 3)
    lam_init = 0.8 - 0.6 * math.exp(-0.3 * layer_idx)
    lp = lam_params.astype(jnp.float32)
    lam = jnp.exp(jnp.sum(lp[0] * lp[1])) - jnp.exp(jnp.sum(lp[2] * lp[3])) + lam_init
    slopes = 2.0 ** (-8.0 * jnp.arange(1, H + 1, dtype=jnp.float32) / H)
    kpos = jnp.arange(S)

    def block(i):
        start = i * Q_BLOCK
        qb = lax.dynamic_slice_in_dim(q, start, Q_BLOCK, axis=3)
        qpos = start + jnp.arange(Q_BLOCK)
        s = jnp.einsum('bhcqd,bhckd->bhcqk', qb, k).astype(jnp.float32)
        dist = (qpos[:, None] - kpos[None, :]).astype(jnp.float32)
        s = s - slopes[None, :, None, None, None] * dist
        s = jnp.where(dist >= 0, s, -jnp.inf)
        p = jax.nn.softmax(s, axis=-1)
        a = p[:, :, 0] - lam * p[:, :, 1]
        return jnp.einsum('bhqk,bhke->bhqe', a.astype(v.dtype), v)

    o = lax.map(block, jnp.arange(S // Q_BLOCK))
    o = o.transpose(1, 0, 3, 2, 4).reshape(B, S, H, DA_V_DIM)
    o = _rmsnorm(o, norm_g) * (1.0 - lam_init)
    return o.reshape(B, S, DA_WIDTH)


def _gla(q, k, v, gate_lr, r, w2, b2, norm_g):
    B, S, _ = q.shape
    H, dk, dv, C = GLA_HEADS, GLA_DK, GLA_DV, GLA_CHUNK
    N = S // C
    gk = jax.nn.log_sigmoid((gate_lr @ w2 + b2).astype(jnp.float32)) / GLA_TAU

    def chunks(t, dim):
        return t.reshape(B, N, C, H, dim).transpose(0, 3, 1, 2, 4)

    q = chunks(q, dk) * (dk ** -0.5)
    k = chunks(k, dk)
    v = chunks(v, dv)
    bcum = jnp.cumsum(chunks(gk, dk), axis=3)
    q_e = q * jnp.exp(bcum)
    k_e = k * jnp.exp(-bcum)
    k_d = k * jnp.exp(bcum[..., -1:, :] - bcum)
    causal = jnp.tril(jnp.ones((C, C), dtype=bool))
    att = jnp.where(causal, jnp.einsum('bhncd,bhnsd->bhncs', q_e, k_e), 0.0)
    o_intra = jnp.einsum('bhncs,bhnse->bhnce', att, v)
    upd = jnp.einsum('bhnsd,bhnse->bhnde', k_d, v)
    decay = jnp.exp(bcum[..., -1, :])

    def step(state, inp):
        dec, u = inp
        return dec[..., None] * state + u, state

    _, s_prev = lax.scan(step, jnp.zeros_like(upd[:, :, 0]),
                         (jnp.moveaxis(decay, 2, 0), jnp.moveaxis(upd, 2, 0)))
    s_prev = jnp.moveaxis(s_prev, 0, 2)
    o = o_intra + jnp.einsum('bhncd,bhnde->bhnce', q_e, s_prev)
    o = o.transpose(0, 2, 3, 1, 4).reshape(B, S, H, dv)
    o = _rmsnorm(o, norm_g).reshape(B, S, GLA_WIDTH)
    return o * jax.nn.silu(r)


def _ssd(z, xbc, dt, conv_w, conv_b, dt_bias, a_log, d_skip, norm_g):
    B, S, _ = z.shape
    G, R, P, NS, Q = SSD_GROUPS, SSD_HEADS // SSD_GROUPS, SSD_HEAD_DIM, SSD_STATE, SSD_CHUNK
    N = S // Q
    xbc = lax.conv_general_dilated(xbc, conv_w[:, None, :], window_strides=(1,),
                                   padding=[(SSD_CONV - 1, 0)],
                                   dimension_numbers=('NWC', 'WIO', 'NWC'),
                                   feature_group_count=SSD_CONV_DIM)
    xbc = jax.nn.silu(xbc + conv_b)
    xs, bm, cm = jnp.split(xbc, [SSD_WIDTH, SSD_WIDTH + G * NS], axis=-1)
    xs = xs.reshape(B, N, Q, G, R, P)
    bm = bm.reshape(B, N, Q, G, NS)
    cm = cm.reshape(B, N, Q, G, NS)
    dt = jax.nn.softplus((dt + dt_bias).astype(jnp.float32)).reshape(B, N, Q, G, R)
    a = -jnp.exp(a_log.astype(jnp.float32)).reshape(G, R)
    a_cs = jnp.cumsum((dt * a).transpose(0, 3, 4, 1, 2), axis=-1)
    xdt = xs * dt[..., None]
    causal = jnp.tril(jnp.ones((Q, Q), dtype=bool))
    lmat = jnp.exp(jnp.where(causal, a_cs[..., :, None] - a_cs[..., None, :], -jnp.inf))
    cb = jnp.einsum('bnlgs,bnmgs->bgnlm', cm, bm)
    y_diag = jnp.einsum('bgnlm,bgrnlm,bnmgrp->bnlgrp', cb, lmat, xdt)
    decay_states = jnp.exp(a_cs[..., -1:] - a_cs)
    states = jnp.einsum('bnmgs,bgrnm,bnmgrp->bngrps', bm, decay_states, xdt)
    chunk_cs = jnp.cumsum(jnp.pad(a_cs[..., -1], ((0, 0), (0, 0), (0, 0), (1, 0))), axis=-1)
    cmask = jnp.tril(jnp.ones((N + 1, N + 1), dtype=bool))
    decay_chunk = jnp.exp(jnp.where(cmask, chunk_cs[..., :, None] - chunk_cs[..., None, :], -jnp.inf))
    states = jnp.pad(states, ((0, 0), (1, 0), (0, 0), (0, 0), (0, 0), (0, 0)))
    s_prev = jnp.einsum('bgrzc,bcgrps->bzgrps', decay_chunk, states)[:, :N]
    y_off = jnp.einsum('bnlgs,bngrps,bgrnl->bnlgrp', cm, s_prev, jnp.exp(a_cs))
    y = y_diag + y_off + xs * d_skip.reshape(G, R)[:, :, None]
    y = y.reshape(B, S, SSD_WIDTH) * jax.nn.silu(z)
    gw = SSD_WIDTH // G
    y = _rmsnorm(y.reshape(B, S, G, gw), norm_g.reshape(G, gw))
    return y.reshape(B, S, SSD_WIDTH)


def _routed_experts(xt, eidx, w, wg, wu, wd):
    T = xt.shape[0]
    A = T * TOP_K
    flat_e = eidx.reshape(-1)
    flat_w = w.reshape(-1)
    flat_tok = jnp.arange(A, dtype=jnp.int32) // TOP_K
    order = jnp.argsort(flat_e)
    e_sorted = flat_e[order]
    counts = jnp.bincount(flat_e, length=N_EXPERTS)
    padded = (counts + MOE_BLOCK - 1) // MOE_BLOCK * MOE_BLOCK
    start = jnp.cumsum(counts) - counts
    pend = jnp.cumsum(padded)
    pstart = pend - padded
    dest = pstart[e_sorted] + jnp.arange(A, dtype=jnp.int32) - start[e_sorted]
    nblk = -(-(A + N_EXPERTS * MOE_BLOCK) // MOE_BLOCK)
    P = nblk * MOE_BLOCK
    tok_buf = jnp.zeros((P,), jnp.int32).at[dest].set(flat_tok[order])
    w_buf = jnp.zeros((P,), flat_w.dtype).at[dest].set(flat_w[order])
    blk_e = jnp.minimum(jnp.searchsorted(pend, jnp.arange(nblk) * MOE_BLOCK, side='right'),
                        N_EXPERTS - 1)

    def body(j, y):
        tok = lax.dynamic_slice_in_dim(tok_buf, j * MOE_BLOCK, MOE_BLOCK)
        wt = lax.dynamic_slice_in_dim(w_buf, j * MOE_BLOCK, MOE_BLOCK)
        e = blk_e[j]
        xb = xt[tok]
        h = jax.nn.silu(xb @ wg[e]) * (xb @ wu[e])
        out = (h @ wd[e]) * wt[:, None]
        return y.at[tok].add(out.astype(y.dtype))

    return lax.fori_loop(0, nblk, body, jnp.zeros_like(xt))


def _moe(u, router_w, router_bias, wg, wu, wd, sg, su, sd):
    B, S, D = u.shape
    xt = u.reshape(B * S, D)
    scores = jax.nn.sigmoid((xt @ router_w).astype(jnp.float32))
    sel = scores + router_bias.astype(jnp.float32)
    grp = sel.reshape(-1, N_EXPERT_GROUPS, N_EXPERTS // N_EXPERT_GROUPS)
    gscore = lax.top_k(grp, 2)[0].sum(-1)
    _, gidx = lax.top_k(gscore, TOPK_GROUPS)
    gmask = jax.nn.one_hot(gidx, N_EXPERT_GROUPS).sum(-2) > 0
    emask = jnp.repeat(gmask, N_EXPERTS // N_EXPERT_GROUPS, axis=-1)
    _, eidx = lax.top_k(jnp.where(emask, sel, -jnp.inf), TOP_K)
    w = jnp.take_along_axis(scores, eidx, axis=-1)
    w = w / jnp.sum(w, axis=-1, keepdims=True) * ROUTED_SCALE
    routed = _routed_experts(xt, eidx, w, wg, wu, wd)
    shared = (jax.nn.silu(xt @ sg) * (xt @ su)) @ sd
    return (routed + shared).reshape(B, S, D)


def setup_inputs(seed: int = 0) -> dict:
    key = jax.random.key(seed)
    ks = jax.random.split(key, 32)
    f32 = jnp.float32

    def nrm(k, shape, scale):
        return jax.random.normal(k, shape, f32) * scale

    L, D, E, F = DEPTH, D_MODEL, N_EXPERTS, EXPERT_DIM
    dt0 = jnp.exp(jax.random.uniform(ks[9], (L, SSD_HEADS), f32, math.log(1e-3), math.log(1e-1)))
    return {
        "x": nrm(ks[0], (BATCH, SEQ, D), 1.0),
        "w_in": nrm(ks[1], (L, D, IN_WIDTH), D ** -0.5),
        "da_lambda": nrm(ks[2], (L, 4, DA_HEAD_DIM), 0.1),
        "da_norm_g": 1.0 + nrm(ks[3], (L, DA_V_DIM), 0.02),
        "gla_gate_w2": nrm(ks[4], (L, GLA_GATE_RANK, GLA_HEADS * GLA_DK), GLA_GATE_RANK ** -0.5),
        "gla_gate_b": nrm(ks[5], (L, GLA_HEADS * GLA_DK), 0.1),
        "gla_norm_g": 1.0 + nrm(ks[6], (L, GLA_DV), 0.02),
        "ssd_conv_w": nrm(ks[7], (L, SSD_CONV, SSD_CONV_DIM), SSD_CONV ** -0.5),
        "ssd_conv_b": nrm(ks[8], (L, SSD_CONV_DIM), 0.02),
        "ssd_dt_bias": dt0 + jnp.log(-jnp.expm1(-dt0)),
        "ssd_a_log": jnp.log(jax.random.uniform(ks[10], (L, SSD_HEADS), f32, 1.0, 16.0)),
        "ssd_d": 1.0 + nrm(ks[11], (L, SSD_HEADS), 0.02),
        "ssd_norm_g": 1.0 + nrm(ks[12], (L, SSD_WIDTH), 0.02),
        "w_branch": nrm(ks[13], (L, N_BRANCH, BRANCH_WIDTH, D), BRANCH_WIDTH ** -0.5 * DEEPNORM_BETA),
        "w_out": nrm(ks[14], (L, D, D), D ** -0.5 * DEEPNORM_BETA),
        "ln1_g": 1.0 + nrm(ks[15], (L, D), 0.02),
        "ln1_b": nrm(ks[16], (L, D), 0.02),
        "router_w": nrm(ks[17], (L, D, E), D ** -0.5),
        "router_bias": nrm(ks[18], (L, E), 0.01),
        "exp_w_gate": nrm(ks[19], (L, E, D, F), D ** -0.5),
        "exp_w_up": nrm(ks[20], (L, E, D, F), D ** -0.5),
        "exp_w_down": nrm(ks[21], (L, E, F, D), F ** -0.5 * DEEPNORM_BETA),
        "sh_w_gate": nrm(ks[22], (L, D, F), D ** -0.5),
        "sh_w_up": nrm(ks[23], (L, D, F), D ** -0.5),
        "sh_w_down": nrm(ks[24], (L, F, D), F ** -0.5 * DEEPNORM_BETA),
        "ln2_g": 1.0 + nrm(ks[25], (L, D), 0.02),
        "ln2_b": nrm(ks[26], (L, D), 0.02),
    }


def reference(x, w_in, da_lambda, da_norm_g, gla_gate_w2, gla_gate_b, gla_norm_g,
              ssd_conv_w, ssd_conv_b, ssd_dt_bias, ssd_a_log, ssd_d, ssd_norm_g,
              w_branch, w_out, ln1_g, ln1_b, router_w, router_bias,
              exp_w_gate, exp_w_up, exp_w_down, sh_w_gate, sh_w_up, sh_w_down,
              ln2_g, ln2_b):
    B, S, D = x.shape
    offsets = np.cumsum(IN_SIZES)[:-1].tolist()
    for l in range(DEPTH):
        proj = x @ w_in[l]
        (da_q, da_k, da_v, g_q, g_k, g_v, g_lr, g_r,
         s_z, s_xbc, s_dt, m_gate) = jnp.split(proj, offsets, axis=-1)
        br_a = _diff_attention(da_q, da_k, da_v, da_lambda[l], da_norm_g[l], l)
        br_b = _gla(g_q, g_k, g_v, g_lr, g_r, gla_gate_w2[l], gla_gate_b[l], gla_norm_g[l])
        br_c = _ssd(s_z, s_xbc, s_dt, ssd_conv_w[l], ssd_conv_b[l], ssd_dt_bias[l],
                    ssd_a_log[l], ssd_d[l], ssd_norm_g[l])
        gates = jax.nn.sigmoid(m_gate.reshape(B, S, N_BRANCH, D))
        branches = (br_a, br_b, br_c)
        merged = gates[:, :, 0] * (branches[0] @ w_branch[l, 0])
        for i in range(1, N_BRANCH):
            merged = merged + gates[:, :, i] * (branches[i] @ w_branch[l, i])
        mix = merged @ w_out[l]
        x = _layernorm(DEEPNORM_ALPHA * x + mix, ln1_g[l], ln1_b[l])
        ffn = _moe(x, router_w[l], router_bias[l], exp_w_gate[l], exp_w_up[l], exp_w_down[l],
                   sh_w_gate[l], sh_w_up[l], sh_w_down[l])
        x = _layernorm(DEEPNORM_ALPHA * x + ffn, ln2_g[l], ln2_b[l])
    return x
```

```python
import functools
import math

import jax
import jax.numpy as jnp
import numpy as np
from jax import lax
from jax.experimental import pallas as pl
from jax.experimental.pallas import tpu as pltpu

F32 = jnp.float32
BF16 = jnp.bfloat16

DA_HEADS = 4
DA_HEAD_DIM = 128
DA_V_DIM = 2 * DA_HEAD_DIM
GLA_HEADS = 4
GLA_DK = 128
GLA_DV = 256
GLA_GATE_RANK = 16
GLA_TAU = 16.0
GLA_CHUNK = 64
SSD_HEADS = 16
SSD_HEAD_DIM = 64
SSD_STATE = 128
SSD_GROUPS = 4
SSD_CONV = 4
SSD_CHUNK = 128
SSD_WIDTH = SSD_HEADS * SSD_HEAD_DIM
N_BRANCH = 3
BRANCH_WIDTH = 1024
N_EXPERTS = 64
TOP_K = 8
N_EXPERT_GROUPS = 8
TOPK_GROUPS = 4
ROUTED_SCALE = 2.5
EPS = 1e-5

LANES = 128
SUBLANES = 8
VMEM_LIMIT = 52 * 1024 * 1024

ATTN_BLOCK = 512
GLA_BLOCK = 512
MOE_BLOCK = 512
TOKEN_TILE = 256


def _params(sem):
    return pltpu.CompilerParams(dimension_semantics=sem, vmem_limit_bytes=VMEM_LIMIT)


def _silu(x):
    return x * (1.0 / (1.0 + jnp.exp(-x)))


def _sigmoid(x):
    return 1.0 / (1.0 + jnp.exp(-x))


def _split3(x):
    h1 = x.astype(BF16)
    r1 = x - h1.astype(F32)
    h2 = r1.astype(BF16)
    r2 = r1 - h2.astype(F32)
    return h1, h2, r2.astype(BF16)


def _dot(a, b):
    return jnp.dot(a, b, preferred_element_type=F32)


def _dot_nt(a, b):
    return lax.dot_general(a, b, (((1,), (1,)), ((), ())), preferred_element_type=F32)


def _dot_tn(a, b):
    return lax.dot_general(a, b, (((0,), (0,)), ((), ())), preferred_element_type=F32)


def _mm_kernel(a_ref, w_ref, s_ref, o_ref):
    acc = _dot(a_ref[...], w_ref[...])
    o_ref[...] = (acc * s_ref[...]).astype(o_ref.dtype)


def _matmul(a, w, col_scale, out_dtype, tm, tn, name):
    m, k = a.shape
    n = w.shape[1]
    return pl.pallas_call(
        _mm_kernel,
        out_shape=jax.ShapeDtypeStruct((m, n), out_dtype),
        grid=(m // tm, n // tn),
        in_specs=[
            pl.BlockSpec((tm, k), lambda i, j: (i, 0)),
            pl.BlockSpec((k, tn), lambda i, j: (0, j)),
            pl.BlockSpec((1, tn), lambda i, j: (0, j)),
        ],
        out_specs=pl.BlockSpec((tm, tn), lambda i, j: (i, j)),
        compiler_params=_params(("parallel", "arbitrary")),
        name=name,
    )(a, w, col_scale)


def _attn_kernel(qi_tab, ki_tab, q_ref, k_ref, v_ref, slope_ref, lam_ref, g_ref, o_ref,
                 m_ref, l_ref, acc_ref, *, blk, lam_init):
    p = pl.program_id(2)
    qi = qi_tab[p]
    ki = ki_tab[p]

    @pl.when(ki == 0)
    def _init():
        m_ref[...] = jnp.full(m_ref.shape, -jnp.inf, F32)
        l_ref[...] = jnp.zeros(l_ref.shape, F32)
        acc_ref[...] = jnp.zeros(acc_ref.shape, F32)

    col = lax.broadcasted_iota(jnp.int32, (1, blk), 1)
    rel = (col + (ki - qi) * blk).astype(F32)
    bias = slope_ref[0] * rel

    def step(masked):
        q = q_ref[...]
        k = k_ref[...]
        v = v_ref[...]
        if masked:
            r = lax.broadcasted_iota(jnp.int32, (blk, blk), 0)
            c = lax.broadcasted_iota(jnp.int32, (blk, blk), 1)
            keep = c <= r
        for mp in range(2):
            sl = slice(mp * DA_HEAD_DIM, (mp + 1) * DA_HEAD_DIM)
            s = _dot_nt(q[:, sl], k[:, sl]) + bias
            if masked:
                s = jnp.where(keep, s, -jnp.inf)
            m_old = m_ref[mp]
            m_new = jnp.maximum(m_old, jnp.max(s, axis=1, keepdims=True))
            alpha = jnp.exp(m_old - m_new)
            pr = jnp.exp(s - m_new)
            l_ref[mp] = alpha * l_ref[mp] + jnp.sum(pr, axis=1, keepdims=True)
            acc_ref[mp] = alpha * acc_ref[mp] + _dot(pr.astype(BF16), v)
            m_ref[mp] = m_new

    @pl.when(ki < qi)
    def _off_diag():
        step(False)

    @pl.when(ki == qi)
    def _diag():
        step(True)
        lp = lam_ref[...]
        lam = (jnp.exp(jnp.sum(lp[0:1] * lp[1:2], axis=1, keepdims=True))
               - jnp.exp(jnp.sum(lp[2:3] * lp[3:4], axis=1, keepdims=True)) + lam_init)
        o = acc_ref[0] / l_ref[0] - lam * (acc_ref[1] / l_ref[1])
        o = o * lax.rsqrt(jnp.mean(o * o, axis=1, keepdims=True) + EPS)
        o_ref[...] = (o * g_ref[...] * (1.0 - lam_init)).astype(o_ref.dtype)


def _diff_attention(qkv, lam_params, norm_g, layer_idx, batch, seq):
    blk = min(ATTN_BLOCK, seq)
    nq = seq // blk
    pairs = [(i, j) for i in range(nq) for j in range(i + 1)]
    qi_tab = jnp.asarray([p[0] for p in pairs], jnp.int32)
    ki_tab = jnp.asarray([p[1] for p in pairs], jnp.int32)
    lam_init = 0.8 - 0.6 * math.exp(-0.3 * layer_idx)
    h = DA_HEADS
    slopes = np.asarray([2.0 ** (-8.0 * (i + 1) / h) for i in range(h)], np.float32)
    slopes = jnp.asarray(np.broadcast_to(slopes[:, None, None], (h, 1, blk)))
    t = batch * seq
    kern = functools.partial(_attn_kernel, blk=blk, lam_init=lam_init)
    grid_spec = pltpu.PrefetchScalarGridSpec(
        num_scalar_prefetch=2,
        grid=(batch, h, len(pairs)),
        in_specs=[
            pl.BlockSpec((blk, DA_V_DIM), lambda b, hh, p, qt, kt: (b * nq + qt[p], hh)),
            pl.BlockSpec((blk, DA_V_DIM), lambda b, hh, p, qt, kt: (b * nq + kt[p], h + hh)),
            pl.BlockSpec((blk, DA_V_DIM), lambda b, hh, p, qt, kt: (b * nq + kt[p], 2 * h + hh)),
            pl.BlockSpec((1, 1, blk), lambda b, hh, p, qt, kt: (hh, 0, 0)),
            pl.BlockSpec((4, DA_HEAD_DIM), lambda b, hh, p, qt, kt: (0, 0)),
            pl.BlockSpec((1, DA_V_DIM), lambda b, hh, p, qt, kt: (0, 0)),
        ],
        out_specs=pl.BlockSpec((blk, DA_V_DIM), lambda b, hh, p, qt, kt: (b * nq + qt[p], hh)),
        scratch_shapes=[
            pltpu.VMEM((2, blk, 1), F32),
            pltpu.VMEM((2, blk, 1), F32),
            pltpu.VMEM((2, blk, DA_V_DIM), F32),
        ],
    )
    return pl.pallas_call(
        kern,
        out_shape=jax.ShapeDtypeStruct((t, h * DA_V_DIM), BF16),
        grid_spec=grid_spec,
        compiler_params=_params(("parallel", "parallel", "arbitrary")),
        name="diff_attention",
    )(qi_tab, ki_tab, qkv, qkv, qkv, slopes, lam_params, norm_g.reshape(1, DA_V_DIM))


def _gla_kernel(pg_ref, sm_ref, w2_ref, b2_ref, g_ref, o_ref, st_ref, *, blk):
    n = pl.program_id(1)
    hk = GLA_HEADS * GLA_DK
    hv = GLA_HEADS * GLA_DV
    c_len = GLA_CHUNK

    @pl.when(n == 0)
    def _init():
        st_ref[...] = jnp.zeros(st_ref.shape, F32)

    lr = sm_ref[:, 0:LANES].astype(BF16)
    gl = _dot(lr, w2_ref[...]) + b2_ref[...]
    gk = (jnp.minimum(gl, 0.0) - jnp.log(1.0 + jnp.exp(-jnp.abs(gl)))) * (1.0 / GLA_TAU)
    r = lax.broadcasted_iota(jnp.int32, (blk, blk), 0)
    c = lax.broadcasted_iota(jnp.int32, (blk, blk), 1)
    tri = jnp.where((c <= r) & ((r // c_len) == (c // c_len)), 1.0, 0.0).astype(BF16)
    g1, g2, g3 = _split3(gk)
    bcum = _dot(tri, g1) + _dot(tri, g2) + _dot(tri, g3)

    rr = lax.broadcasted_iota(jnp.int32, (c_len, c_len), 0)
    cc = lax.broadcasted_iota(jnp.int32, (c_len, c_len), 1)
    causal = cc <= rr
    scale = GLA_DK ** -0.5
    for ci in range(blk // c_len):
        rows = slice(ci * c_len, (ci + 1) * c_len)
        for h in range(GLA_HEADS):
            kc = slice(h * GLA_DK, (h + 1) * GLA_DK)
            vc = slice(h * GLA_DV, (h + 1) * GLA_DV)
            b = bcum[rows, kc]
            b_last = b[c_len - 1:c_len, :]
            q = pg_ref[rows, kc]
            k = pg_ref[rows, hk + h * GLA_DK: hk + (h + 1) * GLA_DK]
            v = pg_ref[rows, 2 * hk + h * GLA_DV: 2 * hk + (h + 1) * GLA_DV].astype(BF16)
            rg = pg_ref[rows, 2 * hk + hv + h * GLA_DV: 2 * hk + hv + (h + 1) * GLA_DV]
            q_e = (q * scale * jnp.exp(b)).astype(BF16)
            k_e = (k * jnp.exp(-b)).astype(BF16)
            k_d = (k * jnp.exp(b_last - b)).astype(BF16)
            att = jnp.where(causal, _dot_nt(q_e, k_e), 0.0)
            st = st_ref[h]
            o = _dot(att.astype(BF16), v) + _dot_nt(q_e, st.astype(BF16))
            st_ref[h] = st * jnp.exp(b_last) + _dot_tn(v, k_d)
            o = o * lax.rsqrt(jnp.mean(o * o, axis=1, keepdims=True) + EPS) * g_ref[...]
            o_ref[rows, vc] = (o * _silu(rg)).astype(o_ref.dtype)


def _gla(pg, small, w2p, b2, norm_g, batch, seq):
    blk = min(GLA_BLOCK, seq)
    nb = seq // blk
    t = batch * seq
    hk = GLA_HEADS * GLA_DK
    hv = GLA_HEADS * GLA_DV
    width = pg.shape[1]
    return pl.pallas_call(
        functools.partial(_gla_kernel, blk=blk),
        out_shape=jax.ShapeDtypeStruct((t, hv), BF16),
        grid=(batch, nb),
        in_specs=[
            pl.BlockSpec((blk, width), lambda b, n: (b * nb + n, 0)),
            pl.BlockSpec((blk, small.shape[1]), lambda b, n: (b * nb + n, 0)),
            pl.BlockSpec((LANES, hk), lambda b, n: (0, 0)),
            pl.BlockSpec((1, hk), lambda b, n: (0, 0)),
            pl.BlockSpec((1, GLA_DV), lambda b, n: (0, 0)),
        ],
        out_specs=pl.BlockSpec((blk, hv), lambda b, n: (b * nb + n, 0)),
        scratch_shapes=[pltpu.VMEM((GLA_HEADS, GLA_DV, GLA_DK), F32)],
        compiler_params=_params(("parallel", "arbitrary")),
        name="gla",
    )(pg, small, w2p, b2, norm_g.reshape(1, GLA_DV))


def _ssd_kernel(ps_ref, sm_ref, cw_ref, cb_ref, dtb_ref, alog_ref, dsk_ref, g_ref, o_ref,
                tail_ref, st_ref):
    n = pl.program_id(1)
    q_len = SSD_CHUNK
    w = SSD_WIDTH
    gs = SSD_GROUPS * SSD_STATE
    heads_per_group = SSD_HEADS // SSD_GROUPS
    gw = heads_per_group * SSD_HEAD_DIM

    @pl.when(n == 0)
    def _init():
        tail_ref[...] = jnp.zeros(tail_ref.shape, F32)
        st_ref[...] = jnp.zeros(st_ref.shape, F32)

    cur = ps_ref[:, w:w + w + 2 * gs]
    ext = jnp.concatenate([tail_ref[...], cur], axis=0)
    acc = cb_ref[...] + cw_ref[0:1, :] * ext[SUBLANES - 3:SUBLANES - 3 + q_len]
    for i in range(1, SSD_CONV):
        off = SUBLANES - (SSD_CONV - 1) + i
        acc = acc + cw_ref[i:i + 1, :] * ext[off:off + q_len]
    tail_ref[...] = cur[q_len - SUBLANES:q_len]
    xbc = _silu(acc)
    xs = xbc[:, 0:w]
    bm = xbc[:, w:w + gs]
    cm = xbc[:, w + gs:w + 2 * gs]

    dtr = sm_ref[:, LANES:2 * LANES] + dtb_ref[...]
    dtv = jnp.maximum(dtr, 0.0) + jnp.log(1.0 + jnp.exp(-jnp.abs(dtr)))
    da = dtv * (-jnp.exp(alog_ref[...]))
    r = lax.broadcasted_iota(jnp.int32, (q_len, q_len), 0)
    c = lax.broadcasted_iota(jnp.int32, (q_len, q_len), 1)
    causal = c <= r
    tri = jnp.where(causal, 1.0, 0.0).astype(BF16)
    d1, d2, d3 = _split3(da)
    a_cs = _dot(tri, d1) + _dot(tri, d2) + _dot(tri, d3)
    a_cs_t = a_cs.T
    a_last = a_cs[q_len - 1:q_len, :]
    e_last = jnp.exp(a_last)

    for g in range(SSD_GROUPS):
        bm_g = bm[:, g * SSD_STATE:(g + 1) * SSD_STATE]
        cm_g = cm[:, g * SSD_STATE:(g + 1) * SSD_STATE]
        bm_b = bm_g.astype(BF16)
        cm_b = cm_g.astype(BF16)
        cb = _dot_nt(cm_b, bm_b)
        y_diag = []
        xdd = []
        e_col = []
        e_row = []
        for rh in range(heads_per_group):
            h = g * heads_per_group + rh
            col = a_cs[:, h:h + 1]
            row = a_cs_t[h:h + 1, :]
            lm = jnp.exp(jnp.where(causal, col - row, -jnp.inf))
            xdt = xs[:, h * SSD_HEAD_DIM:(h + 1) * SSD_HEAD_DIM] * dtv[:, h:h + 1]
            y_diag.append(_dot((cb * lm).astype(BF16), xdt.astype(BF16)))
            xdd.append(xdt * jnp.exp(a_last[:, h:h + 1] - col))
            e_col.append(jnp.broadcast_to(jnp.exp(col), (q_len, SSD_HEAD_DIM)))
            e_row.append(jnp.broadcast_to(e_last[:, h:h + 1], (1, SSD_HEAD_DIM)))
        y_diag = jnp.concatenate(y_diag, axis=1)
        xdd = jnp.concatenate(xdd, axis=1)
        e_col = jnp.concatenate(e_col, axis=1)
        e_row = jnp.concatenate(e_row, axis=1)
        s_prev = st_ref[g]
        y_off = _dot(cm_b, s_prev.astype(BF16)) * e_col
        st_ref[g] = s_prev * e_row + _dot_tn(bm_b, xdd.astype(BF16))
        lanes = slice(g * gw, (g + 1) * gw)
        y = y_diag + y_off + xs[:, lanes] * dsk_ref[:, lanes]
        y = y * _silu(ps_ref[:, lanes])
        y = y * lax.rsqrt(jnp.mean(y * y, axis=1, keepdims=True) + EPS) * g_ref[:, lanes]
        o_ref[:, lanes] = y.astype(o_ref.dtype)


def _ssd(ps, small, conv_w, conv_b, dt_bias, a_log, d_skip, norm_g, batch, seq):
    q_len = SSD_CHUNK
    nb = seq // q_len
    t = batch * seq
    conv_dim = conv_w.shape[1]
    pad = LANES - SSD_HEADS
    dtb = jnp.pad(dt_bias, (0, pad)).reshape(1, LANES)
    alog = jnp.pad(a_log, (0, pad)).reshape(1, LANES)
    dsk = jnp.repeat(d_skip, SSD_HEAD_DIM).reshape(1, SSD_WIDTH)
    const = lambda b, n: (0, 0)
    return pl.pallas_call(
        _ssd_kernel,
        out_shape=jax.ShapeDtypeStruct((t, SSD_WIDTH), BF16),
        grid=(batch, nb),
        in_specs=[
            pl.BlockSpec((q_len, ps.shape[1]), lambda b, n: (b * nb + n, 0)),
            pl.BlockSpec((q_len, small.shape[1]), lambda b, n: (b * nb + n, 0)),
            pl.BlockSpec((SSD_CONV, conv_dim), const),
            pl.BlockSpec((1, conv_dim), const),
            pl.BlockSpec((1, LANES), const),
            pl.BlockSpec((1, LANES), const),
            pl.BlockSpec((1, SSD_WIDTH), const),
            pl.BlockSpec((1, SSD_WIDTH), const),
        ],
        out_specs=pl.BlockSpec((q_len, SSD_WIDTH), lambda b, n: (b * nb + n, 0)),
        scratch_shapes=[
            pltpu.VMEM((SUBLANES, conv_dim), F32),
            pltpu.VMEM((SSD_GROUPS, SSD_STATE, SSD_WIDTH // SSD_GROUPS), F32),
        ],
        compiler_params=_params(("parallel", "arbitrary")),
        name="ssd",
    )(ps, small, conv_w, conv_b.reshape(1, conv_dim), dtb, alog, dsk, norm_g.reshape(1, SSD_WIDTH))


def _merge_kernel(x_ref, wg_ref, ba_ref, bb_ref, bc_ref, wb_ref, o_ref):
    x = x_ref[...]
    acc = None
    for i, br in enumerate((ba_ref, bb_ref, bc_ref)):
        gate = _sigmoid(_dot(x, wg_ref[i]))
        term = gate * _dot(br[...], wb_ref[i])
        acc = term if acc is None else acc + term
    o_ref[...] = acc.astype(o_ref.dtype)


def _merge(xb, wgate, br_a, br_b, br_c, wbr, tm, tn):
    t, d = xb.shape
    bw = br_a.shape[1]
    return pl.pallas_call(
        _merge_kernel,
        out_shape=jax.ShapeDtypeStruct((t, d), BF16),
        grid=(t // tm, d // tn),
        in_specs=[
            pl.BlockSpec((tm, d), lambda i, j: (i, 0)),
            pl.BlockSpec((N_BRANCH, d, tn), lambda i, j: (0, 0, j)),
            pl.BlockSpec((tm, bw), lambda i, j: (i, 0)),
            pl.BlockSpec((tm, bw), lambda i, j: (i, 0)),
            pl.BlockSpec((tm, bw), lambda i, j: (i, 0)),
            pl.BlockSpec((N_BRANCH, bw, tn), lambda i, j: (0, 0, j)),
        ],
        out_specs=pl.BlockSpec((tm, tn), lambda i, j: (i, j)),
        compiler_params=_params(("parallel", "arbitrary")),
        name="gated_merge",
    )(xb, wgate, br_a, br_b, br_c, wbr)


def _layernorm(v, g, b):
    mu = jnp.mean(v, axis=1, keepdims=True)
    d = v - mu
    var = jnp.mean(d * d, axis=1, keepdims=True)
    return d * lax.rsqrt(var + EPS) * g + b


def _outproj_ln_kernel(m_ref, w_ref, x_ref, g_ref, b_ref, o_ref, ob_ref, *, alpha):
    mix = _dot(m_ref[...], w_ref[...])
    y = _layernorm(alpha * x_ref[...] + mix, g_ref[...], b_ref[...])
    o_ref[...] = y
    ob_ref[...] = y.astype(BF16)


def _outproj_ln(merged, w_out, x, g, b, alpha, tm):
    t, d = x.shape
    const = lambda i: (0, 0)
    return pl.pallas_call(
        functools.partial(_outproj_ln_kernel, alpha=alpha),
        out_shape=(jax.ShapeDtypeStruct((t, d), F32), jax.ShapeDtypeStruct((t, d), BF16)),
        grid=(t // tm,),
        in_specs=[
            pl.BlockSpec((tm, d), lambda i: (i, 0)),
            pl.BlockSpec((d, d), const),
            pl.BlockSpec((tm, d), lambda i: (i, 0)),
            pl.BlockSpec((1, d), const),
            pl.BlockSpec((1, d), const),
        ],
        out_specs=(pl.BlockSpec((tm, d), lambda i: (i, 0)), pl.BlockSpec((tm, d), lambda i: (i, 0))),
        compiler_params=_params(("parallel",)),
        name="outproj_layernorm",
    )(merged, w_out, x, g.reshape(1, d), b.reshape(1, d))


def _router_kernel(x_ref, w_ref, bias_ref, eidx_ref, wt_ref, pos_ref, cnt_ref, carry_ref, *, tm):
    i = pl.program_id(0)
    ne = N_EXPERTS
    ng = N_EXPERT_GROUPS
    per = ne // ng

    @pl.when(i == 0)
    def _init():
        carry_ref[...] = jnp.zeros(carry_ref.shape, F32)

    x = x_ref[...]
    xh = x.astype(BF16)
    xl = (x - xh.astype(F32)).astype(BF16)
    w = w_ref[...]
    wh = w.astype(BF16)
    wl = (w - wh.astype(F32)).astype(BF16)
    logits = _dot_nt(wh, xh) + _dot_nt(wh, xl) + _dot_nt(wl, xh)
    scores = _sigmoid(logits)
    sel = scores + bias_ref[...]

    grp = sel.reshape(ng, per, tm)
    io_p = lax.broadcasted_iota(jnp.int32, (ng, per, tm), 1)
    m1 = jnp.max(grp, axis=1, keepdims=True)
    i1 = jnp.min(jnp.where(grp == m1, io_p, per), axis=1, keepdims=True)
    m2 = jnp.max(jnp.where(io_p == i1, -jnp.inf, grp), axis=1, keepdims=True)
    gscore = (m1 + m2).reshape(ng, tm)

    io_g = lax.broadcasted_iota(jnp.int32, (ng, tm), 0)
    gsel = jnp.zeros((ng, tm), F32)
    gwork = gscore
    for _ in range(TOPK_GROUPS):
        gm = jnp.max(gwork, axis=0, keepdims=True)
        gi = jnp.min(jnp.where(gwork == gm, io_g, ng), axis=0, keepdims=True)
        hit = io_g == gi
        gsel = jnp.where(hit, 1.0, gsel)
        gwork = jnp.where(hit, -jnp.inf, gwork)
    emask = jnp.broadcast_to(gsel.reshape(ng, 1, tm), (ng, per, tm)).reshape(ne, tm) > 0.5

    io_e = lax.broadcasted_iota(jnp.int32, (ne, tm), 0)
    work = jnp.where(emask, sel, -jnp.inf)
    member = jnp.zeros((ne, tm), F32)
    idx_rows = []
    w_rows = []
    for _ in range(TOP_K):
        mx = jnp.max(work, axis=0, keepdims=True)
        ei = jnp.min(jnp.where(work == mx, io_e, ne), axis=0, keepdims=True)
        hit = io_e == ei
        idx_rows.append(ei)
        w_rows.append(jnp.sum(jnp.where(hit, scores, 0.0), axis=0, keepdims=True))
        member = jnp.where(hit, 1.0, member)
        work = jnp.where(hit, -jnp.inf, work)
    wsum = w_rows[0]
    for wr in w_rows[1:]:
        wsum = wsum + wr
    inv = ROUTED_SCALE / wsum

    r = lax.broadcasted_iota(jnp.int32, (tm, tm), 0)
    c = lax.broadcasted_iota(jnp.int32, (tm, tm), 1)
    upper = jnp.where(r < c, 1.0, 0.0).astype(BF16)
    prefix = _dot(member.astype(BF16), upper) + carry_ref[...]
    for j in range(TOP_K):
        hit = io_e == idx_rows[j]
        eidx_ref[j:j + 1, :] = idx_rows[j]
        wt_ref[j:j + 1, :] = w_rows[j] * inv
        pos_ref[j:j + 1, :] = jnp.sum(jnp.where(hit, prefix, 0.0), axis=0, keepdims=True).astype(jnp.int32)
    carry = carry_ref[...] + jnp.sum(member, axis=1, keepdims=True)
    carry_ref[...] = carry
    cnt_ref[...] = jnp.broadcast_to(carry, cnt_ref.shape)


def _router(x, router_w, router_bias, tm):
    t, d = x.shape
    ne = N_EXPERTS
    return pl.pallas_call(
        functools.partial(_router_kernel, tm=tm),
        out_shape=(
            jax.ShapeDtypeStruct((TOP_K, t), jnp.int32),
            jax.ShapeDtypeStruct((TOP_K, t), F32),
            jax.ShapeDtypeStruct((TOP_K, t), jnp.int32),
            jax.ShapeDtypeStruct((ne, LANES), F32),
        ),
        grid=(t // tm,),
        in_specs=[
            pl.BlockSpec((tm, d), lambda i: (i, 0)),
            pl.BlockSpec((ne, d), lambda i: (0, 0)),
            pl.BlockSpec((ne, 1), lambda i: (0, 0)),
        ],
        out_specs=(
            pl.BlockSpec((TOP_K, tm), lambda i: (0, i)),
            pl.BlockSpec((TOP_K, tm), lambda i: (0, i)),
            pl.BlockSpec((TOP_K, tm), lambda i: (0, i)),
            pl.BlockSpec((ne, LANES), lambda i: (0, 0)),
        ),
        scratch_shapes=[pltpu.VMEM((ne, 1), F32)],
        compiler_params=_params(("arbitrary",)),
        name="router",
    )(x, router_w.T, router_bias.reshape(ne, 1))


def _experts_kernel(blk_e, nvalid, xs_ref, wg_ref, wu_ref, wd_ref, o_ref, wgu_s, wd_s):
    i = pl.program_id(0)
    f = wg_ref.shape[-1]

    @pl.when(i < nvalid[0])
    def _():
        e = blk_e[i]
        prev = blk_e[jnp.maximum(i - 1, 0)]

        @pl.when((i == 0) | (e != prev))
        def _cast():
            wgu_s[:, 0:f] = wg_ref[...].astype(BF16)
            wgu_s[:, f:2 * f] = wu_ref[...].astype(BF16)
            wd_s[...] = wd_ref[...].astype(BF16)

        gu = _dot(xs_ref[...], wgu_s[...])
        hid = _silu(gu[:, 0:f]) * gu[:, f:2 * f]
        o_ref[...] = _dot(hid.astype(BF16), wd_s[...])


def _experts(xs, blk_e, nvalid, w_gate, w_up, w_down, layer):
    p, d = xs.shape
    f = w_gate.shape[-1]
    bm = MOE_BLOCK
    row = lambda i, be, nv: (jnp.minimum(i, nv[0] - 1), 0)
    wmap = lambda i, be, nv: (layer, be[i], 0, 0)
    grid_spec = pltpu.PrefetchScalarGridSpec(
        num_scalar_prefetch=2,
        grid=(p // bm,),
        in_specs=[
            pl.BlockSpec((bm, d), row),
            pl.BlockSpec((None, None, d, f), wmap),
            pl.BlockSpec((None, None, d, f), wmap),
            pl.BlockSpec((None, None, f, d), wmap),
        ],
        out_specs=pl.BlockSpec((bm, d), row),
        scratch_shapes=[pltpu.VMEM((d, 2 * f), BF16), pltpu.VMEM((f, d), BF16)],
    )
    return pl.pallas_call(
        _experts_kernel,
        out_shape=jax.ShapeDtypeStruct((p, d), F32),
        grid_spec=grid_spec,
        compiler_params=_params(("arbitrary",)),
        name="routed_experts",
    )(blk_e, nvalid, xs, w_gate, w_up, w_down)


def _shared_ln_kernel(x_ref, xb_ref, r_ref, wgu_ref, wd_ref, g_ref, b_ref, o_ref, ob_ref, *, alpha):
    f = wd_ref.shape[0]
    gu = _dot(xb_ref[...], wgu_ref[...])
    hid = _silu(gu[:, 0:f]) * gu[:, f:2 * f]
    ffn = _dot(hid.astype(BF16), wd_ref[...]) + r_ref[...]
    y = _layernorm(alpha * x_ref[...] + ffn, g_ref[...], b_ref[...])
    o_ref[...] = y
    ob_ref[...] = y.astype(BF16)


def _shared_ln(x, xb, routed, wgu, wd, g, b, alpha, tm):
    t, d = x.shape
    f = wd.shape[0]
    const = lambda i: (0, 0)
    tok = lambda i: (i, 0)
    return pl.pallas_call(
        functools.partial(_shared_ln_kernel, alpha=alpha),
        out_shape=(jax.ShapeDtypeStruct((t, d), F32), jax.ShapeDtypeStruct((t, d), BF16)),
        grid=(t // tm,),
        in_specs=[
            pl.BlockSpec((tm, d), tok),
            pl.BlockSpec((tm, d), tok),
            pl.BlockSpec((tm, d), tok),
            pl.BlockSpec((d, 2 * f), const),
            pl.BlockSpec((f, d), const),
            pl.BlockSpec((1, d), const),
            pl.BlockSpec((1, d), const),
        ],
        out_specs=(pl.BlockSpec((tm, d), tok), pl.BlockSpec((tm, d), tok)),
        compiler_params=_params(("parallel",)),
        name="shared_expert_layernorm",
    )(x, xb, routed, wgu, wd, g.reshape(1, d), b.reshape(1, d))


def _moe_layout(eidx, pos, cnt):
    t = eidx.shape[1]
    counts = cnt[:, 0].astype(jnp.int32)
    padded = (counts + MOE_BLOCK - 1) // MOE_BLOCK * MOE_BLOCK
    pend = jnp.cumsum(padded)
    pstart = pend - padded
    dest = pstart[eidx] + pos
    nblk = (t * TOP_K + N_EXPERTS * MOE_BLOCK) // MOE_BLOCK
    nvalid = (pend[-1] // MOE_BLOCK).astype(jnp.int32)
    starts = jnp.minimum(jnp.arange(nblk, dtype=jnp.int32), nvalid - 1) * MOE_BLOCK
    blk_e = jnp.minimum(jnp.searchsorted(pend, starts, side="right"), N_EXPERTS - 1).astype(jnp.int32)
    return dest, blk_e, nvalid.reshape(1), nblk


def kernel(x, w_in, da_lambda, da_norm_g, gla_gate_w2, gla_gate_b, gla_norm_g, ssd_conv_w, ssd_conv_b,
           ssd_dt_bias, ssd_a_log, ssd_d, ssd_norm_g, w_branch, w_out, ln1_g, ln1_b, router_w, router_bias,
           exp_w_gate, exp_w_up, exp_w_down, sh_w_gate, sh_w_up, sh_w_down, ln2_g, ln2_b):
    batch, seq, d = x.shape
    depth = w_in.shape[0]
    t = batch * seq
    alpha = (2 * depth) ** 0.25

    da_w = DA_HEADS * DA_V_DIM
    gk_w = GLA_HEADS * GLA_DK
    gv_w = GLA_HEADS * GLA_DV
    conv_dim = SSD_WIDTH + 2 * SSD_GROUPS * SSD_STATE
    sizes = (da_w, da_w, da_w, gk_w, gk_w, gv_w, GLA_GATE_RANK, gv_w, SSD_WIDTH, conv_dim, SSD_HEADS,
             N_BRANCH * d)
    offs = np.concatenate([[0], np.cumsum(sizes)]).tolist()
    (o_daq, o_dak, o_dav, o_gq, o_gk, o_gv, o_glr, o_gr, o_sz, o_sx, o_sdt, o_mg, o_end) = offs

    ones = lambda n: jnp.ones((1, n), F32)
    qkv_scale = jnp.concatenate([jnp.full((1, da_w), DA_HEAD_DIM ** -0.5, F32), ones(2 * da_w)], axis=1)

    xf = x.reshape(t, d)
    xb = xf.astype(BF16)
    for l in range(depth):
        wl = w_in[l]
        w_qkv = wl[:, o_daq:o_gq].astype(BF16)
        w_gla = jnp.concatenate([wl[:, o_gq:o_glr], wl[:, o_gr:o_sz]], axis=1).astype(BF16)
        w_ssd = wl[:, o_sz:o_sdt].astype(BF16)
        w_small = jnp.concatenate([
            jnp.pad(wl[:, o_glr:o_gr], ((0, 0), (0, LANES - GLA_GATE_RANK))),
            jnp.pad(wl[:, o_sdt:o_mg], ((0, 0), (0, LANES - SSD_HEADS)))], axis=1).astype(BF16)
        w_mg = wl[:, o_mg:o_end].reshape(d, N_BRANCH, d).transpose(1, 0, 2).astype(BF16)

        qkv = _matmul(xb, w_qkv, qkv_scale, BF16, 512, 1024, "proj_attn")
        pg = _matmul(xb, w_gla, ones(w_gla.shape[1]), F32, 512, 1024, "proj_gla")
        ps = _matmul(xb, w_ssd, ones(w_ssd.shape[1]), F32, 512, 1024, "proj_ssd")
        small = _matmul(xb, w_small, ones(2 * LANES), F32, 512, 2 * LANES, "proj_small")

        br_a = _diff_attention(qkv, da_lambda[l], da_norm_g[l], l, batch, seq)
        w2p = jnp.pad(gla_gate_w2[l], ((0, LANES - GLA_GATE_RANK), (0, 0))).astype(BF16)
        br_b = _gla(pg, small, w2p, gla_gate_b[l].reshape(1, gk_w), gla_norm_g[l], batch, seq)
        br_c = _ssd(ps, small, ssd_conv_w[l], ssd_conv_b[l], ssd_dt_bias[l], ssd_a_log[l], ssd_d[l],
                    ssd_norm_g[l], batch, seq)

        merged = _merge(xb, w_mg, br_a, br_b, br_c, w_branch[l].astype(BF16), 512, 512)
        xf, xb = _outproj_ln(merged, w_out[l].astype(BF16), xf, ln1_g[l], ln1_b[l], alpha, TOKEN_TILE)

        eidx, wts, pos, cnt = _router(xf, router_w[l], router_bias[l], TOKEN_TILE)
        dest, blk_e, nvalid, nblk = _moe_layout(eidx, pos, cnt)
        tok = jnp.broadcast_to(jnp.arange(t, dtype=jnp.int32)[None, :], (TOP_K, t))
        tok_buf = jnp.zeros((nblk * MOE_BLOCK,), jnp.int32).at[dest.reshape(-1)].set(tok.reshape(-1))
        xs = jnp.take(xb, tok_buf, axis=0)
        ys = _experts(xs, blk_e, nvalid, exp_w_gate, exp_w_up, exp_w_down, l)
        routed = jnp.sum(jnp.take(ys, dest, axis=0) * wts[:, :, None], axis=0)
        wgu = jnp.concatenate([sh_w_gate[l], sh_w_up[l]], axis=1).astype(BF16)
        xf, xb = _shared_ln(xf, xb, routed, wgu, sh_w_down[l].astype(BF16), ln2_g[l], ln2_b[l], alpha,
                            TOKEN_TILE)
    return xf.reshape(batch, seq, d)
```

```python
import functools
import math

import jax
import jax.numpy as jnp
import numpy as np
from jax import lax
from jax.experimental import pallas as pl
from jax.experimental.pallas import tpu as pltpu

F32 = jnp.float32
BF16 = jnp.bfloat16

DA_HEADS = 4
DA_HEAD_DIM = 128
DA_V_DIM = 2 * DA_HEAD_DIM
GLA_HEADS = 4
GLA_DK = 128
GLA_DV = 256
GLA_GATE_RANK = 16
GLA_TAU = 16.0
GLA_CHUNK = 64
SSD_HEADS = 16
SSD_HEAD_DIM = 64
SSD_STATE = 128
SSD_GROUPS = 4
SSD_CONV = 4
SSD_CHUNK = 128
SSD_WIDTH = SSD_HEADS * SSD_HEAD_DIM
N_BRANCH = 3
BRANCH_WIDTH = 1024
N_EXPERTS = 64
TOP_K = 8
N_EXPERT_GROUPS = 8
TOPK_GROUPS = 4
ROUTED_SCALE = 2.5
EPS = 1e-5

LANES = 128
SUBLANES = 8
VMEM_LIMIT = 52 * 1024 * 1024

ATTN_BLOCK = 512
GLA_BLOCK = 512
MOE_BLOCK = 512
TOKEN_TILE = 256


def _params(sem):
    return pltpu.CompilerParams(dimension_semantics=sem, vmem_limit_bytes=VMEM_LIMIT)


def _silu(x):
    return x * (1.0 / (1.0 + jnp.exp(-x)))


def _sigmoid(x):
    return 1.0 / (1.0 + jnp.exp(-x))


def _split3(x):
    h1 = x.astype(BF16)
    r1 = x - h1.astype(F32)
    h2 = r1.astype(BF16)
    r2 = r1 - h2.astype(F32)
    return h1, h2, r2.astype(BF16)


def _dot(a, b):
    return jnp.dot(a, b, preferred_element_type=F32)


def _dot_nt(a, b):
    return lax.dot_general(a, b, (((1,), (1,)), ((), ())), preferred_element_type=F32)


def _dot_tn(a, b):
    return lax.dot_general(a, b, (((0,), (0,)), ((), ())), preferred_element_type=F32)


def _mm_kernel(a_ref, w_ref, s_ref, o_ref):
    acc = _dot(a_ref[...], w_ref[...])
    o_ref[...] = (acc * s_ref[...]).astype(o_ref.dtype)


def _matmul(a, w, col_scale, out_dtype, tm, tn, name):
    m, k = a.shape
    n = w.shape[1]
    return pl.pallas_call(
        _mm_kernel,
        out_shape=jax.ShapeDtypeStruct((m, n), out_dtype),
        grid=(m // tm, n // tn),
        in_specs=[
            pl.BlockSpec((tm, k), lambda i, j: (i, 0)),
            pl.BlockSpec((k, tn), lambda i, j: (0, j)),
            pl.BlockSpec((1, tn), lambda i, j: (0, j)),
        ],
        out_specs=pl.BlockSpec((tm, tn), lambda i, j: (i, j)),
        compiler_params=_params(("parallel", "arbitrary")),
        name=name,
    )(a, w, col_scale)


def _attn_kernel(qi_tab, ki_tab, q_ref, k_ref, v_ref, slope_ref, lam_ref, g_ref, o_ref,
                 m_ref, l_ref, acc_ref, *, blk, lam_init):
    p = pl.program_id(2)
    qi = qi_tab[p]
    ki = ki_tab[p]

    @pl.when(ki == 0)
    def _init():
        m_ref[...] = jnp.full(m_ref.shape, -jnp.inf, F32)
        l_ref[...] = jnp.zeros(l_ref.shape, F32)
        acc_ref[...] = jnp.zeros(acc_ref.shape, F32)

    col = lax.broadcasted_iota(jnp.int32, (1, blk), 1)
    rel = (col + (ki - qi) * blk).astype(F32)
    bias = slope_ref[0] * rel

    def step(masked):
        q = q_ref[...]
        k = k_ref[...]
        v = v_ref[...]
        if masked:
            r = lax.broadcasted_iota(jnp.int32, (blk, blk), 0)
            c = lax.broadcasted_iota(jnp.int32, (blk, blk), 1)
            keep = c <= r
        for mp in range(2):
            sl = slice(mp * DA_HEAD_DIM, (mp + 1) * DA_HEAD_DIM)
            s = _dot_nt(q[:, sl], k[:, sl]) + bias
            if masked:
                s = jnp.where(keep, s, -jnp.inf)
            m_old = m_ref[mp]
            m_new = jnp.maximum(m_old, jnp.max(s, axis=1, keepdims=True))
            alpha = jnp.exp(m_old - m_new)
            pr = jnp.exp(s - m_new)
            l_ref[mp] = alpha * l_ref[mp] + jnp.sum(pr, axis=1, keepdims=True)
            acc_ref[mp] = alpha * acc_ref[mp] + _dot(pr.astype(BF16), v)
            m_ref[mp] = m_new

    @pl.when(ki < qi)
    def _off_diag():
        step(False)

    @pl.when(ki == qi)
    def _diag():
        step(True)
        lp = lam_ref[...]
        lam = (jnp.exp(jnp.sum(lp[0:1] * lp[1:2], axis=1, keepdims=True))
               - jnp.exp(jnp.sum(lp[2:3] * lp[3:4], axis=1, keepdims=True)) + lam_init)
        o = acc_ref[0] / l_ref[0] - lam * (acc_ref[1] / l_ref[1])
        o = o * lax.rsqrt(jnp.mean(o * o, axis=1, keepdims=True) + EPS)
        o_ref[...] = (o * g_ref[...] * (1.0 - lam_init)).astype(o_ref.dtype)


def _diff_attention(qkv, lam_params, norm_g, layer_idx, batch, seq):
    blk = min(ATTN_BLOCK, seq)
    nq = seq // blk
    pairs = [(i, j) for i in range(nq) for j in range(i + 1)]
    qi_tab = jnp.asarray([p[0] for p in pairs], jnp.int32)
    ki_tab = jnp.asarray([p[1] for p in pairs], jnp.int32)
    lam_init = 0.8 - 0.6 * math.exp(-0.3 * layer_idx)
    h = DA_HEADS
    slopes = np.asarray([2.0 ** (-8.0 * (i + 1) / h) for i in range(h)], np.float32)
    slopes = jnp.asarray(np.broadcast_to(slopes[:, None, None], (h, 1, blk)))
    t = batch * seq
    kern = functools.partial(_attn_kernel, blk=blk, lam_init=lam_init)
    grid_spec = pltpu.PrefetchScalarGridSpec(
        num_scalar_prefetch=2,
        grid=(batch, h, len(pairs)),
        in_specs=[
            pl.BlockSpec((blk, DA_V_DIM), lambda b, hh, p, qt, kt: (b * nq + qt[p], hh)),
            pl.BlockSpec((blk, DA_V_DIM), lambda b, hh, p, qt, kt: (b * nq + kt[p], h + hh)),
            pl.BlockSpec((blk, DA_V_DIM), lambda b, hh, p, qt, kt: (b * nq + kt[p], 2 * h + hh)),
            pl.BlockSpec((1, 1, blk), lambda b, hh, p, qt, kt: (hh, 0, 0)),
            pl.BlockSpec((4, DA_HEAD_DIM), lambda b, hh, p, qt, kt: (0, 0)),
            pl.BlockSpec((1, DA_V_DIM), lambda b, hh, p, qt, kt: (0, 0)),
        ],
        out_specs=pl.BlockSpec((blk, DA_V_DIM), lambda b, hh, p, qt, kt: (b * nq + qt[p], hh)),
        scratch_shapes=[
            pltpu.VMEM((2, blk, 1), F32),
            pltpu.VMEM((2, blk, 1), F32),
            pltpu.VMEM((2, blk, DA_V_DIM), F32),
        ],
    )
    return pl.pallas_call(
        kern,
        out_shape=jax.ShapeDtypeStruct((t, h * DA_V_DIM), BF16),
        grid_spec=grid_spec,
        compiler_params=_params(("parallel", "parallel", "arbitrary")),
        name="diff_attention",
    )(qi_tab, ki_tab, qkv, qkv, qkv, slopes, lam_params, norm_g.reshape(1, DA_V_DIM))


def _gla_kernel(pg_ref, sm_ref, w2_ref, b2_ref, g_ref, o_ref, st_ref, *, blk):
    n = pl.program_id(1)
    hk = GLA_HEADS * GLA_DK
    hv = GLA_HEADS * GLA_DV
    c_len = GLA_CHUNK

    @pl.when(n == 0)
    def _init():
        st_ref[...] = jnp.zeros(st_ref.shape, F32)

    lr = sm_ref[:, 0:LANES].astype(BF16)
    gl = _dot(lr, w2_ref[...]) + b2_ref[...]
    gk = (jnp.minimum(gl, 0.0) - jnp.log(1.0 + jnp.exp(-jnp.abs(gl)))) * (1.0 / GLA_TAU)
    r = lax.broadcasted_iota(jnp.int32, (blk, blk), 0)
    c = lax.broadcasted_iota(jnp.int32, (blk, blk), 1)
    tri = jnp.where((c <= r) & ((r // c_len) == (c // c_len)), 1.0, 0.0).astype(BF16)
    g1, g2, g3 = _split3(gk)
    bcum = _dot(tri, g1) + _dot(tri, g2) + _dot(tri, g3)

    rr = lax.broadcasted_iota(jnp.int32, (c_len, c_len), 0)
    cc = lax.broadcasted_iota(jnp.int32, (c_len, c_len), 1)
    causal = cc <= rr
    scale = GLA_DK ** -0.5
    for ci in range(blk // c_len):
        rows = slice(ci * c_len, (ci + 1) * c_len)
        for h in range(GLA_HEADS):
            kc = slice(h * GLA_DK, (h + 1) * GLA_DK)
            vc = slice(h * GLA_DV, (h + 1) * GLA_DV)
            b = bcum[rows, kc]
            b_last = b[c_len - 1:c_len, :]
            q = pg_ref[rows, kc]
            k = pg_ref[rows, hk + h * GLA_DK: hk + (h + 1) * GLA_DK]
            v = pg_ref[rows, 2 * hk + h * GLA_DV: 2 * hk + (h + 1) * GLA_DV].astype(BF16)
            rg = pg_ref[rows, 2 * hk + hv + h * GLA_DV: 2 * hk + hv + (h + 1) * GLA_DV]
            q_e = (q * scale * jnp.exp(b)).astype(BF16)
            k_e = (k * jnp.exp(-b)).astype(BF16)
            k_d = (k * jnp.exp(b_last - b)).astype(BF16)
            att = jnp.where(causal, _dot_nt(q_e, k_e), 0.0)
            st = st_ref[h]
            o = _dot(att.astype(BF16), v) + _dot_nt(q_e, st.astype(BF16))
            st_ref[h] = st * jnp.exp(b_last) + _dot_tn(v, k_d)
            o = o * lax.rsqrt(jnp.mean(o * o, axis=1, keepdims=True) + EPS) * g_ref[...]
            o_ref[rows, vc] = (o * _silu(rg)).astype(o_ref.dtype)


def _gla(pg, small, w2p, b2, norm_g, batch, seq):
    blk = min(GLA_BLOCK, seq)
    nb = seq // blk
    t = batch * seq
    hk = GLA_HEADS * GLA_DK
    hv = GLA_HEADS * GLA_DV
    width = pg.shape[1]
    return pl.pallas_call(
        functools.partial(_gla_kernel, blk=blk),
        out_shape=jax.ShapeDtypeStruct((t, hv), BF16),
        grid=(batch, nb),
        in_specs=[
            pl.BlockSpec((blk, width), lambda b, n: (b * nb + n, 0)),
            pl.BlockSpec((blk, small.shape[1]), lambda b, n: (b * nb + n, 0)),
            pl.BlockSpec((LANES, hk), lambda b, n: (0, 0)),
            pl.BlockSpec((1, hk), lambda b, n: (0, 0)),
            pl.BlockSpec((1, GLA_DV), lambda b, n: (0, 0)),
        ],
        out_specs=pl.BlockSpec((blk, hv), lambda b, n: (b * nb + n, 0)),
        scratch_shapes=[pltpu.VMEM((GLA_HEADS, GLA_DV, GLA_DK), F32)],
        compiler_params=_params(("parallel", "arbitrary")),
        name="gla",
    )(pg, small, w2p, b2, norm_g.reshape(1, GLA_DV))


def _ssd_kernel(ps_ref, sm_ref, cw_ref, cb_ref, dtb_ref, alog_ref, dsk_ref, g_ref, o_ref,
                tail_ref, st_ref):
    n = pl.program_id(1)
    q_len = SSD_CHUNK
    w = SSD_WIDTH
    gs = SSD_GROUPS * SSD_STATE
    heads_per_group = SSD_HEADS // SSD_GROUPS
    gw = heads_per_group * SSD_HEAD_DIM

    @pl.when(n == 0)
    def _init():
        tail_ref[...] = jnp.zeros(tail_ref.shape, F32)
        st_ref[...] = jnp.zeros(st_ref.shape, F32)

    cur = ps_ref[:, w:w + w + 2 * gs]
    ext = jnp.concatenate([tail_ref[...], cur], axis=0)
    acc = cb_ref[...] + cw_ref[0:1, :] * ext[SUBLANES - 3:SUBLANES - 3 + q_len]
    for i in range(1, SSD_CONV):
        off = SUBLANES - (SSD_CONV - 1) + i
        acc = acc + cw_ref[i:i + 1, :] * ext[off:off + q_len]
    tail_ref[...] = cur[q_len - SUBLANES:q_len]
    xbc = _silu(acc)
    xs = xbc[:, 0:w]
    bm = xbc[:, w:w + gs]
    cm = xbc[:, w + gs:w + 2 * gs]

    dtr = sm_ref[:, LANES:2 * LANES] + dtb_ref[...]
    dtv = jnp.maximum(dtr, 0.0) + jnp.log(1.0 + jnp.exp(-jnp.abs(dtr)))
    da = dtv * (-jnp.exp(alog_ref[...]))
    r = lax.broadcasted_iota(jnp.int32, (q_len, q_len), 0)
    c = lax.broadcasted_iota(jnp.int32, (q_len, q_len), 1)
    causal = c <= r
    tri = jnp.where(causal, 1.0, 0.0).astype(BF16)
    d1, d2, d3 = _split3(da)
    a_cs = _dot(tri, d1) + _dot(tri, d2) + _dot(tri, d3)
    a_cs_t = a_cs.T
    a_last = a_cs[q_len - 1:q_len, :]
    e_last = jnp.exp(a_last)

    for g in range(SSD_GROUPS):
        bm_g = bm[:, g * SSD_STATE:(g + 1) * SSD_STATE]
        cm_g = cm[:, g * SSD_STATE:(g + 1) * SSD_STATE]
        bm_b = bm_g.astype(BF16)
        cm_b = cm_g.astype(BF16)
        cb = _dot_nt(cm_b, bm_b)
        y_diag = []
        xdd = []
        e_col = []
        e_row = []
        for rh in range(heads_per_group):
            h = g * heads_per_group + rh
            col = a_cs[:, h:h + 1]
            row = a_cs_t[h:h + 1, :]
            lm = jnp.exp(jnp.where(causal, col - row, -jnp.inf))
            xdt = xs[:, h * SSD_HEAD_DIM:(h + 1) * SSD_HEAD_DIM] * dtv[:, h:h + 1]
            y_diag.append(_dot((cb * lm).astype(BF16), xdt.astype(BF16)))
            xdd.append(xdt * jnp.exp(a_last[:, h:h + 1] - col))
            e_col.append(jnp.broadcast_to(jnp.exp(col), (q_len, SSD_HEAD_DIM)))
            e_row.append(jnp.broadcast_to(e_last[:, h:h + 1], (1, SSD_HEAD_DIM)))
        y_diag = jnp.concatenate(y_diag, axis=1)
        xdd = jnp.concatenate(xdd, axis=1)
        e_col = jnp.concatenate(e_col, axis=1)
        e_row = jnp.concatenate(e_row, axis=1)
        s_prev = st_ref[g]
        y_off = _dot(cm_b, s_prev.astype(BF16)) * e_col
        st_ref[g] = s_prev * e_row + _dot_tn(bm_b, xdd.astype(BF16))
        lanes = slice(g * gw, (g + 1) * gw)
        y = y_diag + y_off + xs[:, lanes] * dsk_ref[:, lanes]
        y = y * _silu(ps_ref[:, lanes])
        y = y * lax.rsqrt(jnp.mean(y * y, axis=1, keepdims=True) + EPS) * g_ref[:, lanes]
        o_ref[:, lanes] = y.astype(o_ref.dtype)


def _ssd(ps, small, conv_w, conv_b, dt_bias, a_log, d_skip, norm_g, batch, seq):
    q_len = SSD_CHUNK
    nb = seq // q_len
    t = batch * seq
    conv_dim = conv_w.shape[1]
    pad = LANES - SSD_HEADS
    dtb = jnp.pad(dt_bias, (0, pad)).reshape(1, LANES)
    alog = jnp.pad(a_log, (0, pad)).reshape(1, LANES)
    dsk = jnp.repeat(d_skip, SSD_HEAD_DIM).reshape(1, SSD_WIDTH)
    const = lambda b, n: (0, 0)
    return pl.pallas_call(
        _ssd_kernel,
        out_shape=jax.ShapeDtypeStruct((t, SSD_WIDTH), BF16),
        grid=(batch, nb),
        in_specs=[
            pl.BlockSpec((q_len, ps.shape[1]), lambda b, n: (b * nb + n, 0)),
            pl.BlockSpec((q_len, small.shape[1]), lambda b, n: (b * nb + n, 0)),
            pl.BlockSpec((SSD_CONV, conv_dim), const),
            pl.BlockSpec((1, conv_dim), const),
            pl.BlockSpec((1, LANES), const),
            pl.BlockSpec((1, LANES), const),
            pl.BlockSpec((1, SSD_WIDTH), const),
            pl.BlockSpec((1, SSD_WIDTH), const),
        ],
        out_specs=pl.BlockSpec((q_len, SSD_WIDTH), lambda b, n: (b * nb + n, 0)),
        scratch_shapes=[
            pltpu.VMEM((SUBLANES, conv_dim), F32),
            pltpu.VMEM((SSD_GROUPS, SSD_STATE, SSD_WIDTH // SSD_GROUPS), F32),
        ],
        compiler_params=_params(("parallel", "arbitrary")),
        name="ssd",
    )(ps, small, conv_w, conv_b.reshape(1, conv_dim), dtb, alog, dsk, norm_g.reshape(1, SSD_WIDTH))


def _merge_kernel(x_ref, wg_ref, ba_ref, bb_ref, bc_ref, wb_ref, o_ref):
    x = x_ref[...]
    acc = None
    for i, br in enumerate((ba_ref, bb_ref, bc_ref)):
        gate = _sigmoid(_dot(x, wg_ref[i]))
        term = gate * _dot(br[...], wb_ref[i])
        acc = term if acc is None else acc + term
    o_ref[...] = acc.astype(o_ref.dtype)


def _merge(xb, wgate, br_a, br_b, br_c, wbr, tm, tn):
    t, d = xb.shape
    bw = br_a.shape[1]
    return pl.pallas_call(
        _merge_kernel,
        out_shape=jax.ShapeDtypeStruct((t, d), BF16),
        grid=(t // tm, d // tn),
        in_specs=[
            pl.BlockSpec((tm, d), lambda i, j: (i, 0)),
            pl.BlockSpec((N_BRANCH, d, tn), lambda i, j: (0, 0, j)),
            pl.BlockSpec((tm, bw), lambda i, j: (i, 0)),
            pl.BlockSpec((tm, bw), lambda i, j: (i, 0)),
            pl.BlockSpec((tm, bw), lambda i, j: (i, 0)),
            pl.BlockSpec((N_BRANCH, bw, tn), lambda i, j: (0, 0, j)),
        ],
        out_specs=pl.BlockSpec((tm, tn), lambda i, j: (i, j)),
        compiler_params=_params(("parallel", "arbitrary")),
        name="gated_merge",
    )(xb, wgate, br_a, br_b, br_c, wbr)


def _layernorm(v, g, b):
    mu = jnp.mean(v, axis=1, keepdims=True)
    d = v - mu
    var = jnp.mean(d * d, axis=1, keepdims=True)
    return d * lax.rsqrt(var + EPS) * g + b


def _outproj_ln_kernel(m_ref, w_ref, x_ref, g_ref, b_ref, o_ref, ob_ref, *, alpha):
    mix = _dot(m_ref[...], w_ref[...])
    y = _layernorm(alpha * x_ref[...] + mix, g_ref[...], b_ref[...])
    o_ref[...] = y
    ob_ref[...] = y.astype(BF16)


def _outproj_ln(merged, w_out, x, g, b, alpha, tm):
    t, d = x.shape
    const = lambda i: (0, 0)
    return pl.pallas_call(
        functools.partial(_outproj_ln_kernel, alpha=alpha),
        out_shape=(jax.ShapeDtypeStruct((t, d), F32), jax.ShapeDtypeStruct((t, d), BF16)),
        grid=(t // tm,),
        in_specs=[
            pl.BlockSpec((tm, d), lambda i: (i, 0)),
            pl.BlockSpec((d, d), const),
            pl.BlockSpec((tm, d), lambda i: (i, 0)),
            pl.BlockSpec((1, d), const),
            pl.BlockSpec((1, d), const),
        ],
        out_specs=(pl.BlockSpec((tm, d), lambda i: (i, 0)), pl.BlockSpec((tm, d), lambda i: (i, 0))),
        compiler_params=_params(("parallel",)),
        name="outproj_layernorm",
    )(merged, w_out, x, g.reshape(1, d), b.reshape(1, d))


def _router_kernel(x_ref, w_ref, bias_ref, eidx_ref, wt_ref, pos_ref, cnt_ref, carry_ref, *, tm):
    i = pl.program_id(0)
    ne = N_EXPERTS
    ng = N_EXPERT_GROUPS
    per = ne // ng

    @pl.when(i == 0)
    def _init():
        carry_ref[...] = jnp.zeros(carry_ref.shape, F32)

    x = x_ref[...]
    xh = x.astype(BF16)
    xl = (x - xh.astype(F32)).astype(BF16)
    w = w_ref[...]
    wh = w.astype(BF16)
    wl = (w - wh.astype(F32)).astype(BF16)
    logits = _dot_nt(wh, xh) + _dot_nt(wh, xl) + _dot_nt(wl, xh)
    scores = _sigmoid(logits)
    sel = scores + bias_ref[...]

    grp = sel.reshape(ng, per, tm)
    io_p = lax.broadcasted_iota(jnp.int32, (ng, per, tm), 1)
    m1 = jnp.max(grp, axis=1, keepdims=True)
    i1 = jnp.min(jnp.where(grp == m1, io_p, per), axis=1, keepdims=True)
    m2 = jnp.max(jnp.where(io_p == i1, -jnp.inf, grp), axis=1, keepdims=True)
    gscore = (m1 + m2).reshape(ng, tm)

    io_g = lax.broadcasted_iota(jnp.int32, (ng, tm), 0)
    gsel = jnp.zeros((ng, tm), F32)
    gwork = gscore
    for _ in range(TOPK_GROUPS):
        gm = jnp.max(gwork, axis=0, keepdims=True)
        gi = jnp.min(jnp.where(gwork == gm, io_g, ng), axis=0, keepdims=True)
        hit = io_g == gi
        gsel = jnp.where(hit, 1.0, gsel)
        gwork = jnp.where(hit, -jnp.inf, gwork)
    emask = jnp.broadcast_to(gsel.reshape(ng, 1, tm), (ng, per, tm)).reshape(ne, tm) > 0.5

    io_e = lax.broadcasted_iota(jnp.int32, (ne, tm), 0)
    work = jnp.where(emask, sel, -jnp.inf)
    member = jnp.zeros((ne, tm), F32)
    idx_rows = []
    w_rows = []
    for _ in range(TOP_K):
        mx = jnp.max(work, axis=0, keepdims=True)
        ei = jnp.min(jnp.where(work == mx, io_e, ne), axis=0, keepdims=True)
        hit = io_e == ei
        idx_rows.append(ei)
        w_rows.append(jnp.sum(jnp.where(hit, scores, 0.0), axis=0, keepdims=True))
        member = jnp.where(hit, 1.0, member)
        work = jnp.where(hit, -jnp.inf, work)
    wsum = w_rows[0]
    for wr in w_rows[1:]:
        wsum = wsum + wr
    inv = ROUTED_SCALE / wsum

    r = lax.broadcasted_iota(jnp.int32, (tm, tm), 0)
    c = lax.broadcasted_iota(jnp.int32, (tm, tm), 1)
    upper = jnp.where(r < c, 1.0, 0.0).astype(BF16)
    prefix = _dot(member.astype(BF16), upper) + carry_ref[...]
    for j in range(TOP_K):
        hit = io_e == idx_rows[j]
        eidx_ref[j:j + 1, :] = idx_rows[j]
        wt_ref[j:j + 1, :] = w_rows[j] * inv
        pos_ref[j:j + 1, :] = jnp.sum(jnp.where(hit, prefix, 0.0), axis=0, keepdims=True).astype(jnp.int32)
    carry = carry_ref[...] + jnp.sum(member, axis=1, keepdims=True)
    carry_ref[...] = carry
    cnt_ref[...] = jnp.broadcast_to(carry, cnt_ref.shape)


def _router(x, router_w, router_bias, tm):
    t, d = x.shape
    ne = N_EXPERTS
    return pl.pallas_call(
        functools.partial(_router_kernel, tm=tm),
        out_shape=(
            jax.ShapeDtypeStruct((TOP_K, t), jnp.int32),
            jax.ShapeDtypeStruct((TOP_K, t), F32),
            jax.ShapeDtypeStruct((TOP_K, t), jnp.int32),
            jax.ShapeDtypeStruct((ne, LANES), F32),
        ),
        grid=(t // tm,),
        in_specs=[
            pl.BlockSpec((tm, d), lambda i: (i, 0)),
            pl.BlockSpec((ne, d), lambda i: (0, 0)),
            pl.BlockSpec((ne, 1), lambda i: (0, 0)),
        ],
        out_specs=(
            pl.BlockSpec((TOP_K, tm), lambda i: (0, i)),
            pl.BlockSpec((TOP_K, tm), lambda i: (0, i)),
            pl.BlockSpec((TOP_K, tm), lambda i: (0, i)),
            pl.BlockSpec((ne, LANES), lambda i: (0, 0)),
        ),
        scratch_shapes=[pltpu.VMEM((ne, 1), F32)],
        compiler_params=_params(("arbitrary",)),
        name="router",
    )(x, router_w.T, router_bias.reshape(ne, 1))


def _slots_kernel(pstart, eidx_ref, pos_ref, dest_ref):
    eidx = eidx_ref[...]
    base = jnp.zeros(eidx.shape, jnp.int32)
    for e in range(N_EXPERTS):
        base = jnp.where(eidx == e, pstart[e], base)
    dest_ref[...] = base + pos_ref[...]


def _slots(pstart, eidx, pos, tn):
    k, t = eidx.shape
    blk = lambda i, ps: (0, i)
    grid_spec = pltpu.PrefetchScalarGridSpec(
        num_scalar_prefetch=1,
        grid=(t // tn,),
        in_specs=[pl.BlockSpec((k, tn), blk), pl.BlockSpec((k, tn), blk)],
        out_specs=pl.BlockSpec((k, tn), blk),
    )
    return pl.pallas_call(
        _slots_kernel,
        out_shape=jax.ShapeDtypeStruct((k, t), jnp.int32),
        grid_spec=grid_spec,
        compiler_params=_params(("parallel",)),
        name="moe_slots",
    )(pstart, eidx, pos)


def _dispatch_kernel(dest_ref, x_ref, xs_hbm, sem):
    tm = x_ref.shape[0]

    def issue(r, carry):
        for j in range(TOP_K):
            pltpu.make_async_copy(x_ref.at[pl.ds(r, 1), :], xs_hbm.at[pl.ds(dest_ref[j, r], 1), :], sem).start()
        return carry

    lax.fori_loop(0, tm, issue, 0)
    for j in range(TOP_K):
        pltpu.make_async_copy(x_ref, xs_hbm.at[pl.ds(0, tm), :], sem).wait()


def _dispatch(dest, x, rows, tm):
    t, d = x.shape
    return pl.pallas_call(
        _dispatch_kernel,
        out_shape=jax.ShapeDtypeStruct((rows, d), x.dtype),
        grid=(t // tm,),
        in_specs=[
            pl.BlockSpec((TOP_K, tm), lambda i: (0, i), memory_space=pltpu.SMEM),
            pl.BlockSpec((tm, d), lambda i: (i, 0)),
        ],
        out_specs=pl.BlockSpec(memory_space=pl.ANY),
        scratch_shapes=[pltpu.SemaphoreType.DMA(())],
        compiler_params=_params(("arbitrary",)),
        name="moe_dispatch",
    )(dest, x)


def _experts_kernel(blk_e, blk_rows, nvalid, xs_ref, wg_ref, wu_ref, wd_ref, o_ref, wgu_s, wd_s):
    i = pl.program_id(0)
    f = wg_ref.shape[-1]

    @pl.when(i < nvalid[0])
    def _():
        e = blk_e[i]
        prev = blk_e[jnp.maximum(i - 1, 0)]

        @pl.when((i == 0) | (e != prev))
        def _cast():
            wgu_s[:, 0:f] = wg_ref[...].astype(BF16)
            wgu_s[:, f:2 * f] = wu_ref[...].astype(BF16)
            wd_s[...] = wd_ref[...].astype(BF16)

        row = lax.broadcasted_iota(jnp.int32, xs_ref.shape, 0)
        x = jnp.where(row < blk_rows[i], xs_ref[...], 0.0).astype(BF16)
        gu = _dot(x, wgu_s[...])
        hid = _silu(gu[:, 0:f]) * gu[:, f:2 * f]
        o_ref[...] = _dot(hid.astype(BF16), wd_s[...])


def _experts(xs, blk_e, blk_rows, nvalid, w_gate, w_up, w_down, layer):
    p, d = xs.shape
    f = w_gate.shape[-1]
    bm = MOE_BLOCK
    row = lambda i, be, br, nv: (jnp.minimum(i, nv[0] - 1), 0)
    wmap = lambda i, be, br, nv: (layer, be[i], 0, 0)
    grid_spec = pltpu.PrefetchScalarGridSpec(
        num_scalar_prefetch=3,
        grid=(p // bm,),
        in_specs=[
            pl.BlockSpec((bm, d), row),
            pl.BlockSpec((None, None, d, f), wmap),
            pl.BlockSpec((None, None, d, f), wmap),
            pl.BlockSpec((None, None, f, d), wmap),
        ],
        out_specs=pl.BlockSpec((bm, d), row),
        scratch_shapes=[pltpu.VMEM((d, 2 * f), BF16), pltpu.VMEM((f, d), BF16)],
    )
    return pl.pallas_call(
        _experts_kernel,
        out_shape=jax.ShapeDtypeStruct((p, d), F32),
        grid_spec=grid_spec,
        compiler_params=_params(("arbitrary",)),
        name="routed_experts",
    )(blk_e, blk_rows, nvalid, xs, w_gate, w_up, w_down)


def _shared_ln_kernel(dest_ref, x_ref, xb_ref, wt_ref, ys_hbm, wgu_ref, wd_ref, g_ref, b_ref, o_ref, ob_ref,
                      buf, sem, *, alpha):
    f = wd_ref.shape[0]
    tm = x_ref.shape[0]

    def issue(r, carry):
        for j in range(TOP_K):
            pltpu.make_async_copy(ys_hbm.at[pl.ds(dest_ref[j, r], 1), :], buf.at[j, pl.ds(r, 1), :], sem).start()
        return carry

    lax.fori_loop(0, tm, issue, 0)
    gu = _dot(xb_ref[...], wgu_ref[...])
    hid = _silu(gu[:, 0:f]) * gu[:, f:2 * f]
    ffn = _dot(hid.astype(BF16), wd_ref[...])
    for j in range(TOP_K):
        pltpu.make_async_copy(ys_hbm.at[pl.ds(0, tm), :], buf.at[j], sem).wait()
    for j in range(TOP_K):
        ffn = ffn + wt_ref[:, j:j + 1] * buf[j]
    y = _layernorm(alpha * x_ref[...] + ffn, g_ref[...], b_ref[...])
    o_ref[...] = y
    ob_ref[...] = y.astype(BF16)


def _shared_ln(dest, x, xb, wts_t, ys, wgu, wd, g, b, alpha, tm):
    t, d = x.shape
    f = wd.shape[0]
    const = lambda i: (0, 0)
    tok = lambda i: (i, 0)
    return pl.pallas_call(
        functools.partial(_shared_ln_kernel, alpha=alpha),
        out_shape=(jax.ShapeDtypeStruct((t, d), F32), jax.ShapeDtypeStruct((t, d), BF16)),
        grid=(t // tm,),
        in_specs=[
            pl.BlockSpec((TOP_K, tm), lambda i: (0, i), memory_space=pltpu.SMEM),
            pl.BlockSpec((tm, d), tok),
            pl.BlockSpec((tm, d), tok),
            pl.BlockSpec((tm, TOP_K), tok),
            pl.BlockSpec(memory_space=pl.ANY),
            pl.BlockSpec((d, 2 * f), const),
            pl.BlockSpec((f, d), const),
            pl.BlockSpec((1, d), const),
            pl.BlockSpec((1, d), const),
        ],
        out_specs=(pl.BlockSpec((tm, d), tok), pl.BlockSpec((tm, d), tok)),
        scratch_shapes=[pltpu.VMEM((TOP_K, tm, d), F32), pltpu.SemaphoreType.DMA(())],
        compiler_params=_params(("arbitrary",)),
        name="shared_expert_combine_layernorm",
    )(dest, x, xb, wts_t, ys, wgu, wd, g.reshape(1, d), b.reshape(1, d))


def _moe_layout(cnt, t):
    counts = cnt[:, 0].astype(jnp.int32)
    padded = (counts + MOE_BLOCK - 1) // MOE_BLOCK * MOE_BLOCK
    pend = jnp.cumsum(padded)
    pstart = pend - padded
    nblk = (t * TOP_K + N_EXPERTS * MOE_BLOCK) // MOE_BLOCK
    nvalid = pend[-1] // MOE_BLOCK
    starts = jnp.minimum(jnp.arange(nblk, dtype=jnp.int32), nvalid - 1) * MOE_BLOCK
    blk_e = jnp.sum((pend[None, :] <= starts[:, None]).astype(jnp.int32), axis=1)
    used_end = jnp.sum(jnp.where(blk_e[:, None] == jnp.arange(N_EXPERTS)[None, :], (pstart + counts)[None, :], 0),
                       axis=1)
    blk_rows = jnp.clip(used_end - starts, 0, MOE_BLOCK).astype(jnp.int32)
    return pstart.astype(jnp.int32), blk_e.astype(jnp.int32), blk_rows, nvalid.astype(jnp.int32).reshape(1), nblk


def kernel(x, w_in, da_lambda, da_norm_g, gla_gate_w2, gla_gate_b, gla_norm_g, ssd_conv_w, ssd_conv_b,
           ssd_dt_bias, ssd_a_log, ssd_d, ssd_norm_g, w_branch, w_out, ln1_g, ln1_b, router_w, router_bias,
           exp_w_gate, exp_w_up, exp_w_down, sh_w_gate, sh_w_up, sh_w_down, ln2_g, ln2_b):
    batch, seq, d = x.shape
    depth = w_in.shape[0]
    t = batch * seq
    alpha = (2 * depth) ** 0.25

    da_w = DA_HEADS * DA_V_DIM
    gk_w = GLA_HEADS * GLA_DK
    gv_w = GLA_HEADS * GLA_DV
    conv_dim = SSD_WIDTH + 2 * SSD_GROUPS * SSD_STATE
    sizes = (da_w, da_w, da_w, gk_w, gk_w, gv_w, GLA_GATE_RANK, gv_w, SSD_WIDTH, conv_dim, SSD_HEADS,
             N_BRANCH * d)
    offs = np.concatenate([[0], np.cumsum(sizes)]).tolist()
    (o_daq, o_dak, o_dav, o_gq, o_gk, o_gv, o_glr, o_gr, o_sz, o_sx, o_sdt, o_mg, o_end) = offs

    ones = lambda n: jnp.ones((1, n), F32)
    qkv_scale = jnp.concatenate([jnp.full((1, da_w), DA_HEAD_DIM ** -0.5, F32), ones(2 * da_w)], axis=1)

    xf = x.reshape(t, d)
    xb = xf.astype(BF16)
    for l in range(depth):
        wl = w_in[l]
        w_qkv = wl[:, o_daq:o_gq].astype(BF16)
        w_gla = jnp.concatenate([wl[:, o_gq:o_glr], wl[:, o_gr:o_sz]], axis=1).astype(BF16)
        w_ssd = wl[:, o_sz:o_sdt].astype(BF16)
        w_small = jnp.concatenate([
            jnp.pad(wl[:, o_glr:o_gr], ((0, 0), (0, LANES - GLA_GATE_RANK))),
            jnp.pad(wl[:, o_sdt:o_mg], ((0, 0), (0, LANES - SSD_HEADS)))], axis=1).astype(BF16)
        w_mg = wl[:, o_mg:o_end].reshape(d, N_BRANCH, d).transpose(1, 0, 2).astype(BF16)

        qkv = _matmul(xb, w_qkv, qkv_scale, BF16, 512, 1024, "proj_attn")
        pg = _matmul(xb, w_gla, ones(w_gla.shape[1]), F32, 512, 1024, "proj_gla")
        ps = _matmul(xb, w_ssd, ones(w_ssd.shape[1]), F32, 512, 1024, "proj_ssd")
        small = _matmul(xb, w_small, ones(2 * LANES), F32, 512, 2 * LANES, "proj_small")

        br_a = _diff_attention(qkv, da_lambda[l], da_norm_g[l], l, batch, seq)
        w2p = jnp.pad(gla_gate_w2[l], ((0, LANES - GLA_GATE_RANK), (0, 0))).astype(BF16)
        br_b = _gla(pg, small, w2p, gla_gate_b[l].reshape(1, gk_w), gla_norm_g[l], batch, seq)
        br_c = _ssd(ps, small, ssd_conv_w[l], ssd_conv_b[l], ssd_dt_bias[l], ssd_a_log[l], ssd_d[l],
                    ssd_norm_g[l], batch, seq)

        merged = _merge(xb, w_mg, br_a, br_b, br_c, w_branch[l].astype(BF16), 512, 512)
        xf, xb = _outproj_ln(merged, w_out[l].astype(BF16), xf, ln1_g[l], ln1_b[l], alpha, TOKEN_TILE)

        eidx, wts, pos, cnt = _router(xf, router_w[l], router_bias[l], TOKEN_TILE)
        pstart, blk_e, blk_rows, nvalid, nblk = _moe_layout(cnt, t)
        dest = _slots(pstart, eidx, pos, min(t, 2048))
        xs = _dispatch(dest, xf, nblk * MOE_BLOCK, TOKEN_TILE)
        ys = _experts(xs, blk_e, blk_rows, nvalid, exp_w_gate, exp_w_up, exp_w_down, l)
        wgu = jnp.concatenate([sh_w_gate[l], sh_w_up[l]], axis=1).astype(BF16)
        xf, xb = _shared_ln(dest, xf, xb, wts.T, ys, wgu, sh_w_down[l].astype(BF16), ln2_g[l], ln2_b[l], alpha,
                            TOKEN_TILE)
    return xf.reshape(batch, seq, d)
```

```python
import functools
import math

import jax
import jax.numpy as jnp
import numpy as np
from jax import lax
from jax.experimental import pallas as pl
from jax.experimental.pallas import tpu as pltpu

F32 = jnp.float32
BF16 = jnp.bfloat16

DA_HEADS = 4
DA_HEAD_DIM = 128
DA_V_DIM = 2 * DA_HEAD_DIM
GLA_HEADS = 4
GLA_DK = 128
GLA_DV = 256
GLA_GATE_RANK = 16
GLA_TAU = 16.0
GLA_CHUNK = 64
SSD_HEADS = 16
SSD_HEAD_DIM = 64
SSD_STATE = 128
SSD_GROUPS = 4
SSD_CONV = 4
SSD_CHUNK = 128
SSD_WIDTH = SSD_HEADS * SSD_HEAD_DIM
N_BRANCH = 3
BRANCH_WIDTH = 1024
N_EXPERTS = 64
TOP_K = 8
N_EXPERT_GROUPS = 8
TOPK_GROUPS = 4
ROUTED_SCALE = 2.5
EPS = 1e-5

LANES = 128
SUBLANES = 8
VMEM_LIMIT = 52 * 1024 * 1024

ATTN_BLOCK = 512
GLA_BLOCK = 512
MOE_BLOCK = 512
TOKEN_TILE = 256


def _params(sem):
    return pltpu.CompilerParams(dimension_semantics=sem, vmem_limit_bytes=VMEM_LIMIT)


def _silu(x):
    return x * (1.0 / (1.0 + jnp.exp(-x)))


def _sigmoid(x):
    return 1.0 / (1.0 + jnp.exp(-x))


def _split3(x):
    h1 = x.astype(BF16)
    r1 = x - h1.astype(F32)
    h2 = r1.astype(BF16)
    r2 = r1 - h2.astype(F32)
    return h1, h2, r2.astype(BF16)


HI_HALF = -65536


def _pack_bf16_pair(lo, hi):
    lo_bits = lax.bitcast_convert_type(lo.astype(BF16).astype(F32), jnp.int32)
    hi_bits = lax.bitcast_convert_type(hi.astype(BF16).astype(F32), jnp.int32)
    return lax.shift_right_logical(lo_bits, 16) | (hi_bits & HI_HALF)


def _unpack_bf16_pair(w):
    lo = lax.bitcast_convert_type(lax.shift_left(w, 16), F32)
    hi = lax.bitcast_convert_type(w & HI_HALF, F32)
    return lo, hi


def _dot(a, b):
    return jnp.dot(a, b, preferred_element_type=F32)


def _dot_nt(a, b):
    return lax.dot_general(a, b, (((1,), (1,)), ((), ())), preferred_element_type=F32)


def _dot_tn(a, b):
    return lax.dot_general(a, b, (((0,), (0,)), ((), ())), preferred_element_type=F32)


def _mm_kernel(a_ref, w_ref, s_ref, o_ref):
    acc = _dot(a_ref[...], w_ref[...])
    o_ref[...] = (acc * s_ref[...]).astype(o_ref.dtype)


def _matmul(a, w, col_scale, out_dtype, tm, tn, name):
    m, k = a.shape
    n = w.shape[1]
    return pl.pallas_call(
        _mm_kernel,
        out_shape=jax.ShapeDtypeStruct((m, n), out_dtype),
        grid=(m // tm, n // tn),
        in_specs=[
            pl.BlockSpec((tm, k), lambda i, j: (i, 0)),
            pl.BlockSpec((k, tn), lambda i, j: (0, j)),
            pl.BlockSpec((1, tn), lambda i, j: (0, j)),
        ],
        out_specs=pl.BlockSpec((tm, tn), lambda i, j: (i, j)),
        compiler_params=_params(("parallel", "arbitrary")),
        name=name,
    )(a, w, col_scale)


def _mm_nt_kernel(wt_ref, a_ref, o_ref):
    o_ref[...] = _dot_nt(wt_ref[...], a_ref[...]).astype(o_ref.dtype)


def _matmul_nt(wt, a, out_dtype, tm, name):
    n, k = wt.shape
    m = a.shape[0]
    return pl.pallas_call(
        _mm_nt_kernel,
        out_shape=jax.ShapeDtypeStruct((n, m), out_dtype),
        grid=(m // tm,),
        in_specs=[
            pl.BlockSpec((n, k), lambda i: (0, 0)),
            pl.BlockSpec((tm, k), lambda i: (i, 0)),
        ],
        out_specs=pl.BlockSpec((n, tm), lambda i: (0, i)),
        compiler_params=_params(("parallel",)),
        name=name,
    )(wt, a)


def _attn_kernel(qi_tab, ki_tab, q_ref, k_ref, vt_ref, bt_ref, slope_ref, lam_ref, g_ref, o_ref,
                 m_ref, l_ref, acc_ref, *, blk, lam_init):
    p = pl.program_id(2)
    qi = qi_tab[p]
    ki = ki_tab[p]

    @pl.when(ki == 0)
    def _init():
        m_ref[...] = jnp.full(m_ref.shape, -jnp.inf, F32)
        l_ref[...] = jnp.zeros(l_ref.shape, F32)
        acc_ref[...] = jnp.zeros(acc_ref.shape, F32)

    c0 = slope_ref[0] * jnp.full((1, blk), (ki - qi) * blk, jnp.int32).astype(F32)

    def step(masked):
        q = q_ref[...]
        k = k_ref[...]
        vt = vt_ref[...]
        bt = bt_ref[0]
        if masked:
            kc = lax.broadcasted_iota(jnp.int32, (blk, blk), 0)
            qr = lax.broadcasted_iota(jnp.int32, (blk, blk), 1)
            keep = kc <= qr
        for mp in range(2):
            sl = slice(mp * DA_HEAD_DIM, (mp + 1) * DA_HEAD_DIM)
            s = _dot_nt(k[:, sl], q[:, sl]) + bt
            if masked:
                s = jnp.where(keep, s, -jnp.inf)
            m_old = m_ref[mp]
            m_new = jnp.maximum(m_old, jnp.max(s, axis=0, keepdims=True) + c0)
            alpha = jnp.exp(m_old - m_new)
            pr = jnp.exp(s - (m_new - c0))
            l_ref[mp] = alpha * l_ref[mp] + jnp.sum(pr, axis=0, keepdims=True)
            acc_ref[mp] = alpha * acc_ref[mp] + _dot(vt, pr.astype(BF16))
            m_ref[mp] = m_new

    @pl.when(ki < qi)
    def _off_diag():
        step(False)

    @pl.when(ki == qi)
    def _diag():
        step(True)
        lp = lam_ref[...]
        lam = (jnp.exp(jnp.sum(lp[0:1] * lp[1:2], axis=1, keepdims=True))
               - jnp.exp(jnp.sum(lp[2:3] * lp[3:4], axis=1, keepdims=True)) + lam_init)
        ot = acc_ref[0] / l_ref[0] - lam * (acc_ref[1] / l_ref[1])
        o = ot.T
        o = o * lax.rsqrt(jnp.mean(o * o, axis=1, keepdims=True) + EPS)
        o_ref[...] = (o * g_ref[...] * (1.0 - lam_init)).astype(o_ref.dtype)


def _diff_attention(qk, vt, lam_params, norm_g, layer_idx, batch, seq):
    blk = min(ATTN_BLOCK, seq)
    nq = seq // blk
    pairs = [(i, j) for i in range(nq) for j in range(i + 1)]
    qi_tab = jnp.asarray([p[0] for p in pairs], jnp.int32)
    ki_tab = jnp.asarray([p[1] for p in pairs], jnp.int32)
    lam_init = 0.8 - 0.6 * math.exp(-0.3 * layer_idx)
    h = DA_HEADS
    slopes = np.asarray([2.0 ** (-8.0 * (i + 1) / h) for i in range(h)], np.float32)
    key_off = np.arange(blk, dtype=np.float32)
    bias_t = jnp.asarray(np.broadcast_to((slopes[:, None] * key_off[None, :])[:, :, None], (h, blk, blk)))
    slopes = jnp.asarray(np.broadcast_to(slopes[:, None, None], (h, 1, blk)))
    t = batch * seq
    kern = functools.partial(_attn_kernel, blk=blk, lam_init=lam_init)
    grid_spec = pltpu.PrefetchScalarGridSpec(
        num_scalar_prefetch=2,
        grid=(batch, h, len(pairs)),
        in_specs=[
            pl.BlockSpec((blk, DA_V_DIM), lambda b, hh, p, qt, kt: (b * nq + qt[p], hh)),
            pl.BlockSpec((blk, DA_V_DIM), lambda b, hh, p, qt, kt: (b * nq + kt[p], h + hh)),
            pl.BlockSpec((DA_V_DIM, blk), lambda b, hh, p, qt, kt: (hh, b * nq + kt[p])),
            pl.BlockSpec((1, blk, blk), lambda b, hh, p, qt, kt: (hh, 0, 0)),
            pl.BlockSpec((1, 1, blk), lambda b, hh, p, qt, kt: (hh, 0, 0)),
            pl.BlockSpec((4, DA_HEAD_DIM), lambda b, hh, p, qt, kt: (0, 0)),
            pl.BlockSpec((1, DA_V_DIM), lambda b, hh, p, qt, kt: (0, 0)),
        ],
        out_specs=pl.BlockSpec((blk, DA_V_DIM), lambda b, hh, p, qt, kt: (b * nq + qt[p], hh)),
        scratch_shapes=[
            pltpu.VMEM((2, 1, blk), F32),
            pltpu.VMEM((2, 1, blk), F32),
            pltpu.VMEM((2, DA_V_DIM, blk), F32),
        ],
    )
    return pl.pallas_call(
        kern,
        out_shape=jax.ShapeDtypeStruct((t, h * DA_V_DIM), BF16),
        grid_spec=grid_spec,
        compiler_params=_params(("parallel", "parallel", "arbitrary")),
        name="diff_attention",
    )(qi_tab, ki_tab, qk, qk, vt, bias_t, slopes, lam_params, norm_g.reshape(1, DA_V_DIM))


def _gla_kernel(pg_ref, sm_ref, w2_ref, b2_ref, g_ref, o_ref, st_ref, *, blk):
    n = pl.program_id(1)
    hk = GLA_HEADS * GLA_DK
    hv = GLA_HEADS * GLA_DV
    c_len = GLA_CHUNK

    @pl.when(n == 0)
    def _init():
        st_ref[...] = jnp.zeros(st_ref.shape, F32)

    lr = sm_ref[:, 0:LANES].astype(BF16)
    gl = _dot(lr, w2_ref[...]) + b2_ref[...]
    gk = (jnp.minimum(gl, 0.0) - jnp.log(1.0 + jnp.exp(-jnp.abs(gl)))) * (1.0 / GLA_TAU)
    r = lax.broadcasted_iota(jnp.int32, (blk, blk), 0)
    c = lax.broadcasted_iota(jnp.int32, (blk, blk), 1)
    tri = jnp.where((c <= r) & ((r // c_len) == (c // c_len)), 1.0, 0.0).astype(BF16)
    g1, g2, g3 = _split3(gk)
    bcum = _dot(tri, g1) + _dot(tri, g2) + _dot(tri, g3)

    rr = lax.broadcasted_iota(jnp.int32, (c_len, c_len), 0)
    cc = lax.broadcasted_iota(jnp.int32, (c_len, c_len), 1)
    causal = cc <= rr
    scale = GLA_DK ** -0.5
    for ci in range(blk // c_len):
        rows = slice(ci * c_len, (ci + 1) * c_len)
        for h in range(GLA_HEADS):
            kc = slice(h * GLA_DK, (h + 1) * GLA_DK)
            vc = slice(h * GLA_DV, (h + 1) * GLA_DV)
            b = bcum[rows, kc]
            b_last = b[c_len - 1:c_len, :]
            q = pg_ref[rows, kc]
            k = pg_ref[rows, hk + h * GLA_DK: hk + (h + 1) * GLA_DK]
            v = pg_ref[rows, 2 * hk + h * GLA_DV: 2 * hk + (h + 1) * GLA_DV].astype(BF16)
            rg = pg_ref[rows, 2 * hk + hv + h * GLA_DV: 2 * hk + hv + (h + 1) * GLA_DV]
            q_e = (q * scale * jnp.exp(b)).astype(BF16)
            k_e = (k * jnp.exp(-b)).astype(BF16)
            k_d = (k * jnp.exp(b_last - b)).astype(BF16)
            att = jnp.where(causal, _dot_nt(q_e, k_e), 0.0)
            st = st_ref[h]
            o = _dot(att.astype(BF16), v) + _dot_nt(q_e, st.astype(BF16))
            st_ref[h] = st * jnp.exp(b_last) + _dot_tn(v, k_d)
            o = o * lax.rsqrt(jnp.mean(o * o, axis=1, keepdims=True) + EPS) * g_ref[...]
            o_ref[rows, vc] = (o * _silu(rg)).astype(o_ref.dtype)


def _gla(pg, small, w2p, b2, norm_g, batch, seq):
    blk = min(GLA_BLOCK, seq)
    nb = seq // blk
    t = batch * seq
    hk = GLA_HEADS * GLA_DK
    hv = GLA_HEADS * GLA_DV
    width = pg.shape[1]
    return pl.pallas_call(
        functools.partial(_gla_kernel, blk=blk),
        out_shape=jax.ShapeDtypeStruct((t, hv), BF16),
        grid=(batch, nb),
        in_specs=[
            pl.BlockSpec((blk, width), lambda b, n: (b * nb + n, 0)),
            pl.BlockSpec((blk, small.shape[1]), lambda b, n: (b * nb + n, 0)),
            pl.BlockSpec((LANES, hk), lambda b, n: (0, 0)),
            pl.BlockSpec((1, hk), lambda b, n: (0, 0)),
            pl.BlockSpec((1, GLA_DV), lambda b, n: (0, 0)),
        ],
        out_specs=pl.BlockSpec((blk, hv), lambda b, n: (b * nb + n, 0)),
        scratch_shapes=[pltpu.VMEM((GLA_HEADS, GLA_DV, GLA_DK), F32)],
        compiler_params=_params(("parallel", "arbitrary")),
        name="gla",
    )(pg, small, w2p, b2, norm_g.reshape(1, GLA_DV))


def _ssd_kernel(ps_ref, sm_ref, cw_ref, cb_ref, dtb_ref, alog_ref, dsk_ref, g_ref, o_ref,
                tail_ref, st_ref):
    n = pl.program_id(1)
    q_len = SSD_CHUNK
    w = SSD_WIDTH
    gs = SSD_GROUPS * SSD_STATE
    heads_per_group = SSD_HEADS // SSD_GROUPS
    gw = heads_per_group * SSD_HEAD_DIM

    @pl.when(n == 0)
    def _init():
        tail_ref[...] = jnp.zeros(tail_ref.shape, F32)
        st_ref[...] = jnp.zeros(st_ref.shape, F32)

    cur = ps_ref[:, w:w + w + 2 * gs]
    ext = jnp.concatenate([tail_ref[...], cur], axis=0)
    acc = cb_ref[...] + cw_ref[0:1, :] * ext[SUBLANES - 3:SUBLANES - 3 + q_len]
    for i in range(1, SSD_CONV):
        off = SUBLANES - (SSD_CONV - 1) + i
        acc = acc + cw_ref[i:i + 1, :] * ext[off:off + q_len]
    tail_ref[...] = cur[q_len - SUBLANES:q_len]
    xbc = _silu(acc)
    xs = xbc[:, 0:w]
    bm = xbc[:, w:w + gs]
    cm = xbc[:, w + gs:w + 2 * gs]

    dtr = sm_ref[:, LANES:2 * LANES] + dtb_ref[...]
    dtv = jnp.maximum(dtr, 0.0) + jnp.log(1.0 + jnp.exp(-jnp.abs(dtr)))
    da = dtv * (-jnp.exp(alog_ref[...]))
    r = lax.broadcasted_iota(jnp.int32, (q_len, q_len), 0)
    c = lax.broadcasted_iota(jnp.int32, (q_len, q_len), 1)
    causal = c <= r
    tri = jnp.where(causal, 1.0, 0.0).astype(BF16)
    d1, d2, d3 = _split3(da)
    a_cs = _dot(tri, d1) + _dot(tri, d2) + _dot(tri, d3)
    a_cs_t = a_cs.T
    a_last = a_cs[q_len - 1:q_len, :]
    e_last = jnp.exp(a_last)

    for g in range(SSD_GROUPS):
        bm_g = bm[:, g * SSD_STATE:(g + 1) * SSD_STATE]
        cm_g = cm[:, g * SSD_STATE:(g + 1) * SSD_STATE]
        bm_b = bm_g.astype(BF16)
        cm_b = cm_g.astype(BF16)
        cb = _dot_nt(cm_b, bm_b)
        y_diag = []
        xdd = []
        e_col = []
        e_row = []
        for rh in range(heads_per_group):
            h = g * heads_per_group + rh
            col = a_cs[:, h:h + 1]
            row = a_cs_t[h:h + 1, :]
            lm = jnp.exp(jnp.where(causal, col - row, -jnp.inf))
            xdt = xs[:, h * SSD_HEAD_DIM:(h + 1) * SSD_HEAD_DIM] * dtv[:, h:h + 1]
            y_diag.append(_dot((cb * lm).astype(BF16), xdt.astype(BF16)))
            xdd.append(xdt * jnp.exp(a_last[:, h:h + 1] - col))
            e_col.append(jnp.broadcast_to(jnp.exp(col), (q_len, SSD_HEAD_DIM)))
            e_row.append(jnp.broadcast_to(e_last[:, h:h + 1], (1, SSD_HEAD_DIM)))
        y_diag = jnp.concatenate(y_diag, axis=1)
        xdd = jnp.concatenate(xdd, axis=1)
        e_col = jnp.concatenate(e_col, axis=1)
        e_row = jnp.concatenate(e_row, axis=1)
        s_prev = st_ref[g]
        y_off = _dot(cm_b, s_prev.astype(BF16)) * e_col
        st_ref[g] = s_prev * e_row + _dot_tn(bm_b, xdd.astype(BF16))
        lanes = slice(g * gw, (g + 1) * gw)
        y = y_diag + y_off + xs[:, lanes] * dsk_ref[:, lanes]
        y = y * _silu(ps_ref[:, lanes])
        y = y * lax.rsqrt(jnp.mean(y * y, axis=1, keepdims=True) + EPS) * g_ref[:, lanes]
        o_ref[:, lanes] = y.astype(o_ref.dtype)


def _ssd(ps, small, conv_w, conv_b, dt_bias, a_log, d_skip, norm_g, batch, seq):
    q_len = SSD_CHUNK
    nb = seq // q_len
    t = batch * seq
    conv_dim = conv_w.shape[1]
    pad = LANES - SSD_HEADS
    dtb = jnp.pad(dt_bias, (0, pad)).reshape(1, LANES)
    alog = jnp.pad(a_log, (0, pad)).reshape(1, LANES)
    dsk = jnp.repeat(d_skip, SSD_HEAD_DIM).reshape(1, SSD_WIDTH)
    const = lambda b, n: (0, 0)
    return pl.pallas_call(
        _ssd_kernel,
        out_shape=jax.ShapeDtypeStruct((t, SSD_WIDTH), BF16),
        grid=(batch, nb),
        in_specs=[
            pl.BlockSpec((q_len, ps.shape[1]), lambda b, n: (b * nb + n, 0)),
            pl.BlockSpec((q_len, small.shape[1]), lambda b, n: (b * nb + n, 0)),
            pl.BlockSpec((SSD_CONV, conv_dim), const),
            pl.BlockSpec((1, conv_dim), const),
            pl.BlockSpec((1, LANES), const),
            pl.BlockSpec((1, LANES), const),
            pl.BlockSpec((1, SSD_WIDTH), const),
            pl.BlockSpec((1, SSD_WIDTH), const),
        ],
        out_specs=pl.BlockSpec((q_len, SSD_WIDTH), lambda b, n: (b * nb + n, 0)),
        scratch_shapes=[
            pltpu.VMEM((SUBLANES, conv_dim), F32),
            pltpu.VMEM((SSD_GROUPS, SSD_STATE, SSD_WIDTH // SSD_GROUPS), F32),
        ],
        compiler_params=_params(("parallel", "arbitrary")),
        name="ssd",
    )(ps, small, conv_w, conv_b.reshape(1, conv_dim), dtb, alog, dsk, norm_g.reshape(1, SSD_WIDTH))


def _merge_kernel(x_ref, wg_ref, ba_ref, bb_ref, bc_ref, wb_ref, o_ref):
    x = x_ref[...]
    acc = None
    for i, br in enumerate((ba_ref, bb_ref, bc_ref)):
        gate = _sigmoid(_dot(x, wg_ref[i]))
        term = gate * _dot(br[...], wb_ref[i])
        acc = term if acc is None else acc + term
    o_ref[...] = acc.astype(o_ref.dtype)


def _merge(xb, wgate, br_a, br_b, br_c, wbr, tm, tn):
    t, d = xb.shape
    bw = br_a.shape[1]
    return pl.pallas_call(
        _merge_kernel,
        out_shape=jax.ShapeDtypeStruct((t, d), BF16),
        grid=(t // tm, d // tn),
        in_specs=[
            pl.BlockSpec((tm, d), lambda i, j: (i, 0)),
            pl.BlockSpec((N_BRANCH, d, tn), lambda i, j: (0, 0, j)),
            pl.BlockSpec((tm, bw), lambda i, j: (i, 0)),
            pl.BlockSpec((tm, bw), lambda i, j: (i, 0)),
            pl.BlockSpec((tm, bw), lambda i, j: (i, 0)),
            pl.BlockSpec((N_BRANCH, bw, tn), lambda i, j: (0, 0, j)),
        ],
        out_specs=pl.BlockSpec((tm, tn), lambda i, j: (i, j)),
        compiler_params=_params(("parallel", "arbitrary")),
        name="gated_merge",
    )(xb, wgate, br_a, br_b, br_c, wbr)


def _layernorm(v, g, b):
    mu = jnp.mean(v, axis=1, keepdims=True)
    d = v - mu
    var = jnp.mean(d * d, axis=1, keepdims=True)
    return d * lax.rsqrt(var + EPS) * g + b


def _outproj_ln_kernel(m_ref, w_ref, x_ref, g_ref, b_ref, o_ref, ob_ref, op_ref, *, alpha):
    mix = _dot(m_ref[...], w_ref[...])
    y = _layernorm(alpha * x_ref[...] + mix, g_ref[...], b_ref[...])
    half = y.shape[1] // 2
    o_ref[...] = y
    ob_ref[...] = y.astype(BF16)
    op_ref[...] = _pack_bf16_pair(y[:, 0:half], y[:, half:])


def _outproj_ln(merged, w_out, x, g, b, alpha, tm):
    t, d = x.shape
    const = lambda i: (0, 0)
    tok = lambda i: (i, 0)
    return pl.pallas_call(
        functools.partial(_outproj_ln_kernel, alpha=alpha),
        out_shape=(jax.ShapeDtypeStruct((t, d), F32), jax.ShapeDtypeStruct((t, d), BF16),
                   jax.ShapeDtypeStruct((t, d // 2), jnp.int32)),
        grid=(t // tm,),
        in_specs=[
            pl.BlockSpec((tm, d), tok),
            pl.BlockSpec((d, d), const),
            pl.BlockSpec((tm, d), tok),
            pl.BlockSpec((1, d), const),
            pl.BlockSpec((1, d), const),
        ],
        out_specs=(pl.BlockSpec((tm, d), tok), pl.BlockSpec((tm, d), tok), pl.BlockSpec((tm, d // 2), tok)),
        compiler_params=_params(("parallel",)),
        name="outproj_layernorm",
    )(merged, w_out, x, g.reshape(1, d), b.reshape(1, d))


def _router_kernel(x_ref, w_ref, bias_ref, eidx_ref, wt_ref, pos_ref, cnt_ref, carry_ref, *, tm):
    i = pl.program_id(0)
    ne = N_EXPERTS
    ng = N_EXPERT_GROUPS
    per = ne // ng

    @pl.when(i == 0)
    def _init():
        carry_ref[...] = jnp.zeros(carry_ref.shape, F32)

    x = x_ref[...]
    xh = x.astype(BF16)
    xl = (x - xh.astype(F32)).astype(BF16)
    w = w_ref[...]
    wh = w.astype(BF16)
    wl = (w - wh.astype(F32)).astype(BF16)
    logits = _dot_nt(wh, xh) + _dot_nt(wh, xl) + _dot_nt(wl, xh)
    scores = _sigmoid(logits)
    sel = scores + bias_ref[...]

    grp = sel.reshape(ng, per, tm)
    io_p = lax.broadcasted_iota(jnp.int32, (ng, per, tm), 1)
    m1 = jnp.max(grp, axis=1, keepdims=True)
    i1 = jnp.min(jnp.where(grp == m1, io_p, per), axis=1, keepdims=True)
    m2 = jnp.max(jnp.where(io_p == i1, -jnp.inf, grp), axis=1, keepdims=True)
    gscore = (m1 + m2).reshape(ng, tm)

    io_g = lax.broadcasted_iota(jnp.int32, (ng, tm), 0)
    gsel = jnp.zeros((ng, tm), F32)
    gwork = gscore
    for _ in range(TOPK_GROUPS):
        gm = jnp.max(gwork, axis=0, keepdims=True)
        gi = jnp.min(jnp.where(gwork == gm, io_g, ng), axis=0, keepdims=True)
        hit = io_g == gi
        gsel = jnp.where(hit, 1.0, gsel)
        gwork = jnp.where(hit, -jnp.inf, gwork)
    emask = jnp.broadcast_to(gsel.reshape(ng, 1, tm), (ng, per, tm)).reshape(ne, tm) > 0.5

    io_e = lax.broadcasted_iota(jnp.int32, (ne, tm), 0)
    work = jnp.where(emask, sel, -jnp.inf)
    member = jnp.zeros((ne, tm), F32)
    idx_rows = []
    w_rows = []
    for _ in range(TOP_K):
        mx = jnp.max(work, axis=0, keepdims=True)
        ei = jnp.min(jnp.where(work == mx, io_e, ne), axis=0, keepdims=True)
        hit = io_e == ei
        idx_rows.append(ei)
        w_rows.append(jnp.sum(jnp.where(hit, scores, 0.0), axis=0, keepdims=True))
        member = jnp.where(hit, 1.0, member)
        work = jnp.where(hit, -jnp.inf, work)
    wsum = w_rows[0]
    for wr in w_rows[1:]:
        wsum = wsum + wr
    inv = ROUTED_SCALE / wsum

    r = lax.broadcasted_iota(jnp.int32, (tm, tm), 0)
    c = lax.broadcasted_iota(jnp.int32, (tm, tm), 1)
    upper = jnp.where(r < c, 1.0, 0.0).astype(BF16)
    prefix = _dot(member.astype(BF16), upper) + carry_ref[...]
    for j in range(TOP_K):
        hit = io_e == idx_rows[j]
        eidx_ref[j:j + 1, :] = idx_rows[j]
        wt_ref[j:j + 1, :] = w_rows[j] * inv
        pos_ref[j:j + 1, :] = jnp.sum(jnp.where(hit, prefix, 0.0), axis=0, keepdims=True).astype(jnp.int32)
    carry = carry_ref[...] + jnp.sum(member, axis=1, keepdims=True)
    carry_ref[...] = carry
    cnt_ref[...] = jnp.broadcast_to(carry, cnt_ref.shape)


def _router(x, router_w, router_bias, tm):
    t, d = x.shape
    ne = N_EXPERTS
    return pl.pallas_call(
        functools.partial(_router_kernel, tm=tm),
        out_shape=(
            jax.ShapeDtypeStruct((TOP_K, t), jnp.int32),
            jax.ShapeDtypeStruct((TOP_K, t), F32),
            jax.ShapeDtypeStruct((TOP_K, t), jnp.int32),
            jax.ShapeDtypeStruct((ne, LANES), F32),
        ),
        grid=(t // tm,),
        in_specs=[
            pl.BlockSpec((tm, d), lambda i: (i, 0)),
            pl.BlockSpec((ne, d), lambda i: (0, 0)),
            pl.BlockSpec((ne, 1), lambda i: (0, 0)),
        ],
        out_specs=(
            pl.BlockSpec((TOP_K, tm), lambda i: (0, i)),
            pl.BlockSpec((TOP_K, tm), lambda i: (0, i)),
            pl.BlockSpec((TOP_K, tm), lambda i: (0, i)),
            pl.BlockSpec((ne, LANES), lambda i: (0, 0)),
        ),
        scratch_shapes=[pltpu.VMEM((ne, 1), F32)],
        compiler_params=_params(("arbitrary",)),
        name="router",
    )(x, router_w.T, router_bias.reshape(ne, 1))


def _slots_kernel(pstart, eidx_ref, pos_ref, dest_ref):
    eidx = eidx_ref[...]
    base = jnp.zeros(eidx.shape, jnp.int32)
    for e in range(N_EXPERTS):
        base = jnp.where(eidx == e, pstart[e], base)
    dest_ref[...] = base + pos_ref[...]


def _slots(pstart, eidx, pos, tn):
    k, t = eidx.shape
    blk = lambda i, ps: (0, i)
    grid_spec = pltpu.PrefetchScalarGridSpec(
        num_scalar_prefetch=1,
        grid=(t // tn,),
        in_specs=[pl.BlockSpec((k, tn), blk), pl.BlockSpec((k, tn), blk)],
        out_specs=pl.BlockSpec((k, tn), blk),
    )
    return pl.pallas_call(
        _slots_kernel,
        out_shape=jax.ShapeDtypeStruct((k, t), jnp.int32),
        grid_spec=grid_spec,
        compiler_params=_params(("parallel",)),
        name="moe_slots",
    )(pstart, eidx, pos)


def _dispatch_kernel(dest_ref, x_ref, xs_hbm, sem):
    tm = x_ref.shape[0]

    def issue(r, carry):
        for j in range(TOP_K):
            pltpu.make_async_copy(x_ref.at[pl.ds(r, 1), :], xs_hbm.at[pl.ds(dest_ref[j, r], 1), :],
                                  sem).start(priority=j % 2)
        return carry

    lax.fori_loop(0, tm, issue, 0)
    for j in range(TOP_K):
        pltpu.make_async_copy(x_ref, xs_hbm.at[pl.ds(0, tm), :], sem).wait()


def _dispatch(dest, x, rows, tm):
    t, d = x.shape
    return pl.pallas_call(
        _dispatch_kernel,
        out_shape=jax.ShapeDtypeStruct((rows, d), x.dtype),
        grid=(t // tm,),
        in_specs=[
            pl.BlockSpec((TOP_K, tm), lambda i: (0, i), memory_space=pltpu.SMEM),
            pl.BlockSpec((tm, d), lambda i: (i, 0)),
        ],
        out_specs=pl.BlockSpec(memory_space=pl.ANY),
        scratch_shapes=[pltpu.SemaphoreType.DMA(())],
        compiler_params=_params(("arbitrary",)),
        name="moe_dispatch",
    )(dest, x)


def _experts_kernel(blk_e, blk_rows, nvalid, xs_ref, wg_ref, wu_ref, wd_ref, o_ref, wgu_s, wd_s):
    i = pl.program_id(0)
    f = wg_ref.shape[-1]
    half = xs_ref.shape[1]

    @pl.when(i < nvalid[0])
    def _():
        e = blk_e[i]
        prev = blk_e[jnp.maximum(i - 1, 0)]

        @pl.when((i == 0) | (e != prev))
        def _cast():
            wgu_s[:, 0:f] = wg_ref[...].astype(BF16)
            wgu_s[:, f:2 * f] = wu_ref[...].astype(BF16)
            wd_s[...] = wd_ref[...].astype(BF16)

        row = lax.broadcasted_iota(jnp.int32, xs_ref.shape, 0)
        x_lo, x_hi = _unpack_bf16_pair(jnp.where(row < blk_rows[i], xs_ref[...], 0))
        gu = _dot(x_lo.astype(BF16), wgu_s[0:half, :]) + _dot(x_hi.astype(BF16), wgu_s[half:2 * half, :])
        hid = _silu(gu[:, 0:f]) * gu[:, f:2 * f]
        out = _dot(hid.astype(BF16), wd_s[...])
        o_ref[...] = _pack_bf16_pair(out[:, 0:half], out[:, half:])

    @pl.when(i >= nvalid[0])
    def _():
        o_ref[...] = jnp.zeros(o_ref.shape, o_ref.dtype)


def _experts(xs, blk_e, blk_rows, nvalid, w_gate, w_up, w_down, layer):
    p, half = xs.shape
    d = 2 * half
    f = w_gate.shape[-1]
    bm = MOE_BLOCK
    row = lambda i, be, br, nv: (jnp.minimum(i, nv[0] - 1), 0)
    wmap = lambda i, be, br, nv: (layer, be[i], 0, 0)
    grid_spec = pltpu.PrefetchScalarGridSpec(
        num_scalar_prefetch=3,
        grid=(p // bm,),
        in_specs=[
            pl.BlockSpec((bm, half), row),
            pl.BlockSpec((None, None, d, f), wmap),
            pl.BlockSpec((None, None, d, f), wmap),
            pl.BlockSpec((None, None, f, d), wmap),
        ],
        out_specs=pl.BlockSpec((bm, half), lambda i, be, br, nv: (i, 0)),
        scratch_shapes=[pltpu.VMEM((d, 2 * f), BF16), pltpu.VMEM((f, d), BF16)],
    )
    return pl.pallas_call(
        _experts_kernel,
        out_shape=jax.ShapeDtypeStruct((p, half), jnp.int32),
        grid_spec=grid_spec,
        compiler_params=_params(("arbitrary",)),
        name="routed_experts",
    )(blk_e, blk_rows, nvalid, xs, w_gate, w_up, w_down)


def _shared_ln_kernel(dest_ref, x_ref, xb_ref, wt_ref, ys_hbm, wgu_ref, wd_ref, g_ref, b_ref, o_ref, ob_ref,
                      buf, sem, *, alpha):
    f = wd_ref.shape[0]
    tm = x_ref.shape[0]

    def issue(r, carry):
        for j in range(TOP_K):
            pltpu.make_async_copy(ys_hbm.at[pl.ds(dest_ref[j, r], 1), :], buf.at[j, pl.ds(r, 1), :],
                                  sem).start(priority=j % 2)
        return carry

    lax.fori_loop(0, tm, issue, 0)
    gu = _dot(xb_ref[...], wgu_ref[...])
    hid = _silu(gu[:, 0:f]) * gu[:, f:2 * f]
    ffn = _dot(hid.astype(BF16), wd_ref[...])
    for j in range(TOP_K):
        pltpu.make_async_copy(ys_hbm.at[pl.ds(0, tm), :], buf.at[j], sem).wait()
    half = buf.shape[2]
    r_lo = jnp.zeros((tm, half), F32)
    r_hi = jnp.zeros((tm, half), F32)
    for j in range(TOP_K):
        y_lo, y_hi = _unpack_bf16_pair(buf[j])
        wj = wt_ref[:, j:j + 1]
        r_lo = r_lo + wj * y_lo
        r_hi = r_hi + wj * y_hi
    ffn = ffn + jnp.concatenate([r_lo, r_hi], axis=1)
    y = _layernorm(alpha * x_ref[...] + ffn, g_ref[...], b_ref[...])
    o_ref[...] = y
    ob_ref[...] = y.astype(BF16)


def _shared_ln(dest, x, xb, wts_t, ys, wgu, wd, g, b, alpha, tm):
    t, d = x.shape
    f = wd.shape[0]
    const = lambda i: (0, 0)
    tok = lambda i: (i, 0)
    return pl.pallas_call(
        functools.partial(_shared_ln_kernel, alpha=alpha),
        out_shape=(jax.ShapeDtypeStruct((t, d), F32), jax.ShapeDtypeStruct((t, d), BF16)),
        grid=(t // tm,),
        in_specs=[
            pl.BlockSpec((TOP_K, tm), lambda i: (0, i), memory_space=pltpu.SMEM),
            pl.BlockSpec((tm, d), tok),
            pl.BlockSpec((tm, d), tok),
            pl.BlockSpec((tm, TOP_K), tok),
            pl.BlockSpec(memory_space=pl.ANY),
            pl.BlockSpec((d, 2 * f), const),
            pl.BlockSpec((f, d), const),
            pl.BlockSpec((1, d), const),
            pl.BlockSpec((1, d), const),
        ],
        out_specs=(pl.BlockSpec((tm, d), tok), pl.BlockSpec((tm, d), tok)),
        scratch_shapes=[pltpu.VMEM((TOP_K, tm, d // 2), jnp.int32), pltpu.SemaphoreType.DMA(())],
        compiler_params=_params(("arbitrary",)),
        name="shared_expert_combine_layernorm",
    )(dest, x, xb, wts_t, ys, wgu, wd, g.reshape(1, d), b.reshape(1, d))


def _moe_layout(cnt, t):
    counts = cnt[:, 0].astype(jnp.int32)
    padded = (counts + MOE_BLOCK - 1) // MOE_BLOCK * MOE_BLOCK
    pend = jnp.cumsum(padded)
    pstart = pend - padded
    nblk = (t * TOP_K + N_EXPERTS * MOE_BLOCK) // MOE_BLOCK
    nvalid = pend[-1] // MOE_BLOCK
    starts = jnp.minimum(jnp.arange(nblk, dtype=jnp.int32), nvalid - 1) * MOE_BLOCK
    blk_e = jnp.sum((pend[None, :] <= starts[:, None]).astype(jnp.int32), axis=1)
    used_end = jnp.sum(jnp.where(blk_e[:, None] == jnp.arange(N_EXPERTS)[None, :], (pstart + counts)[None, :], 0),
                       axis=1)
    blk_rows = jnp.clip(used_end - starts, 0, MOE_BLOCK).astype(jnp.int32)
    return pstart.astype(jnp.int32), blk_e.astype(jnp.int32), blk_rows, nvalid.astype(jnp.int32).reshape(1), nblk


def kernel(x, w_in, da_lambda, da_norm_g, gla_gate_w2, gla_gate_b, gla_norm_g, ssd_conv_w, ssd_conv_b,
           ssd_dt_bias, ssd_a_log, ssd_d, ssd_norm_g, w_branch, w_out, ln1_g, ln1_b, router_w, router_bias,
           exp_w_gate, exp_w_up, exp_w_down, sh_w_gate, sh_w_up, sh_w_down, ln2_g, ln2_b):
    batch, seq, d = x.shape
    depth = w_in.shape[0]
    t = batch * seq
    alpha = (2 * depth) ** 0.25

    da_w = DA_HEADS * DA_V_DIM
    gk_w = GLA_HEADS * GLA_DK
    gv_w = GLA_HEADS * GLA_DV
    conv_dim = SSD_WIDTH + 2 * SSD_GROUPS * SSD_STATE
    sizes = (da_w, da_w, da_w, gk_w, gk_w, gv_w, GLA_GATE_RANK, gv_w, SSD_WIDTH, conv_dim, SSD_HEADS,
             N_BRANCH * d)
    offs = np.concatenate([[0], np.cumsum(sizes)]).tolist()
    (o_daq, o_dak, o_dav, o_gq, o_gk, o_gv, o_glr, o_gr, o_sz, o_sx, o_sdt, o_mg, o_end) = offs

    ones = lambda n: jnp.ones((1, n), F32)
    qk_scale = jnp.concatenate([jnp.full((1, da_w), DA_HEAD_DIM ** -0.5, F32), ones(da_w)], axis=1)

    xf = x.reshape(t, d)
    xb = xf.astype(BF16)
    for l in range(depth):
        wl = w_in[l]
        w_qk = wl[:, o_daq:o_dav].astype(BF16)
        w_vt = wl[:, o_dav:o_gq].T.astype(BF16)
        w_gla = jnp.concatenate([wl[:, o_gq:o_glr], wl[:, o_gr:o_sz]], axis=1).astype(BF16)
        w_ssd = wl[:, o_sz:o_sdt].astype(BF16)
        w_small = jnp.concatenate([
            jnp.pad(wl[:, o_glr:o_gr], ((0, 0), (0, LANES - GLA_GATE_RANK))),
            jnp.pad(wl[:, o_sdt:o_mg], ((0, 0), (0, LANES - SSD_HEADS)))], axis=1).astype(BF16)
        w_mg = wl[:, o_mg:o_end].reshape(d, N_BRANCH, d).transpose(1, 0, 2).astype(BF16)

        qk = _matmul(xb, w_qk, qk_scale, BF16, 512, 1024, "proj_attn_qk")
        vt = _matmul_nt(w_vt, xb, BF16, 512, "proj_attn_vt")
        pg = _matmul(xb, w_gla, ones(w_gla.shape[1]), F32, 512, 1024, "proj_gla")
        ps = _matmul(xb, w_ssd, ones(w_ssd.shape[1]), F32, 512, 1024, "proj_ssd")
        small = _matmul(xb, w_small, ones(2 * LANES), F32, 512, 2 * LANES, "proj_small")

        br_a = _diff_attention(qk, vt, da_lambda[l], da_norm_g[l], l, batch, seq)
        w2p = jnp.pad(gla_gate_w2[l], ((0, LANES - GLA_GATE_RANK), (0, 0))).astype(BF16)
        br_b = _gla(pg, small, w2p, gla_gate_b[l].reshape(1, gk_w), gla_norm_g[l], batch, seq)
        br_c = _ssd(ps, small, ssd_conv_w[l], ssd_conv_b[l], ssd_dt_bias[l], ssd_a_log[l], ssd_d[l],
                    ssd_norm_g[l], batch, seq)

        merged = _merge(xb, w_mg, br_a, br_b, br_c, w_branch[l].astype(BF16), 512, 512)
        xf, xb, xp = _outproj_ln(merged, w_out[l].astype(BF16), xf, ln1_g[l], ln1_b[l], alpha, TOKEN_TILE)

        eidx, wts, pos, cnt = _router(xf, router_w[l], router_bias[l], TOKEN_TILE)
        pstart, blk_e, blk_rows, nvalid, nblk = _moe_layout(cnt, t)
        dest = _slots(pstart, eidx, pos, min(t, 2048))
        xs = _dispatch(dest, xp, nblk * MOE_BLOCK, TOKEN_TILE)
        ys = _experts(xs, blk_e, blk_rows, nvalid, exp_w_gate, exp_w_up, exp_w_down, l)
        wgu = jnp.concatenate([sh_w_gate[l], sh_w_up[l]], axis=1).astype(BF16)
        xf, xb = _shared_ln(dest, xf, xb, wts.T, ys, wgu, sh_w_down[l].astype(BF16), ln2_g[l], ln2_b[l], alpha,
                            TOKEN_TILE)
    return xf.reshape(batch, seq, d)
```

```python
import functools
import math

import jax
import jax.numpy as jnp
import numpy as np
from jax import lax
from jax.experimental import pallas as pl
from jax.experimental.pallas import tpu as pltpu

F32 = jnp.float32
BF16 = jnp.bfloat16

DA_HEADS = 4
DA_HEAD_DIM = 128
DA_V_DIM = 2 * DA_HEAD_DIM
GLA_HEADS = 4
GLA_DK = 128
GLA_DV = 256
GLA_GATE_RANK = 16
GLA_TAU = 16.0
GLA_CHUNK = 64
SSD_HEADS = 16
SSD_HEAD_DIM = 64
SSD_STATE = 128
SSD_GROUPS = 4
SSD_CONV = 4
SSD_CHUNK = 128
SSD_WIDTH = SSD_HEADS * SSD_HEAD_DIM
N_BRANCH = 3
BRANCH_WIDTH = 1024
N_EXPERTS = 64
TOP_K = 8
N_EXPERT_GROUPS = 8
TOPK_GROUPS = 4
ROUTED_SCALE = 2.5
EPS = 1e-5

LANES = 128
SUBLANES = 8
VMEM_LIMIT = 52 * 1024 * 1024

ATTN_BLOCK = 512
ATTN_HEADS_PER_STEP = 2
GLA_BLOCK = 512
MOE_BLOCK = 512
TOKEN_TILE = 256


def _params(sem):
    return pltpu.CompilerParams(dimension_semantics=sem, vmem_limit_bytes=VMEM_LIMIT)


def _silu(x):
    return x * (1.0 / (1.0 + jnp.exp(-x)))


def _sigmoid(x):
    return 1.0 / (1.0 + jnp.exp(-x))


def _split3(x):
    h1 = x.astype(BF16)
    r1 = x - h1.astype(F32)
    h2 = r1.astype(BF16)
    r2 = r1 - h2.astype(F32)
    return h1, h2, r2.astype(BF16)


HI_HALF = -65536


def _pack_bf16_pair(lo, hi):
    lo_bits = lax.bitcast_convert_type(lo.astype(BF16).astype(F32), jnp.int32)
    hi_bits = lax.bitcast_convert_type(hi.astype(BF16).astype(F32), jnp.int32)
    return lax.shift_right_logical(lo_bits, 16) | (hi_bits & HI_HALF)


def _unpack_bf16_pair(w):
    lo = lax.bitcast_convert_type(lax.shift_left(w, 16), F32)
    hi = lax.bitcast_convert_type(w & HI_HALF, F32)
    return lo, hi


def _dot(a, b):
    return jnp.dot(a, b, preferred_element_type=F32)


def _dot_nt(a, b):
    return lax.dot_general(a, b, (((1,), (1,)), ((), ())), preferred_element_type=F32)


def _dot_tn(a, b):
    return lax.dot_general(a, b, (((0,), (0,)), ((), ())), preferred_element_type=F32)


def _mm_kernel(a_ref, w_ref, s_ref, o_ref):
    acc = _dot(a_ref[...], w_ref[...])
    o_ref[...] = (acc * s_ref[...]).astype(o_ref.dtype)


def _matmul(a, w, col_scale, out_dtype, tm, tn, name):
    m, k = a.shape
    n = w.shape[1]
    return pl.pallas_call(
        _mm_kernel,
        out_shape=jax.ShapeDtypeStruct((m, n), out_dtype),
        grid=(n // tn, m // tm),
        in_specs=[
            pl.BlockSpec((tm, k), lambda j, i: (i, 0)),
            pl.BlockSpec((k, tn), lambda j, i: (0, j)),
            pl.BlockSpec((1, tn), lambda j, i: (0, j)),
        ],
        out_specs=pl.BlockSpec((tm, tn), lambda j, i: (i, j)),
        compiler_params=_params(("parallel", "arbitrary")),
        name=name,
    )(a, w, col_scale)


def _mm_nt_kernel(wt_ref, a_ref, o_ref):
    o_ref[...] = _dot_nt(wt_ref[...], a_ref[...]).astype(o_ref.dtype)


def _matmul_nt(wt, a, out_dtype, tm, name):
    n, k = wt.shape
    m = a.shape[0]
    return pl.pallas_call(
        _mm_nt_kernel,
        out_shape=jax.ShapeDtypeStruct((n, m), out_dtype),
        grid=(m // tm,),
        in_specs=[
            pl.BlockSpec((n, k), lambda i: (0, 0)),
            pl.BlockSpec((tm, k), lambda i: (i, 0)),
        ],
        out_specs=pl.BlockSpec((n, tm), lambda i: (0, i)),
        compiler_params=_params(("parallel",)),
        name=name,
    )(wt, a)


def _attn_kernel(qi_tab, ki_tab, q_ref, k_ref, vt_ref, bt_ref, slope_ref, lam_ref, g_ref, o_ref,
                 m_ref, l_ref, acc_ref, *, blk, lam_init):
    p = pl.program_id(2)
    qi = qi_tab[p]
    ki = ki_tab[p]

    @pl.when(ki == 0)
    def _init():
        m_ref[...] = jnp.full(m_ref.shape, -jnp.inf, F32)
        l_ref[...] = jnp.zeros(l_ref.shape, F32)
        acc_ref[...] = jnp.zeros(acc_ref.shape, F32)

    rel = jnp.full((1, blk), (ki - qi) * blk, jnp.int32).astype(F32)

    def step(masked):
        if masked:
            kc = lax.broadcasted_iota(jnp.int32, (blk, blk), 0)
            qr = lax.broadcasted_iota(jnp.int32, (blk, blk), 1)
            keep = kc <= qr
        for hd in range(ATTN_HEADS_PER_STEP):
            c0 = slope_ref[hd] * rel
            bt = bt_ref[hd]
            vt = vt_ref[hd * DA_V_DIM:(hd + 1) * DA_V_DIM, :]
            for mp in range(2):
                sl = slice(hd * DA_V_DIM + mp * DA_HEAD_DIM, hd * DA_V_DIM + (mp + 1) * DA_HEAD_DIM)
                si = 2 * hd + mp
                s = _dot_nt(k_ref[:, sl], q_ref[:, sl]) + bt
                if masked:
                    s = jnp.where(keep, s, -jnp.inf)
                m_old = m_ref[si]
                m_new = jnp.maximum(m_old, jnp.max(s, axis=0, keepdims=True) + c0)
                alpha = jnp.exp(m_old - m_new)
                pr = jnp.exp(s - (m_new - c0))
                l_ref[si] = alpha * l_ref[si] + jnp.sum(pr, axis=0, keepdims=True)
                acc_ref[si] = alpha * acc_ref[si] + _dot(vt, pr.astype(BF16))
                m_ref[si] = m_new

    @pl.when(ki < qi)
    def _off_diag():
        step(False)

    @pl.when(ki == qi)
    def _diag():
        step(True)
        lp = lam_ref[...]
        lam = (jnp.exp(jnp.sum(lp[0:1] * lp[1:2], axis=1, keepdims=True))
               - jnp.exp(jnp.sum(lp[2:3] * lp[3:4], axis=1, keepdims=True)) + lam_init)
        for hd in range(ATTN_HEADS_PER_STEP):
            ot = acc_ref[2 * hd] / l_ref[2 * hd] - lam * (acc_ref[2 * hd + 1] / l_ref[2 * hd + 1])
            o = ot.T
            o = o * lax.rsqrt(jnp.mean(o * o, axis=1, keepdims=True) + EPS)
            o_ref[:, hd * DA_V_DIM:(hd + 1) * DA_V_DIM] = (o * g_ref[...] * (1.0 - lam_init)).astype(o_ref.dtype)


def _diff_attention(qk, vt, lam_params, norm_g, layer_idx, batch, seq):
    blk = min(ATTN_BLOCK, seq)
    nq = seq // blk
    pairs = [(i, j) for i in range(nq) for j in range(i + 1)]
    qi_tab = jnp.asarray([p[0] for p in pairs], jnp.int32)
    ki_tab = jnp.asarray([p[1] for p in pairs], jnp.int32)
    lam_init = 0.8 - 0.6 * math.exp(-0.3 * layer_idx)
    h = DA_HEADS
    slopes = np.asarray([2.0 ** (-8.0 * (i + 1) / h) for i in range(h)], np.float32)
    key_off = np.arange(blk, dtype=np.float32)
    bias_t = jnp.asarray(np.broadcast_to((slopes[:, None] * key_off[None, :])[:, :, None], (h, blk, blk)))
    slopes = jnp.asarray(np.broadcast_to(slopes[:, None, None], (h, 1, blk)))
    t = batch * seq
    hs = ATTN_HEADS_PER_STEP
    hg = h // hs
    kern = functools.partial(_attn_kernel, blk=blk, lam_init=lam_init)
    grid_spec = pltpu.PrefetchScalarGridSpec(
        num_scalar_prefetch=2,
        grid=(batch, hg, len(pairs)),
        in_specs=[
            pl.BlockSpec((blk, hs * DA_V_DIM), lambda b, hh, p, qt, kt: (b * nq + qt[p], hh)),
            pl.BlockSpec((blk, hs * DA_V_DIM), lambda b, hh, p, qt, kt: (b * nq + kt[p], hg + hh)),
            pl.BlockSpec((hs * DA_V_DIM, blk), lambda b, hh, p, qt, kt: (hh, b * nq + kt[p])),
            pl.BlockSpec((hs, blk, blk), lambda b, hh, p, qt, kt: (hh, 0, 0)),
            pl.BlockSpec((hs, 1, blk), lambda b, hh, p, qt, kt: (hh, 0, 0)),
            pl.BlockSpec((4, DA_HEAD_DIM), lambda b, hh, p, qt, kt: (0, 0)),
            pl.BlockSpec((1, DA_V_DIM), lambda b, hh, p, qt, kt: (0, 0)),
        ],
        out_specs=pl.BlockSpec((blk, hs * DA_V_DIM), lambda b, hh, p, qt, kt: (b * nq + qt[p], hh)),
        scratch_shapes=[
            pltpu.VMEM((2 * hs, 1, blk), F32),
            pltpu.VMEM((2 * hs, 1, blk), F32),
            pltpu.VMEM((2 * hs, DA_V_DIM, blk), F32),
        ],
    )
    return pl.pallas_call(
        kern,
        out_shape=jax.ShapeDtypeStruct((t, h * DA_V_DIM), BF16),
        grid_spec=grid_spec,
        compiler_params=_params(("parallel", "parallel", "arbitrary")),
        name="diff_attention",
    )(qi_tab, ki_tab, qk, qk, vt, bias_t, slopes, lam_params, norm_g.reshape(1, DA_V_DIM))


def _gla_kernel(pg_ref, sm_ref, w2_ref, b2_ref, g_ref, o_ref, st_ref, *, blk):
    n = pl.program_id(1)
    hk = GLA_HEADS * GLA_DK
    hv = GLA_HEADS * GLA_DV
    c_len = GLA_CHUNK

    @pl.when(n == 0)
    def _init():
        st_ref[...] = jnp.zeros(st_ref.shape, F32)

    lr = sm_ref[:, 0:LANES].astype(BF16)
    gl = _dot(lr, w2_ref[...]) + b2_ref[...]
    gk = (jnp.minimum(gl, 0.0) - jnp.log(1.0 + jnp.exp(-jnp.abs(gl)))) * (1.0 / GLA_TAU)
    r = lax.broadcasted_iota(jnp.int32, (blk, blk), 0)
    c = lax.broadcasted_iota(jnp.int32, (blk, blk), 1)
    tri = jnp.where((c <= r) & ((r // c_len) == (c // c_len)), 1.0, 0.0).astype(BF16)
    g1, g2, g3 = _split3(gk)
    bcum = _dot(tri, g1) + _dot(tri, g2) + _dot(tri, g3)

    rr = lax.broadcasted_iota(jnp.int32, (c_len, c_len), 0)
    cc = lax.broadcasted_iota(jnp.int32, (c_len, c_len), 1)
    causal = cc <= rr
    scale = GLA_DK ** -0.5
    for ci in range(blk // c_len):
        rows = slice(ci * c_len, (ci + 1) * c_len)
        for h in range(GLA_HEADS):
            kc = slice(h * GLA_DK, (h + 1) * GLA_DK)
            vc = slice(h * GLA_DV, (h + 1) * GLA_DV)
            b = bcum[rows, kc]
            b_last = b[c_len - 1:c_len, :]
            q = pg_ref[rows, kc]
            k = pg_ref[rows, hk + h * GLA_DK: hk + (h + 1) * GLA_DK]
            v = pg_ref[rows, 2 * hk + h * GLA_DV: 2 * hk + (h + 1) * GLA_DV].astype(BF16)
            rg = pg_ref[rows, 2 * hk + hv + h * GLA_DV: 2 * hk + hv + (h + 1) * GLA_DV]
            q_e = (q * scale * jnp.exp(b)).astype(BF16)
            k_e = (k * jnp.exp(-b)).astype(BF16)
            k_d = (k * jnp.exp(b_last - b)).astype(BF16)
            att = jnp.where(causal, _dot_nt(q_e, k_e), 0.0)
            st = st_ref[h]
            o = _dot(att.astype(BF16), v) + _dot_nt(q_e, st.astype(BF16))
            st_ref[h] = st * jnp.exp(b_last) + _dot_tn(v, k_d)
            o = o * lax.rsqrt(jnp.mean(o * o, axis=1, keepdims=True) + EPS) * g_ref[...]
            o_ref[rows, vc] = (o * _silu(rg)).astype(o_ref.dtype)


def _gla(pg, small, w2p, b2, norm_g, batch, seq):
    blk = min(GLA_BLOCK, seq)
    nb = seq // blk
    t = batch * seq
    hk = GLA_HEADS * GLA_DK
    hv = GLA_HEADS * GLA_DV
    width = pg.shape[1]
    return pl.pallas_call(
        functools.partial(_gla_kernel, blk=blk),
        out_shape=jax.ShapeDtypeStruct((t, hv), BF16),
        grid=(batch, nb),
        in_specs=[
            pl.BlockSpec((blk, width), lambda b, n: (b * nb + n, 0)),
            pl.BlockSpec((blk, small.shape[1]), lambda b, n: (b * nb + n, 0)),
            pl.BlockSpec((LANES, hk), lambda b, n: (0, 0)),
            pl.BlockSpec((1, hk), lambda b, n: (0, 0)),
            pl.BlockSpec((1, GLA_DV), lambda b, n: (0, 0)),
        ],
        out_specs=pl.BlockSpec((blk, hv), lambda b, n: (b * nb + n, 0)),
        scratch_shapes=[pltpu.VMEM((GLA_HEADS, GLA_DV, GLA_DK), F32)],
        compiler_params=_params(("parallel", "arbitrary")),
        name="gla",
    )(pg, small, w2p, b2, norm_g.reshape(1, GLA_DV))


def _ssd_kernel(ps_ref, sm_ref, cw_ref, cb_ref, dtb_ref, alog_ref, dsk_ref, g_ref, o_ref,
                tail_ref, st_ref):
    n = pl.program_id(1)
    q_len = SSD_CHUNK
    w = SSD_WIDTH
    gs = SSD_GROUPS * SSD_STATE
    heads_per_group = SSD_HEADS // SSD_GROUPS
    gw = heads_per_group * SSD_HEAD_DIM

    @pl.when(n == 0)
    def _init():
        tail_ref[...] = jnp.zeros(tail_ref.shape, F32)
        st_ref[...] = jnp.zeros(st_ref.shape, F32)

    cur = ps_ref[:, w:w + w + 2 * gs]
    ext = jnp.concatenate([tail_ref[...], cur], axis=0)
    acc = cb_ref[...] + cw_ref[0:1, :] * ext[SUBLANES - 3:SUBLANES - 3 + q_len]
    for i in range(1, SSD_CONV):
        off = SUBLANES - (SSD_CONV - 1) + i
        acc = acc + cw_ref[i:i + 1, :] * ext[off:off + q_len]
    tail_ref[...] = cur[q_len - SUBLANES:q_len]
    xbc = _silu(acc)
    xs = xbc[:, 0:w]
    bm = xbc[:, w:w + gs]
    cm = xbc[:, w + gs:w + 2 * gs]

    dtr = sm_ref[:, LANES:2 * LANES] + dtb_ref[...]
    dtv = jnp.maximum(dtr, 0.0) + jnp.log(1.0 + jnp.exp(-jnp.abs(dtr)))
    da = dtv * (-jnp.exp(alog_ref[...]))
    r = lax.broadcasted_iota(jnp.int32, (q_len, q_len), 0)
    c = lax.broadcasted_iota(jnp.int32, (q_len, q_len), 1)
    causal = c <= r
    tri = jnp.where(causal, 1.0, 0.0).astype(BF16)
    d1, d2, d3 = _split3(da)
    a_cs = _dot(tri, d1) + _dot(tri, d2) + _dot(tri, d3)
    a_cs_t = a_cs.T
    a_last = a_cs[q_len - 1:q_len, :]
    e_last = jnp.exp(a_last)

    for g in range(SSD_GROUPS):
        bm_g = bm[:, g * SSD_STATE:(g + 1) * SSD_STATE]
        cm_g = cm[:, g * SSD_STATE:(g + 1) * SSD_STATE]
        bm_b = bm_g.astype(BF16)
        cm_b = cm_g.astype(BF16)
        cb = _dot_nt(cm_b, bm_b)
        y_diag = []
        xdd = []
        e_col = []
        e_row = []
        for rh in range(heads_per_group):
            h = g * heads_per_group + rh
            col = a_cs[:, h:h + 1]
            row = a_cs_t[h:h + 1, :]
            lm = jnp.exp(jnp.where(causal, col - row, -jnp.inf))
            xdt = xs[:, h * SSD_HEAD_DIM:(h + 1) * SSD_HEAD_DIM] * dtv[:, h:h + 1]
            y_diag.append(_dot((cb * lm).astype(BF16), xdt.astype(BF16)))
            xdd.append(xdt * jnp.exp(a_last[:, h:h + 1] - col))
            e_col.append(jnp.broadcast_to(jnp.exp(col), (q_len, SSD_HEAD_DIM)))
            e_row.append(jnp.broadcast_to(e_last[:, h:h + 1], (1, SSD_HEAD_DIM)))
        y_diag = jnp.concatenate(y_diag, axis=1)
        xdd = jnp.concatenate(xdd, axis=1)
        e_col = jnp.concatenate(e_col, axis=1)
        e_row = jnp.concatenate(e_row, axis=1)
        s_prev = st_ref[g]
        y_off = _dot(cm_b, s_prev.astype(BF16)) * e_col
        st_ref[g] = s_prev * e_row + _dot_tn(bm_b, xdd.astype(BF16))
        lanes = slice(g * gw, (g + 1) * gw)
        y = y_diag + y_off + xs[:, lanes] * dsk_ref[:, lanes]
        y = y * _silu(ps_ref[:, lanes])
        y = y * lax.rsqrt(jnp.mean(y * y, axis=1, keepdims=True) + EPS) * g_ref[:, lanes]
        o_ref[:, lanes] = y.astype(o_ref.dtype)


def _ssd(ps, small, conv_w, conv_b, dt_bias, a_log, d_skip, norm_g, batch, seq):
    q_len = SSD_CHUNK
    nb = seq // q_len
    t = batch * seq
    conv_dim = conv_w.shape[1]
    pad = LANES - SSD_HEADS
    dtb = jnp.pad(dt_bias, (0, pad)).reshape(1, LANES)
    alog = jnp.pad(a_log, (0, pad)).reshape(1, LANES)
    dsk = jnp.repeat(d_skip, SSD_HEAD_DIM).reshape(1, SSD_WIDTH)
    const = lambda b, n: (0, 0)
    return pl.pallas_call(
        _ssd_kernel,
        out_shape=jax.ShapeDtypeStruct((t, SSD_WIDTH), BF16),
        grid=(batch, nb),
        in_specs=[
            pl.BlockSpec((q_len, ps.shape[1]), lambda b, n: (b * nb + n, 0)),
            pl.BlockSpec((q_len, small.shape[1]), lambda b, n: (b * nb + n, 0)),
            pl.BlockSpec((SSD_CONV, conv_dim), const),
            pl.BlockSpec((1, conv_dim), const),
            pl.BlockSpec((1, LANES), const),
            pl.BlockSpec((1, LANES), const),
            pl.BlockSpec((1, SSD_WIDTH), const),
            pl.BlockSpec((1, SSD_WIDTH), const),
        ],
        out_specs=pl.BlockSpec((q_len, SSD_WIDTH), lambda b, n: (b * nb + n, 0)),
        scratch_shapes=[
            pltpu.VMEM((SUBLANES, conv_dim), F32),
            pltpu.VMEM((SSD_GROUPS, SSD_STATE, SSD_WIDTH // SSD_GROUPS), F32),
        ],
        compiler_params=_params(("parallel", "arbitrary")),
        name="ssd",
    )(ps, small, conv_w, conv_b.reshape(1, conv_dim), dtb, alog, dsk, norm_g.reshape(1, SSD_WIDTH))


def _merge_kernel(x_ref, wg_ref, ba_ref, bb_ref, bc_ref, wb_ref, o_ref):
    x = x_ref[...]
    acc = None
    for i, br in enumerate((ba_ref, bb_ref, bc_ref)):
        gate = _sigmoid(_dot(x, wg_ref[i]))
        term = gate * _dot(br[...], wb_ref[i])
        acc = term if acc is None else acc + term
    o_ref[...] = acc.astype(o_ref.dtype)


def _merge(xb, wgate, br_a, br_b, br_c, wbr, tm, tn):
    t, d = xb.shape
    bw = br_a.shape[1]
    return pl.pallas_call(
        _merge_kernel,
        out_shape=jax.ShapeDtypeStruct((t, d), BF16),
        grid=(t // tm, d // tn),
        in_specs=[
            pl.BlockSpec((tm, d), lambda i, j: (i, 0)),
            pl.BlockSpec((N_BRANCH, d, tn), lambda i, j: (0, 0, j)),
            pl.BlockSpec((tm, bw), lambda i, j: (i, 0)),
            pl.BlockSpec((tm, bw), lambda i, j: (i, 0)),
            pl.BlockSpec((tm, bw), lambda i, j: (i, 0)),
            pl.BlockSpec((N_BRANCH, bw, tn), lambda i, j: (0, 0, j)),
        ],
        out_specs=pl.BlockSpec((tm, tn), lambda i, j: (i, j)),
        compiler_params=_params(("parallel", "arbitrary")),
        name="gated_merge",
    )(xb, wgate, br_a, br_b, br_c, wbr)


def _layernorm(v, g, b):
    mu = jnp.mean(v, axis=1, keepdims=True)
    d = v - mu
    var = jnp.mean(d * d, axis=1, keepdims=True)
    return d * lax.rsqrt(var + EPS) * g + b


def _outproj_ln_kernel(m_ref, w_ref, x_ref, g_ref, b_ref, o_ref, ob_ref, op_ref, *, alpha):
    mix = _dot(m_ref[...], w_ref[...])
    y = _layernorm(alpha * x_ref[...] + mix, g_ref[...], b_ref[...])
    half = y.shape[1] // 2
    o_ref[...] = y
    ob_ref[...] = y.astype(BF16)
    op_ref[...] = _pack_bf16_pair(y[:, 0:half], y[:, half:])


def _outproj_ln(merged, w_out, x, g, b, alpha, tm):
    t, d = x.shape
    const = lambda i: (0, 0)
    tok = lambda i: (i, 0)
    return pl.pallas_call(
        functools.partial(_outproj_ln_kernel, alpha=alpha),
        out_shape=(jax.ShapeDtypeStruct((t, d), F32), jax.ShapeDtypeStruct((t, d), BF16),
                   jax.ShapeDtypeStruct((t, d // 2), jnp.int32)),
        grid=(t // tm,),
        in_specs=[
            pl.BlockSpec((tm, d), tok),
            pl.BlockSpec((d, d), const, pipeline_mode=pl.Buffered(1)),
            pl.BlockSpec((tm, d), tok),
            pl.BlockSpec((1, d), const),
            pl.BlockSpec((1, d), const),
        ],
        out_specs=(pl.BlockSpec((tm, d), tok), pl.BlockSpec((tm, d), tok), pl.BlockSpec((tm, d // 2), tok)),
        compiler_params=_params(("parallel",)),
        name="outproj_layernorm",
    )(merged, w_out, x, g.reshape(1, d), b.reshape(1, d))


def _router_kernel(x_ref, w_ref, bias_ref, eidx_ref, wt_ref, pos_ref, cnt_ref, carry_ref, *, tm):
    i = pl.program_id(0)
    ne = N_EXPERTS
    ng = N_EXPERT_GROUPS
    per = ne // ng

    @pl.when(i == 0)
    def _init():
        carry_ref[...] = jnp.zeros(carry_ref.shape, F32)

    x = x_ref[...]
    xh = x.astype(BF16)
    xl = (x - xh.astype(F32)).astype(BF16)
    w = w_ref[...]
    wh = w.astype(BF16)
    wl = (w - wh.astype(F32)).astype(BF16)
    logits = _dot_nt(wh, xh) + _dot_nt(wh, xl) + _dot_nt(wl, xh)
    scores = _sigmoid(logits)
    sel = scores + bias_ref[...]

    grp = sel.reshape(ng, per, tm)
    io_p = lax.broadcasted_iota(jnp.int32, (ng, per, tm), 1)
    m1 = jnp.max(grp, axis=1, keepdims=True)
    i1 = jnp.min(jnp.where(grp == m1, io_p, per), axis=1, keepdims=True)
    m2 = jnp.max(jnp.where(io_p == i1, -jnp.inf, grp), axis=1, keepdims=True)
    gscore = (m1 + m2).reshape(ng, tm)

    io_g = lax.broadcasted_iota(jnp.int32, (ng, tm), 0)
    gsel = jnp.zeros((ng, tm), F32)
    gwork = gscore
    for _ in range(TOPK_GROUPS):
        gm = jnp.max(gwork, axis=0, keepdims=True)
        gi = jnp.min(jnp.where(gwork == gm, io_g, ng), axis=0, keepdims=True)
        hit = io_g == gi
        gsel = jnp.where(hit, 1.0, gsel)
        gwork = jnp.where(hit, -jnp.inf, gwork)
    emask = jnp.broadcast_to(gsel.reshape(ng, 1, tm), (ng, per, tm)).reshape(ne, tm) > 0.5

    io_e = lax.broadcasted_iota(jnp.int32, (ne, tm), 0)
    work = jnp.where(emask, sel, -jnp.inf)
    member = jnp.zeros((ne, tm), F32)
    idx_rows = []
    w_rows = []
    for _ in range(TOP_K):
        mx = jnp.max(work, axis=0, keepdims=True)
        ei = jnp.min(jnp.where(work == mx, io_e, ne), axis=0, keepdims=True)
        hit = io_e == ei
        idx_rows.append(ei)
        w_rows.append(jnp.sum(jnp.where(hit, scores, 0.0), axis=0, keepdims=True))
        member = jnp.where(hit, 1.0, member)
        work = jnp.where(hit, -jnp.inf, work)
    wsum = w_rows[0]
    for wr in w_rows[1:]:
        wsum = wsum + wr
    inv = ROUTED_SCALE / wsum

    r = lax.broadcasted_iota(jnp.int32, (tm, tm), 0)
    c = lax.broadcasted_iota(jnp.int32, (tm, tm), 1)
    upper = jnp.where(r < c, 1.0, 0.0).astype(BF16)
    prefix = _dot(member.astype(BF16), upper) + carry_ref[...]
    for j in range(TOP_K):
        hit = io_e == idx_rows[j]
        eidx_ref[j:j + 1, :] = idx_rows[j]
        wt_ref[j:j + 1, :] = w_rows[j] * inv
        pos_ref[j:j + 1, :] = jnp.sum(jnp.where(hit, prefix, 0.0), axis=0, keepdims=True).astype(jnp.int32)
    carry = carry_ref[...] + jnp.sum(member, axis=1, keepdims=True)
    carry_ref[...] = carry
    cnt_ref[...] = jnp.broadcast_to(carry, cnt_ref.shape)


def _router(x, router_w, router_bias, tm):
    t, d = x.shape
    ne = N_EXPERTS
    return pl.pallas_call(
        functools.partial(_router_kernel, tm=tm),
        out_shape=(
            jax.ShapeDtypeStruct((TOP_K, t), jnp.int32),
            jax.ShapeDtypeStruct((TOP_K, t), F32),
            jax.ShapeDtypeStruct((TOP_K, t), jnp.int32),
            jax.ShapeDtypeStruct((ne, LANES), F32),
        ),
        grid=(t // tm,),
        in_specs=[
            pl.BlockSpec((tm, d), lambda i: (i, 0)),
            pl.BlockSpec((ne, d), lambda i: (0, 0)),
            pl.BlockSpec((ne, 1), lambda i: (0, 0)),
        ],
        out_specs=(
            pl.BlockSpec((TOP_K, tm), lambda i: (0, i)),
            pl.BlockSpec((TOP_K, tm), lambda i: (0, i)),
            pl.BlockSpec((TOP_K, tm), lambda i: (0, i)),
            pl.BlockSpec((ne, LANES), lambda i: (0, 0)),
        ),
        scratch_shapes=[pltpu.VMEM((ne, 1), F32)],
        compiler_params=_params(("arbitrary",)),
        name="router",
    )(x, router_w.T, router_bias.reshape(ne, 1))


def _slots_kernel(pstart, eidx_ref, pos_ref, dest_ref):
    eidx = eidx_ref[...]
    base = jnp.zeros(eidx.shape, jnp.int32)
    for e in range(N_EXPERTS):
        base = jnp.where(eidx == e, pstart[e], base)
    dest_ref[...] = base + pos_ref[...]


def _slots(pstart, eidx, pos, tn):
    k, t = eidx.shape
    blk = lambda i, ps: (0, i)
    grid_spec = pltpu.PrefetchScalarGridSpec(
        num_scalar_prefetch=1,
        grid=(t // tn,),
        in_specs=[pl.BlockSpec((k, tn), blk), pl.BlockSpec((k, tn), blk)],
        out_specs=pl.BlockSpec((k, tn), blk),
    )
    return pl.pallas_call(
        _slots_kernel,
        out_shape=jax.ShapeDtypeStruct((k, t), jnp.int32),
        grid_spec=grid_spec,
        compiler_params=_params(("parallel",)),
        name="moe_slots",
    )(pstart, eidx, pos)


def _dispatch_kernel(dest_ref, x_ref, xs_hbm, sem):
    tm = x_ref.shape[0]

    def issue(r, carry):
        for j in range(TOP_K):
            pltpu.make_async_copy(x_ref.at[pl.ds(r, 1), :], xs_hbm.at[pl.ds(dest_ref[j, r], 1), :],
                                  sem).start(priority=j % 2)
        return carry

    lax.fori_loop(0, tm, issue, 0)
    for j in range(TOP_K):
        pltpu.make_async_copy(x_ref, xs_hbm.at[pl.ds(0, tm), :], sem).wait()


def _dispatch(dest, x, rows, tm):
    t, d = x.shape
    return pl.pallas_call(
        _dispatch_kernel,
        out_shape=jax.ShapeDtypeStruct((rows, d), x.dtype),
        grid=(t // tm,),
        in_specs=[
            pl.BlockSpec((TOP_K, tm), lambda i: (0, i), memory_space=pltpu.SMEM),
            pl.BlockSpec((tm, d), lambda i: (i, 0)),
        ],
        out_specs=pl.BlockSpec(memory_space=pl.ANY),
        scratch_shapes=[pltpu.SemaphoreType.DMA(())],
        compiler_params=_params(("arbitrary",)),
        name="moe_dispatch",
    )(dest, x)


def _experts_kernel(blk_e, blk_rows, nvalid, xs_ref, wg_ref, wu_ref, wd_ref, o_ref, wgu_s, wd_s):
    i = pl.program_id(0)
    f = wg_ref.shape[-1]
    half = xs_ref.shape[1]

    @pl.when(i < nvalid[0])
    def _():
        e = blk_e[i]
        prev = blk_e[jnp.maximum(i - 1, 0)]

        @pl.when((i == 0) | (e != prev))
        def _cast():
            wgu_s[:, 0:f] = wg_ref[...].astype(BF16)
            wgu_s[:, f:2 * f] = wu_ref[...].astype(BF16)
            wd_s[...] = wd_ref[...].astype(BF16)

        row = lax.broadcasted_iota(jnp.int32, xs_ref.shape, 0)
        x_lo, x_hi = _unpack_bf16_pair(jnp.where(row < blk_rows[i], xs_ref[...], 0))
        gu = _dot(x_lo.astype(BF16), wgu_s[0:half, :]) + _dot(x_hi.astype(BF16), wgu_s[half:2 * half, :])
        hid = _silu(gu[:, 0:f]) * gu[:, f:2 * f]
        out = _dot(hid.astype(BF16), wd_s[...])
        o_ref[...] = _pack_bf16_pair(out[:, 0:half], out[:, half:])

    @pl.when(i >= nvalid[0])
    def _():
        o_ref[...] = jnp.zeros(o_ref.shape, o_ref.dtype)


def _experts(xs, blk_e, blk_rows, nvalid, w_gate, w_up, w_down, layer):
    p, half = xs.shape
    d = 2 * half
    f = w_gate.shape[-1]
    bm = MOE_BLOCK
    row = lambda i, be, br, nv: (jnp.minimum(i, nv[0] - 1), 0)
    wmap = lambda i, be, br, nv: (layer, be[i], 0, 0)
    grid_spec = pltpu.PrefetchScalarGridSpec(
        num_scalar_prefetch=3,
        grid=(p // bm,),
        in_specs=[
            pl.BlockSpec((bm, half), row),
            pl.BlockSpec((None, None, d, f), wmap),
            pl.BlockSpec((None, None, d, f), wmap),
            pl.BlockSpec((None, None, f, d), wmap),
        ],
        out_specs=pl.BlockSpec((bm, half), lambda i, be, br, nv: (i, 0)),
        scratch_shapes=[pltpu.VMEM((d, 2 * f), BF16), pltpu.VMEM((f, d), BF16)],
    )
    return pl.pallas_call(
        _experts_kernel,
        out_shape=jax.ShapeDtypeStruct((p, half), jnp.int32),
        grid_spec=grid_spec,
        compiler_params=_params(("arbitrary",)),
        name="routed_experts",
    )(blk_e, blk_rows, nvalid, xs, w_gate, w_up, w_down)


def _shared_ln_kernel(dest_ref, x_ref, xb_ref, wt_ref, ys_hbm, wgu_ref, wd_ref, g_ref, b_ref, o_ref, ob_ref,
                      buf, sem, *, alpha):
    f = wd_ref.shape[0]
    tm = x_ref.shape[0]

    def issue(r, carry):
        for j in range(TOP_K):
            pltpu.make_async_copy(ys_hbm.at[pl.ds(dest_ref[j, r], 1), :], buf.at[j, pl.ds(r, 1), :],
                                  sem).start(priority=j % 2)
        return carry

    lax.fori_loop(0, tm, issue, 0)
    gu = _dot(xb_ref[...], wgu_ref[...])
    hid = _silu(gu[:, 0:f]) * gu[:, f:2 * f]
    ffn = _dot(hid.astype(BF16), wd_ref[...])
    for j in range(TOP_K):
        pltpu.make_async_copy(ys_hbm.at[pl.ds(0, tm), :], buf.at[j], sem).wait()
    half = buf.shape[2]
    r_lo = jnp.zeros((tm, half), F32)
    r_hi = jnp.zeros((tm, half), F32)
    for j in range(TOP_K):
        y_lo, y_hi = _unpack_bf16_pair(buf[j])
        wj = wt_ref[:, j:j + 1]
        r_lo = r_lo + wj * y_lo
        r_hi = r_hi + wj * y_hi
    ffn = ffn + jnp.concatenate([r_lo, r_hi], axis=1)
    y = _layernorm(alpha * x_ref[...] + ffn, g_ref[...], b_ref[...])
    o_ref[...] = y
    ob_ref[...] = y.astype(BF16)


def _shared_ln(dest, x, xb, wts_t, ys, wgu, wd, g, b, alpha, tm):
    t, d = x.shape
    f = wd.shape[0]
    const = lambda i: (0, 0)
    tok = lambda i: (i, 0)
    return pl.pallas_call(
        functools.partial(_shared_ln_kernel, alpha=alpha),
        out_shape=(jax.ShapeDtypeStruct((t, d), F32), jax.ShapeDtypeStruct((t, d), BF16)),
        grid=(t // tm,),
        in_specs=[
            pl.BlockSpec((TOP_K, tm), lambda i: (0, i), memory_space=pltpu.SMEM),
            pl.BlockSpec((tm, d), tok),
            pl.BlockSpec((tm, d), tok),
            pl.BlockSpec((tm, TOP_K), tok),
            pl.BlockSpec(memory_space=pl.ANY),
            pl.BlockSpec((d, 2 * f), const),
            pl.BlockSpec((f, d), const),
            pl.BlockSpec((1, d), const),
            pl.BlockSpec((1, d), const),
        ],
        out_specs=(pl.BlockSpec((tm, d), tok), pl.BlockSpec((tm, d), tok)),
        scratch_shapes=[pltpu.VMEM((TOP_K, tm, d // 2), jnp.int32), pltpu.SemaphoreType.DMA(())],
        compiler_params=_params(("arbitrary",)),
        name="shared_expert_combine_layernorm",
    )(dest, x, xb, wts_t, ys, wgu, wd, g.reshape(1, d), b.reshape(1, d))


def _moe_layout(cnt, t):
    counts = cnt[:, 0].astype(jnp.int32)
    padded = (counts + MOE_BLOCK - 1) // MOE_BLOCK * MOE_BLOCK
    pend = jnp.cumsum(padded)
    pstart = pend - padded
    nblk = (t * TOP_K + N_EXPERTS * MOE_BLOCK) // MOE_BLOCK
    nvalid = pend[-1] // MOE_BLOCK
    starts = jnp.minimum(jnp.arange(nblk, dtype=jnp.int32), nvalid - 1) * MOE_BLOCK
    blk_e = jnp.sum((pend[None, :] <= starts[:, None]).astype(jnp.int32), axis=1)
    used_end = jnp.sum(jnp.where(blk_e[:, None] == jnp.arange(N_EXPERTS)[None, :], (pstart + counts)[None, :], 0),
                       axis=1)
    blk_rows = jnp.clip(used_end - starts, 0, MOE_BLOCK).astype(jnp.int32)
    return pstart.astype(jnp.int32), blk_e.astype(jnp.int32), blk_rows, nvalid.astype(jnp.int32).reshape(1), nblk


def kernel(x, w_in, da_lambda, da_norm_g, gla_gate_w2, gla_gate_b, gla_norm_g, ssd_conv_w, ssd_conv_b,
           ssd_dt_bias, ssd_a_log, ssd_d, ssd_norm_g, w_branch, w_out, ln1_g, ln1_b, router_w, router_bias,
           exp_w_gate, exp_w_up, exp_w_down, sh_w_gate, sh_w_up, sh_w_down, ln2_g, ln2_b):
    batch, seq, d = x.shape
    depth = w_in.shape[0]
    t = batch * seq
    alpha = (2 * depth) ** 0.25

    da_w = DA_HEADS * DA_V_DIM
    gk_w = GLA_HEADS * GLA_DK
    gv_w = GLA_HEADS * GLA_DV
    conv_dim = SSD_WIDTH + 2 * SSD_GROUPS * SSD_STATE
    sizes = (da_w, da_w, da_w, gk_w, gk_w, gv_w, GLA_GATE_RANK, gv_w, SSD_WIDTH, conv_dim, SSD_HEADS,
             N_BRANCH * d)
    offs = np.concatenate([[0], np.cumsum(sizes)]).tolist()
    (o_daq, o_dak, o_dav, o_gq, o_gk, o_gv, o_glr, o_gr, o_sz, o_sx, o_sdt, o_mg, o_end) = offs

    ones = lambda n: jnp.ones((1, n), F32)
    qk_scale = jnp.concatenate([jnp.full((1, da_w), DA_HEAD_DIM ** -0.5, F32), ones(da_w)], axis=1)

    xf = x.reshape(t, d)
    xb = xf.astype(BF16)
    for l in range(depth):
        wl = w_in[l].astype(BF16)
        w_qk = wl[:, o_daq:o_dav]
        w_vt = wl[:, o_dav:o_gq].T
        w_gla = jnp.concatenate([wl[:, o_gq:o_glr], wl[:, o_gr:o_sz]], axis=1)
        w_ssd = wl[:, o_sz:o_sdt]
        w_small = jnp.concatenate([
            jnp.pad(wl[:, o_glr:o_gr], ((0, 0), (0, LANES - GLA_GATE_RANK))),
            jnp.pad(wl[:, o_sdt:o_mg], ((0, 0), (0, LANES - SSD_HEADS)))], axis=1)
        w_mg = jnp.stack([wl[:, o_mg + i * d:o_mg + (i + 1) * d] for i in range(N_BRANCH)])

        qk = _matmul(xb, w_qk, qk_scale, BF16, 512, 1024, "proj_attn_qk")
        vt = _matmul_nt(w_vt, xb, BF16, 512, "proj_attn_vt")
        pg = _matmul(xb, w_gla, ones(w_gla.shape[1]), F32, 512, 1024, "proj_gla")
        ps = _matmul(xb, w_ssd, ones(w_ssd.shape[1]), F32, 512, 1024, "proj_ssd")
        small = _matmul(xb, w_small, ones(2 * LANES), F32, 512, 2 * LANES, "proj_small")

        br_a = _diff_attention(qk, vt, da_lambda[l], da_norm_g[l], l, batch, seq)
        w2p = jnp.pad(gla_gate_w2[l], ((0, LANES - GLA_GATE_RANK), (0, 0))).astype(BF16)
        br_b = _gla(pg, small, w2p, gla_gate_b[l].reshape(1, gk_w), gla_norm_g[l], batch, seq)
        br_c = _ssd(ps, small, ssd_conv_w[l], ssd_conv_b[l], ssd_dt_bias[l], ssd_a_log[l], ssd_d[l],
                    ssd_norm_g[l], batch, seq)

        merged = _merge(xb, w_mg, br_a, br_b, br_c, w_branch[l].astype(BF16), 512, 512)
        xf, xb, xp = _outproj_ln(merged, w_out[l].astype(BF16), xf, ln1_g[l], ln1_b[l], alpha, 2 * TOKEN_TILE)

        eidx, wts, pos, cnt = _router(xf, router_w[l], router_bias[l], TOKEN_TILE)
        pstart, blk_e, blk_rows, nvalid, nblk = _moe_layout(cnt, t)
        dest = _slots(pstart, eidx, pos, min(t, 2048))
        xs = _dispatch(dest, xp, nblk * MOE_BLOCK, TOKEN_TILE)
        ys = _experts(xs, blk_e, blk_rows, nvalid, exp_w_gate, exp_w_up, exp_w_down, l)
        wgu = jnp.concatenate([sh_w_gate[l], sh_w_up[l]], axis=1).astype(BF16)
        xf, xb = _shared_ln(dest, xf, xb, wts.T, ys, wgu, sh_w_down[l].astype(BF16), ln2_g[l], ln2_b[l], alpha,
                            TOKEN_TILE)
    return xf.reshape(batch, seq, d)
```

```python
import functools
import math

import jax
import jax.numpy as jnp
import numpy as np
from jax import lax
from jax.experimental import pallas as pl
from jax.experimental.pallas import tpu as pltpu

F32 = jnp.float32
BF16 = jnp.bfloat16

DA_HEADS = 4
DA_HEAD_DIM = 128
DA_V_DIM = 2 * DA_HEAD_DIM
GLA_HEADS = 4
GLA_DK = 128
GLA_DV = 256
GLA_GATE_RANK = 16
GLA_TAU = 16.0
GLA_CHUNK = 64
SSD_HEADS = 16
SSD_HEAD_DIM = 64
SSD_STATE = 128
SSD_GROUPS = 4
SSD_CONV = 4
SSD_CHUNK = 128
SSD_WIDTH = SSD_HEADS * SSD_HEAD_DIM
N_BRANCH = 3
BRANCH_WIDTH = 1024
N_EXPERTS = 64
TOP_K = 8
N_EXPERT_GROUPS = 8
TOPK_GROUPS = 4
ROUTED_SCALE = 2.5
EPS = 1e-5

LANES = 128
SUBLANES = 8
VMEM_LIMIT = 52 * 1024 * 1024

ATTN_BLOCK = 512
ATTN_HEADS_PER_STEP = 4
GLA_BLOCK = 512
MOE_BLOCK = 512
TOKEN_TILE = 256


def _params(sem):
    return pltpu.CompilerParams(dimension_semantics=sem, vmem_limit_bytes=VMEM_LIMIT)


def _silu(x):
    return x * (1.0 / (1.0 + jnp.exp(-x)))


def _sigmoid(x):
    return 1.0 / (1.0 + jnp.exp(-x))


def _split3(x):
    h1 = x.astype(BF16)
    r1 = x - h1.astype(F32)
    h2 = r1.astype(BF16)
    r2 = r1 - h2.astype(F32)
    return h1, h2, r2.astype(BF16)


HI_HALF = -65536


def _pack_bf16_pair(lo, hi):
    lo_bits = lax.bitcast_convert_type(lo.astype(BF16).astype(F32), jnp.int32)
    hi_bits = lax.bitcast_convert_type(hi.astype(BF16).astype(F32), jnp.int32)
    return lax.shift_right_logical(lo_bits, 16) | (hi_bits & HI_HALF)


def _unpack_bf16_pair(w):
    lo = lax.bitcast_convert_type(lax.shift_left(w, 16), F32)
    hi = lax.bitcast_convert_type(w & HI_HALF, F32)
    return lo, hi


def _dot(a, b):
    return jnp.dot(a, b, preferred_element_type=F32)


def _dot_nt(a, b):
    return lax.dot_general(a, b, (((1,), (1,)), ((), ())), preferred_element_type=F32)


def _dot_tn(a, b):
    return lax.dot_general(a, b, (((0,), (0,)), ((), ())), preferred_element_type=F32)


def _mm_kernel(a_ref, w_ref, s_ref, o_ref):
    acc = _dot(a_ref[...], w_ref[...])
    o_ref[...] = (acc * s_ref[...]).astype(o_ref.dtype)


def _matmul(a, w, col_scale, out_dtype, tm, tn, name):
    m, k = a.shape
    n = w.shape[1]
    return pl.pallas_call(
        _mm_kernel,
        out_shape=jax.ShapeDtypeStruct((m, n), out_dtype),
        grid=(n // tn, m // tm),
        in_specs=[
            pl.BlockSpec((tm, k), lambda j, i: (i, 0)),
            pl.BlockSpec((k, tn), lambda j, i: (0, j)),
            pl.BlockSpec((1, tn), lambda j, i: (0, j)),
        ],
        out_specs=pl.BlockSpec((tm, tn), lambda j, i: (i, j)),
        compiler_params=_params(("parallel", "arbitrary")),
        name=name,
    )(a, w, col_scale)


def _mm_nt_kernel(wt_ref, a_ref, o_ref):
    o_ref[...] = _dot_nt(wt_ref[...], a_ref[...]).astype(o_ref.dtype)


def _matmul_nt(wt, a, out_dtype, tm, name):
    n, k = wt.shape
    m = a.shape[0]
    return pl.pallas_call(
        _mm_nt_kernel,
        out_shape=jax.ShapeDtypeStruct((n, m), out_dtype),
        grid=(m // tm,),
        in_specs=[
            pl.BlockSpec((n, k), lambda i: (0, 0)),
            pl.BlockSpec((tm, k), lambda i: (i, 0)),
        ],
        out_specs=pl.BlockSpec((n, tm), lambda i: (0, i)),
        compiler_params=_params(("parallel",)),
        name=name,
    )(wt, a)


def _attn_kernel(qi_tab, ki_tab, q_ref, k_ref, vt_ref, bt_ref, slope_ref, lam_ref, g_ref, o_ref,
                 m_ref, l_ref, acc_ref, *, blk, lam_init):
    p = pl.program_id(2)
    qi = qi_tab[p]
    ki = ki_tab[p]

    @pl.when(ki == 0)
    def _init():
        m_ref[...] = jnp.full(m_ref.shape, -jnp.inf, F32)
        l_ref[...] = jnp.zeros(l_ref.shape, F32)
        acc_ref[...] = jnp.zeros(acc_ref.shape, F32)

    rel = jnp.full((1, blk), (ki - qi) * blk, jnp.int32).astype(F32)

    def step(masked):
        if masked:
            kc = lax.broadcasted_iota(jnp.int32, (blk, blk), 0)
            qr = lax.broadcasted_iota(jnp.int32, (blk, blk), 1)
            keep = kc <= qr
        for hd in range(ATTN_HEADS_PER_STEP):
            c0 = slope_ref[hd] * rel
            bt = bt_ref[hd]
            vt = vt_ref[hd * DA_V_DIM:(hd + 1) * DA_V_DIM, :]
            for mp in range(2):
                sl = slice(hd * DA_V_DIM + mp * DA_HEAD_DIM, hd * DA_V_DIM + (mp + 1) * DA_HEAD_DIM)
                si = 2 * hd + mp
                s = _dot_nt(k_ref[:, sl], q_ref[:, sl]) + bt
                if masked:
                    s = jnp.where(keep, s, -jnp.inf)
                m_old = m_ref[si]
                m_new = jnp.maximum(m_old, jnp.max(s, axis=0, keepdims=True) + c0)
                alpha = jnp.exp(m_old - m_new)
                pr = jnp.exp(s - (m_new - c0))
                l_ref[si] = alpha * l_ref[si] + jnp.sum(pr, axis=0, keepdims=True)
                acc_ref[si] = alpha * acc_ref[si] + _dot(vt, pr.astype(BF16))
                m_ref[si] = m_new

    @pl.when(ki < qi)
    def _off_diag():
        step(False)

    @pl.when(ki == qi)
    def _diag():
        step(True)
        lp = lam_ref[...]
        lam = (jnp.exp(jnp.sum(lp[0:1] * lp[1:2], axis=1, keepdims=True))
               - jnp.exp(jnp.sum(lp[2:3] * lp[3:4], axis=1, keepdims=True)) + lam_init)
        for hd in range(ATTN_HEADS_PER_STEP):
            ot = acc_ref[2 * hd] / l_ref[2 * hd] - lam * (acc_ref[2 * hd + 1] / l_ref[2 * hd + 1])
            o = ot.T
            o = o * lax.rsqrt(jnp.mean(o * o, axis=1, keepdims=True) + EPS)
            o_ref[:, hd * DA_V_DIM:(hd + 1) * DA_V_DIM] = (o * g_ref[...] * (1.0 - lam_init)).astype(o_ref.dtype)


def _diff_attention(qk, vt, lam_params, norm_g, layer_idx, batch, seq):
    blk = min(ATTN_BLOCK, seq)
    nq = seq // blk
    pairs = [(i, j) for i in range(nq) for j in range(i + 1)]
    qi_tab = jnp.asarray([p[0] for p in pairs], jnp.int32)
    ki_tab = jnp.asarray([p[1] for p in pairs], jnp.int32)
    lam_init = 0.8 - 0.6 * math.exp(-0.3 * layer_idx)
    h = DA_HEADS
    slopes = np.asarray([2.0 ** (-8.0 * (i + 1) / h) for i in range(h)], np.float32)
    key_off = np.arange(blk, dtype=np.float32)
    bias_t = jnp.asarray(np.broadcast_to((slopes[:, None] * key_off[None, :])[:, :, None], (h, blk, blk)))
    slopes = jnp.asarray(np.broadcast_to(slopes[:, None, None], (h, 1, blk)))
    t = batch * seq
    hs = ATTN_HEADS_PER_STEP
    hg = h // hs
    kern = functools.partial(_attn_kernel, blk=blk, lam_init=lam_init)
    grid_spec = pltpu.PrefetchScalarGridSpec(
        num_scalar_prefetch=2,
        grid=(batch, hg, len(pairs)),
        in_specs=[
            pl.BlockSpec((blk, hs * DA_V_DIM), lambda b, hh, p, qt, kt: (b * nq + qt[p], hh)),
            pl.BlockSpec((blk, hs * DA_V_DIM), lambda b, hh, p, qt, kt: (b * nq + kt[p], hg + hh)),
            pl.BlockSpec((hs * DA_V_DIM, blk), lambda b, hh, p, qt, kt: (hh, b * nq + kt[p])),
            pl.BlockSpec((hs, blk, blk), lambda b, hh, p, qt, kt: (hh, 0, 0)),
            pl.BlockSpec((hs, 1, blk), lambda b, hh, p, qt, kt: (hh, 0, 0)),
            pl.BlockSpec((4, DA_HEAD_DIM), lambda b, hh, p, qt, kt: (0, 0)),
            pl.BlockSpec((1, DA_V_DIM), lambda b, hh, p, qt, kt: (0, 0)),
        ],
        out_specs=pl.BlockSpec((blk, hs * DA_V_DIM), lambda b, hh, p, qt, kt: (b * nq + qt[p], hh)),
        scratch_shapes=[
            pltpu.VMEM((2 * hs, 1, blk), F32),
            pltpu.VMEM((2 * hs, 1, blk), F32),
            pltpu.VMEM((2 * hs, DA_V_DIM, blk), F32),
        ],
    )
    return pl.pallas_call(
        kern,
        out_shape=jax.ShapeDtypeStruct((t, h * DA_V_DIM), BF16),
        grid_spec=grid_spec,
        compiler_params=_params(("parallel", "parallel", "arbitrary")),
        name="diff_attention",
    )(qi_tab, ki_tab, qk, qk, vt, bias_t, slopes, lam_params, norm_g.reshape(1, DA_V_DIM))


def _gla_kernel(pg_ref, sm_ref, w2_ref, b2_ref, g_ref, o_ref, st_ref, *, blk):
    n = pl.program_id(1)
    hk = GLA_HEADS * GLA_DK
    hv = GLA_HEADS * GLA_DV
    c_len = GLA_CHUNK

    @pl.when(n == 0)
    def _init():
        st_ref[...] = jnp.zeros(st_ref.shape, F32)

    lr = sm_ref[:, 0:LANES].astype(BF16)
    gl = _dot(lr, w2_ref[...]) + b2_ref[...]
    gk = (jnp.minimum(gl, 0.0) - jnp.log(1.0 + jnp.exp(-jnp.abs(gl)))) * (1.0 / GLA_TAU)
    r = lax.broadcasted_iota(jnp.int32, (blk, blk), 0)
    c = lax.broadcasted_iota(jnp.int32, (blk, blk), 1)
    tri = jnp.where((c <= r) & ((r // c_len) == (c // c_len)), 1.0, 0.0).astype(BF16)
    g1, g2, g3 = _split3(gk)
    bcum = _dot(tri, g1) + _dot(tri, g2) + _dot(tri, g3)

    rr = lax.broadcasted_iota(jnp.int32, (c_len, c_len), 0)
    cc = lax.broadcasted_iota(jnp.int32, (c_len, c_len), 1)
    causal = cc <= rr
    scale = GLA_DK ** -0.5
    for ci in range(blk // c_len):
        rows = slice(ci * c_len, (ci + 1) * c_len)
        for h in range(GLA_HEADS):
            kc = slice(h * GLA_DK, (h + 1) * GLA_DK)
            vc = slice(h * GLA_DV, (h + 1) * GLA_DV)
            b = bcum[rows, kc]
            b_last = b[c_len - 1:c_len, :]
            q = pg_ref[rows, kc]
            k = pg_ref[rows, hk + h * GLA_DK: hk + (h + 1) * GLA_DK]
            v = pg_ref[rows, 2 * hk + h * GLA_DV: 2 * hk + (h + 1) * GLA_DV].astype(BF16)
            rg = pg_ref[rows, 2 * hk + hv + h * GLA_DV: 2 * hk + hv + (h + 1) * GLA_DV]
            q_e = (q * scale * jnp.exp(b)).astype(BF16)
            k_e = (k * jnp.exp(-b)).astype(BF16)
            k_d = (k * jnp.exp(b_last - b)).astype(BF16)
            att = jnp.where(causal, _dot_nt(q_e, k_e), 0.0)
            st = st_ref[h]
            o = _dot(att.astype(BF16), v) + _dot_nt(q_e, st.astype(BF16))
            st_ref[h] = st * jnp.exp(b_last) + _dot_tn(v, k_d)
            o = o * lax.rsqrt(jnp.mean(o * o, axis=1, keepdims=True) + EPS) * g_ref[...]
            o_ref[rows, vc] = (o * _silu(rg)).astype(o_ref.dtype)


def _gla(pg, small, w2p, b2, norm_g, batch, seq):
    blk = min(GLA_BLOCK, seq)
    nb = seq // blk
    t = batch * seq
    hk = GLA_HEADS * GLA_DK
    hv = GLA_HEADS * GLA_DV
    width = pg.shape[1]
    return pl.pallas_call(
        functools.partial(_gla_kernel, blk=blk),
        out_shape=jax.ShapeDtypeStruct((t, hv), BF16),
        grid=(batch, nb),
        in_specs=[
            pl.BlockSpec((blk, width), lambda b, n: (b * nb + n, 0)),
            pl.BlockSpec((blk, small.shape[1]), lambda b, n: (b * nb + n, 0)),
            pl.BlockSpec((LANES, hk), lambda b, n: (0, 0)),
            pl.BlockSpec((1, hk), lambda b, n: (0, 0)),
            pl.BlockSpec((1, GLA_DV), lambda b, n: (0, 0)),
        ],
        out_specs=pl.BlockSpec((blk, hv), lambda b, n: (b * nb + n, 0)),
        scratch_shapes=[pltpu.VMEM((GLA_HEADS, GLA_DV, GLA_DK), F32)],
        compiler_params=_params(("parallel", "arbitrary")),
        name="gla",
    )(pg, small, w2p, b2, norm_g.reshape(1, GLA_DV))


def _ssd_kernel(ps_ref, sm_ref, cw_ref, cb_ref, dtb_ref, alog_ref, dsk_ref, g_ref, o_ref,
                tail_ref, st_ref):
    n = pl.program_id(1)
    q_len = SSD_CHUNK
    w = SSD_WIDTH
    gs = SSD_GROUPS * SSD_STATE
    heads_per_group = SSD_HEADS // SSD_GROUPS
    gw = heads_per_group * SSD_HEAD_DIM

    @pl.when(n == 0)
    def _init():
        tail_ref[...] = jnp.zeros(tail_ref.shape, F32)
        st_ref[...] = jnp.zeros(st_ref.shape, F32)

    cur = ps_ref[:, w:w + w + 2 * gs]
    ext = jnp.concatenate([tail_ref[...], cur], axis=0)
    acc = cb_ref[...] + cw_ref[0:1, :] * ext[SUBLANES - 3:SUBLANES - 3 + q_len]
    for i in range(1, SSD_CONV):
        off = SUBLANES - (SSD_CONV - 1) + i
        acc = acc + cw_ref[i:i + 1, :] * ext[off:off + q_len]
    tail_ref[...] = cur[q_len - SUBLANES:q_len]
    xbc = _silu(acc)
    xs = xbc[:, 0:w]
    bm = xbc[:, w:w + gs]
    cm = xbc[:, w + gs:w + 2 * gs]

    dtr = sm_ref[:, LANES:2 * LANES] + dtb_ref[...]
    dtv = jnp.maximum(dtr, 0.0) + jnp.log(1.0 + jnp.exp(-jnp.abs(dtr)))
    da = dtv * (-jnp.exp(alog_ref[...]))
    r = lax.broadcasted_iota(jnp.int32, (q_len, q_len), 0)
    c = lax.broadcasted_iota(jnp.int32, (q_len, q_len), 1)
    causal = c <= r
    tri = jnp.where(causal, 1.0, 0.0).astype(BF16)
    d1, d2, d3 = _split3(da)
    a_cs = _dot(tri, d1) + _dot(tri, d2) + _dot(tri, d3)
    a_cs_t = a_cs.T
    a_last = a_cs[q_len - 1:q_len, :]
    e_last = jnp.exp(a_last)

    for g in range(SSD_GROUPS):
        bm_g = bm[:, g * SSD_STATE:(g + 1) * SSD_STATE]
        cm_g = cm[:, g * SSD_STATE:(g + 1) * SSD_STATE]
        bm_b = bm_g.astype(BF16)
        cm_b = cm_g.astype(BF16)
        cb = _dot_nt(cm_b, bm_b)
        y_diag = []
        xdd = []
        e_col = []
        e_row = []
        for rh in range(heads_per_group):
            h = g * heads_per_group + rh
            col = a_cs[:, h:h + 1]
            row = a_cs_t[h:h + 1, :]
            lm = jnp.exp(jnp.where(causal, col - row, -jnp.inf))
            xdt = xs[:, h * SSD_HEAD_DIM:(h + 1) * SSD_HEAD_DIM] * dtv[:, h:h + 1]
            y_diag.append(_dot((cb * lm).astype(BF16), xdt.astype(BF16)))
            xdd.append(xdt * jnp.exp(a_last[:, h:h + 1] - col))
            e_col.append(jnp.broadcast_to(jnp.exp(col), (q_len, SSD_HEAD_DIM)))
            e_row.append(jnp.broadcast_to(e_last[:, h:h + 1], (1, SSD_HEAD_DIM)))
        y_diag = jnp.concatenate(y_diag, axis=1)
        xdd = jnp.concatenate(xdd, axis=1)
        e_col = jnp.concatenate(e_col, axis=1)
        e_row = jnp.concatenate(e_row, axis=1)
        s_prev = st_ref[g]
        y_off = _dot(cm_b, s_prev.astype(BF16)) * e_col
        st_ref[g] = s_prev * e_row + _dot_tn(bm_b, xdd.astype(BF16))
        lanes = slice(g * gw, (g + 1) * gw)
        y = y_diag + y_off + xs[:, lanes] * dsk_ref[:, lanes]
        y = y * _silu(ps_ref[:, lanes])
        y = y * lax.rsqrt(jnp.mean(y * y, axis=1, keepdims=True) + EPS) * g_ref[:, lanes]
        o_ref[:, lanes] = y.astype(o_ref.dtype)


def _ssd(ps, small, conv_w, conv_b, dt_bias, a_log, d_skip, norm_g, batch, seq):
    q_len = SSD_CHUNK
    nb = seq // q_len
    t = batch * seq
    conv_dim = conv_w.shape[1]
    pad = LANES - SSD_HEADS
    dtb = jnp.pad(dt_bias, (0, pad)).reshape(1, LANES)
    alog = jnp.pad(a_log, (0, pad)).reshape(1, LANES)
    dsk = jnp.repeat(d_skip, SSD_HEAD_DIM).reshape(1, SSD_WIDTH)
    const = lambda b, n: (0, 0)
    return pl.pallas_call(
        _ssd_kernel,
        out_shape=jax.ShapeDtypeStruct((t, SSD_WIDTH), BF16),
        grid=(batch, nb),
        in_specs=[
            pl.BlockSpec((q_len, ps.shape[1]), lambda b, n: (b * nb + n, 0)),
            pl.BlockSpec((q_len, small.shape[1]), lambda b, n: (b * nb + n, 0)),
            pl.BlockSpec((SSD_CONV, conv_dim), const),
            pl.BlockSpec((1, conv_dim), const),
            pl.BlockSpec((1, LANES), const),
            pl.BlockSpec((1, LANES), const),
            pl.BlockSpec((1, SSD_WIDTH), const),
            pl.BlockSpec((1, SSD_WIDTH), const),
        ],
        out_specs=pl.BlockSpec((q_len, SSD_WIDTH), lambda b, n: (b * nb + n, 0)),
        scratch_shapes=[
            pltpu.VMEM((SUBLANES, conv_dim), F32),
            pltpu.VMEM((SSD_GROUPS, SSD_STATE, SSD_WIDTH // SSD_GROUPS), F32),
        ],
        compiler_params=_params(("parallel", "arbitrary")),
        name="ssd",
    )(ps, small, conv_w, conv_b.reshape(1, conv_dim), dtb, alog, dsk, norm_g.reshape(1, SSD_WIDTH))


def _merge_kernel(x_ref, wga_ref, wgb_ref, wgc_ref, ba_ref, bb_ref, bc_ref, wb_ref, o_ref):
    x = x_ref[...]
    acc = None
    for i, (wg, br) in enumerate(((wga_ref, ba_ref), (wgb_ref, bb_ref), (wgc_ref, bc_ref))):
        gate = _sigmoid(_dot(x, wg[...]))
        term = gate * _dot(br[...], wb_ref[i])
        acc = term if acc is None else acc + term
    o_ref[...] = acc.astype(o_ref.dtype)


def _merge(xb, wgate, br_a, br_b, br_c, wbr, tm, tn):
    t, d = xb.shape
    bw = br_a.shape[1]
    nj = d // tn
    gate_spec = lambda i_br: pl.BlockSpec((d, tn), lambda i, j: (0, i_br * nj + j))
    return pl.pallas_call(
        _merge_kernel,
        out_shape=jax.ShapeDtypeStruct((t, d), BF16),
        grid=(t // tm, nj),
        in_specs=[
            pl.BlockSpec((tm, d), lambda i, j: (i, 0)),
            gate_spec(0), gate_spec(1), gate_spec(2),
            pl.BlockSpec((tm, bw), lambda i, j: (i, 0)),
            pl.BlockSpec((tm, bw), lambda i, j: (i, 0)),
            pl.BlockSpec((tm, bw), lambda i, j: (i, 0)),
            pl.BlockSpec((N_BRANCH, bw, tn), lambda i, j: (0, 0, j)),
        ],
        out_specs=pl.BlockSpec((tm, tn), lambda i, j: (i, j)),
        compiler_params=_params(("parallel", "arbitrary")),
        name="gated_merge",
    )(xb, wgate, wgate, wgate, br_a, br_b, br_c, wbr)


def _layernorm(v, g, b):
    mu = jnp.mean(v, axis=1, keepdims=True)
    d = v - mu
    var = jnp.mean(d * d, axis=1, keepdims=True)
    return d * lax.rsqrt(var + EPS) * g + b


def _outproj_ln_kernel(m_ref, w_ref, x_ref, g_ref, b_ref, o_ref, ob_ref, op_ref, *, alpha):
    mix = _dot(m_ref[...], w_ref[...])
    y = _layernorm(alpha * x_ref[...] + mix, g_ref[...], b_ref[...])
    half = y.shape[1] // 2
    o_ref[...] = y
    ob_ref[...] = y.astype(BF16)
    op_ref[...] = _pack_bf16_pair(y[:, 0:half], y[:, half:])


def _outproj_ln(merged, w_out, x, g, b, alpha, tm):
    t, d = x.shape
    const = lambda i: (0, 0)
    tok = lambda i: (i, 0)
    return pl.pallas_call(
        functools.partial(_outproj_ln_kernel, alpha=alpha),
        out_shape=(jax.ShapeDtypeStruct((t, d), F32), jax.ShapeDtypeStruct((t, d), BF16),
                   jax.ShapeDtypeStruct((t, d // 2), jnp.int32)),
        grid=(t // tm,),
        in_specs=[
            pl.BlockSpec((tm, d), tok),
            pl.BlockSpec((d, d), const, pipeline_mode=pl.Buffered(1)),
            pl.BlockSpec((tm, d), tok),
            pl.BlockSpec((1, d), const),
            pl.BlockSpec((1, d), const),
        ],
        out_specs=(pl.BlockSpec((tm, d), tok), pl.BlockSpec((tm, d), tok), pl.BlockSpec((tm, d // 2), tok)),
        compiler_params=_params(("parallel",)),
        name="outproj_layernorm",
    )(merged, w_out, x, g.reshape(1, d), b.reshape(1, d))


def _router_kernel(x_ref, w_ref, bias_ref, eidx_ref, wt_ref, pos_ref, cnt_ref, carry_ref, *, tm):
    i = pl.program_id(0)
    ne = N_EXPERTS
    ng = N_EXPERT_GROUPS
    per = ne // ng

    @pl.when(i == 0)
    def _init():
        carry_ref[...] = jnp.zeros(carry_ref.shape, F32)

    x = x_ref[...]
    xh = x.astype(BF16)
    xl = (x - xh.astype(F32)).astype(BF16)
    w = w_ref[...]
    wh = w.astype(BF16)
    wl = (w - wh.astype(F32)).astype(BF16)
    logits = _dot_nt(wh, xh) + _dot_nt(wh, xl) + _dot_nt(wl, xh)
    scores = _sigmoid(logits)
    sel = scores + bias_ref[...]

    grp = sel.reshape(ng, per, tm)
    io_p = lax.broadcasted_iota(jnp.int32, (ng, per, tm), 1)
    m1 = jnp.max(grp, axis=1, keepdims=True)
    i1 = jnp.min(jnp.where(grp == m1, io_p, per), axis=1, keepdims=True)
    m2 = jnp.max(jnp.where(io_p == i1, -jnp.inf, grp), axis=1, keepdims=True)
    gscore = (m1 + m2).reshape(ng, tm)

    io_g = lax.broadcasted_iota(jnp.int32, (ng, tm), 0)
    gsel = jnp.zeros((ng, tm), F32)
    gwork = gscore
    for _ in range(TOPK_GROUPS):
        gm = jnp.max(gwork, axis=0, keepdims=True)
        gi = jnp.min(jnp.where(gwork == gm, io_g, ng), axis=0, keepdims=True)
        hit = io_g == gi
        gsel = jnp.where(hit, 1.0, gsel)
        gwork = jnp.where(hit, -jnp.inf, gwork)
    emask = jnp.broadcast_to(gsel.reshape(ng, 1, tm), (ng, per, tm)).reshape(ne, tm) > 0.5

    io_e = lax.broadcasted_iota(jnp.int32, (ne, tm), 0)
    work = jnp.where(emask, sel, -jnp.inf)
    member = jnp.zeros((ne, tm), F32)
    idx_rows = []
    w_rows = []
    for _ in range(TOP_K):
        mx = jnp.max(work, axis=0, keepdims=True)
        ei = jnp.min(jnp.where(work == mx, io_e, ne), axis=0, keepdims=True)
        hit = io_e == ei
        idx_rows.append(ei)
        w_rows.append(jnp.sum(jnp.where(hit, scores, 0.0), axis=0, keepdims=True))
        member = jnp.where(hit, 1.0, member)
        work = jnp.where(hit, -jnp.inf, work)
    wsum = w_rows[0]
    for wr in w_rows[1:]:
        wsum = wsum + wr
    inv = ROUTED_SCALE / wsum

    r = lax.broadcasted_iota(jnp.int32, (tm, tm), 0)
    c = lax.broadcasted_iota(jnp.int32, (tm, tm), 1)
    upper = jnp.where(r < c, 1.0, 0.0).astype(BF16)
    prefix = _dot(member.astype(BF16), upper) + carry_ref[...]
    for j in range(TOP_K):
        hit = io_e == idx_rows[j]
        eidx_ref[j:j + 1, :] = idx_rows[j]
        wt_ref[j:j + 1, :] = w_rows[j] * inv
        pos_ref[j:j + 1, :] = jnp.sum(jnp.where(hit, prefix, 0.0), axis=0, keepdims=True).astype(jnp.int32)
    carry = carry_ref[...] + jnp.sum(member, axis=1, keepdims=True)
    carry_ref[...] = carry
    cnt_ref[...] = jnp.broadcast_to(carry, cnt_ref.shape)


def _router(x, router_w, router_bias, tm):
    t, d = x.shape
    ne = N_EXPERTS
    return pl.pallas_call(
        functools.partial(_router_kernel, tm=tm),
        out_shape=(
            jax.ShapeDtypeStruct((TOP_K, t), jnp.int32),
            jax.ShapeDtypeStruct((TOP_K, t), F32),
            jax.ShapeDtypeStruct((TOP_K, t), jnp.int32),
            jax.ShapeDtypeStruct((ne, LANES), F32),
        ),
        grid=(t // tm,),
        in_specs=[
            pl.BlockSpec((tm, d), lambda i: (i, 0)),
            pl.BlockSpec((ne, d), lambda i: (0, 0)),
            pl.BlockSpec((ne, 1), lambda i: (0, 0)),
        ],
        out_specs=(
            pl.BlockSpec((TOP_K, tm), lambda i: (0, i)),
            pl.BlockSpec((TOP_K, tm), lambda i: (0, i)),
            pl.BlockSpec((TOP_K, tm), lambda i: (0, i)),
            pl.BlockSpec((ne, LANES), lambda i: (0, 0)),
        ),
        scratch_shapes=[pltpu.VMEM((ne, 1), F32)],
        compiler_params=_params(("arbitrary",)),
        name="router",
    )(x, router_w.T, router_bias.reshape(ne, 1))


def _slots_kernel(pstart, eidx_ref, pos_ref, dest_ref):
    eidx = eidx_ref[...]
    base = jnp.zeros(eidx.shape, jnp.int32)
    for e in range(N_EXPERTS):
        base = jnp.where(eidx == e, pstart[e], base)
    dest_ref[...] = base + pos_ref[...]


def _slots(pstart, eidx, pos, tn):
    k, t = eidx.shape
    blk = lambda i, ps: (0, i)
    grid_spec = pltpu.PrefetchScalarGridSpec(
        num_scalar_prefetch=1,
        grid=(t // tn,),
        in_specs=[pl.BlockSpec((k, tn), blk), pl.BlockSpec((k, tn), blk)],
        out_specs=pl.BlockSpec((k, tn), blk),
    )
    return pl.pallas_call(
        _slots_kernel,
        out_shape=jax.ShapeDtypeStruct((k, t), jnp.int32),
        grid_spec=grid_spec,
        compiler_params=_params(("parallel",)),
        name="moe_slots",
    )(pstart, eidx, pos)


def _dispatch_kernel(dest_ref, x_ref, xs_hbm, sem):
    tm = x_ref.shape[0]

    def issue(r, carry):
        for j in range(TOP_K):
            pltpu.make_async_copy(x_ref.at[pl.ds(r, 1), :], xs_hbm.at[pl.ds(dest_ref[j, r], 1), :],
                                  sem).start(priority=j % 2)
        return carry

    lax.fori_loop(0, tm, issue, 0)
    for j in range(TOP_K):
        pltpu.make_async_copy(x_ref, xs_hbm.at[pl.ds(0, tm), :], sem).wait()


def _dispatch(dest, x, rows, tm):
    t, d = x.shape
    return pl.pallas_call(
        _dispatch_kernel,
        out_shape=jax.ShapeDtypeStruct((rows, d), x.dtype),
        grid=(t // tm,),
        in_specs=[
            pl.BlockSpec((TOP_K, tm), lambda i: (0, i), memory_space=pltpu.SMEM),
            pl.BlockSpec((tm, d), lambda i: (i, 0)),
        ],
        out_specs=pl.BlockSpec(memory_space=pl.ANY),
        scratch_shapes=[pltpu.SemaphoreType.DMA(())],
        compiler_params=_params(("arbitrary",)),
        name="moe_dispatch",
    )(dest, x)


def _experts_kernel(blk_e, blk_rows, nvalid, blk_first, blk_ord, seq_e, npresent,
                    xs_ref, wg_hbm, wu_hbm, wd_hbm, o_ref, wg_f, wu_f, wd_f, wgu_s, wd_s, sem, *, layer):
    i = pl.program_id(0)
    f = wg_f.shape[-1]
    half = xs_ref.shape[1]

    def weight_copies(e, slot):
        return (pltpu.make_async_copy(wg_hbm.at[layer, e], wg_f.at[slot], sem.at[slot]),
                pltpu.make_async_copy(wu_hbm.at[layer, e], wu_f.at[slot], sem.at[slot]),
                pltpu.make_async_copy(wd_hbm.at[layer, e], wd_f.at[slot], sem.at[slot]))

    @pl.when(i == 0)
    def _prime():
        for cp in weight_copies(seq_e[0], 0):
            cp.start()

    @pl.when(i < nvalid[0])
    def _():
        @pl.when(blk_first[i] == 1)
        def _next_expert():
            k = blk_ord[i]
            slot = lax.rem(k, 2)

            @pl.when(k + 1 < npresent[0])
            def _prefetch():
                for cp in weight_copies(seq_e[k + 1], 1 - slot):
                    cp.start()

            for cp in weight_copies(blk_e[i], slot):
                cp.wait()
            wgu_s[:, 0:f] = wg_f[slot].astype(BF16)
            wgu_s[:, f:2 * f] = wu_f[slot].astype(BF16)
            wd_s[...] = wd_f[slot].astype(BF16)

        row = lax.broadcasted_iota(jnp.int32, xs_ref.shape, 0)
        x_lo, x_hi = _unpack_bf16_pair(jnp.where(row < blk_rows[i], xs_ref[...], 0))
        gu = _dot(x_lo.astype(BF16), wgu_s[0:half, :]) + _dot(x_hi.astype(BF16), wgu_s[half:2 * half, :])
        hid = _silu(gu[:, 0:f]) * gu[:, f:2 * f]
        out = _dot(hid.astype(BF16), wd_s[...])
        o_ref[...] = _pack_bf16_pair(out[:, 0:half], out[:, half:])

    @pl.when(i >= nvalid[0])
    def _():
        o_ref[...] = jnp.zeros(o_ref.shape, o_ref.dtype)


def _experts(xs, layout, w_gate, w_up, w_down, layer):
    p, half = xs.shape
    d = 2 * half
    f = w_gate.shape[-1]
    bm = MOE_BLOCK
    row = lambda i, be, br, nv, *_: (jnp.minimum(i, nv[0] - 1), 0)
    grid_spec = pltpu.PrefetchScalarGridSpec(
        num_scalar_prefetch=len(layout),
        grid=(p // bm,),
        in_specs=[
            pl.BlockSpec((bm, half), row),
            pl.BlockSpec(memory_space=pl.ANY),
            pl.BlockSpec(memory_space=pl.ANY),
            pl.BlockSpec(memory_space=pl.ANY),
        ],
        out_specs=pl.BlockSpec((bm, half), lambda i, *_: (i, 0)),
        scratch_shapes=[
            pltpu.VMEM((2, d, f), F32), pltpu.VMEM((2, d, f), F32), pltpu.VMEM((2, f, d), F32),
            pltpu.VMEM((d, 2 * f), BF16), pltpu.VMEM((f, d), BF16),
            pltpu.SemaphoreType.DMA((2,)),
        ],
    )
    return pl.pallas_call(
        functools.partial(_experts_kernel, layer=layer),
        out_shape=jax.ShapeDtypeStruct((p, half), jnp.int32),
        grid_spec=grid_spec,
        compiler_params=_params(("arbitrary",)),
        name="routed_experts",
    )(*layout, xs, w_gate, w_up, w_down)


def _shared_ln_kernel(dest_ref, x_ref, xb_ref, wt_ref, ys_hbm, wgu_ref, wd_ref, g_ref, b_ref, o_ref, ob_ref,
                      buf, sem, *, alpha):
    f = wd_ref.shape[0]
    tm = x_ref.shape[0]

    def issue(r, carry):
        for j in range(TOP_K):
            pltpu.make_async_copy(ys_hbm.at[pl.ds(dest_ref[j, r], 1), :], buf.at[j, pl.ds(r, 1), :],
                                  sem).start(priority=j % 2)
        return carry

    lax.fori_loop(0, tm, issue, 0)
    gu = _dot(xb_ref[...], wgu_ref[...])
    hid = _silu(gu[:, 0:f]) * gu[:, f:2 * f]
    ffn = _dot(hid.astype(BF16), wd_ref[...])
    for j in range(TOP_K):
        pltpu.make_async_copy(ys_hbm.at[pl.ds(0, tm), :], buf.at[j], sem).wait()
    half = buf.shape[2]
    r_lo = jnp.zeros((tm, half), F32)
    r_hi = jnp.zeros((tm, half), F32)
    for j in range(TOP_K):
        y_lo, y_hi = _unpack_bf16_pair(buf[j])
        wj = wt_ref[:, j:j + 1]
        r_lo = r_lo + wj * y_lo
        r_hi = r_hi + wj * y_hi
    ffn = ffn + jnp.concatenate([r_lo, r_hi], axis=1)
    y = _layernorm(alpha * x_ref[...] + ffn, g_ref[...], b_ref[...])
    o_ref[...] = y
    ob_ref[...] = y.astype(BF16)


def _shared_ln(dest, x, xb, wts_t, ys, wgu, wd, g, b, alpha, tm):
    t, d = x.shape
    f = wd.shape[0]
    const = lambda i: (0, 0)
    tok = lambda i: (i, 0)
    return pl.pallas_call(
        functools.partial(_shared_ln_kernel, alpha=alpha),
        out_shape=(jax.ShapeDtypeStruct((t, d), F32), jax.ShapeDtypeStruct((t, d), BF16)),
        grid=(t // tm,),
        in_specs=[
            pl.BlockSpec((TOP_K, tm), lambda i: (0, i), memory_space=pltpu.SMEM),
            pl.BlockSpec((tm, d), tok),
            pl.BlockSpec((tm, d), tok),
            pl.BlockSpec((tm, TOP_K), tok),
            pl.BlockSpec(memory_space=pl.ANY),
            pl.BlockSpec((d, 2 * f), const),
            pl.BlockSpec((f, d), const),
            pl.BlockSpec((1, d), const),
            pl.BlockSpec((1, d), const),
        ],
        out_specs=(pl.BlockSpec((tm, d), tok), pl.BlockSpec((tm, d), tok)),
        scratch_shapes=[pltpu.VMEM((TOP_K, tm, d // 2), jnp.int32), pltpu.SemaphoreType.DMA(())],
        compiler_params=_params(("arbitrary",)),
        name="shared_expert_combine_layernorm",
    )(dest, x, xb, wts_t, ys, wgu, wd, g.reshape(1, d), b.reshape(1, d))


def _moe_layout(cnt, t):
    counts = cnt[:, 0].astype(jnp.int32)
    padded = (counts + MOE_BLOCK - 1) // MOE_BLOCK * MOE_BLOCK
    pend = jnp.cumsum(padded)
    pstart = pend - padded
    nblk = (t * TOP_K + N_EXPERTS * MOE_BLOCK) // MOE_BLOCK
    nvalid = pend[-1] // MOE_BLOCK
    experts = jnp.arange(N_EXPERTS, dtype=jnp.int32)
    blocks = jnp.arange(nblk, dtype=jnp.int32)
    starts = jnp.minimum(blocks, nvalid - 1) * MOE_BLOCK
    blk_e = jnp.sum((pend[None, :] <= starts[:, None]).astype(jnp.int32), axis=1)
    onehot = blk_e[:, None] == experts[None, :]
    used_end = jnp.sum(jnp.where(onehot, (pstart + counts)[None, :], 0), axis=1)
    blk_rows = jnp.clip(used_end - starts, 0, MOE_BLOCK)
    present = counts > 0
    ord_e = jnp.cumsum(present.astype(jnp.int32)) - 1
    blk_ord = jnp.sum(jnp.where(onehot, ord_e[None, :], 0), axis=1)
    seq_e = jnp.sum(jnp.where(present[None, :] & (ord_e[None, :] == experts[:, None]), experts[None, :], 0), axis=1)
    prev_e = jnp.concatenate([jnp.full((1,), -1, jnp.int32), blk_e[:-1]])
    blk_first = ((blk_e != prev_e) & (blocks < nvalid)).astype(jnp.int32)
    i32 = lambda v: v.astype(jnp.int32)
    layout = (i32(blk_e), i32(blk_rows), i32(nvalid).reshape(1), blk_first, i32(blk_ord), i32(seq_e),
              i32(jnp.sum(present)).reshape(1))
    return i32(pstart), layout, nblk * MOE_BLOCK


def kernel(x, w_in, da_lambda, da_norm_g, gla_gate_w2, gla_gate_b, gla_norm_g, ssd_conv_w, ssd_conv_b,
           ssd_dt_bias, ssd_a_log, ssd_d, ssd_norm_g, w_branch, w_out, ln1_g, ln1_b, router_w, router_bias,
           exp_w_gate, exp_w_up, exp_w_down, sh_w_gate, sh_w_up, sh_w_down, ln2_g, ln2_b):
    batch, seq, d = x.shape
    depth = w_in.shape[0]
    t = batch * seq
    alpha = (2 * depth) ** 0.25

    da_w = DA_HEADS * DA_V_DIM
    gk_w = GLA_HEADS * GLA_DK
    gv_w = GLA_HEADS * GLA_DV
    conv_dim = SSD_WIDTH + 2 * SSD_GROUPS * SSD_STATE
    sizes = (da_w, da_w, da_w, gk_w, gk_w, gv_w, GLA_GATE_RANK, gv_w, SSD_WIDTH, conv_dim, SSD_HEADS,
             N_BRANCH * d)
    offs = np.concatenate([[0], np.cumsum(sizes)]).tolist()
    (o_daq, o_dak, o_dav, o_gq, o_gk, o_gv, o_glr, o_gr, o_sz, o_sx, o_sdt, o_mg, o_end) = offs

    ones = lambda n: jnp.ones((1, n), F32)
    qk_scale = jnp.concatenate([jnp.full((1, da_w), DA_HEAD_DIM ** -0.5, F32), ones(da_w)], axis=1)

    xf = x.reshape(t, d)
    xb = xf.astype(BF16)
    for l in range(depth):
        wl = w_in[l].astype(BF16)
        w_qk = wl[:, o_daq:o_dav]
        w_vt = wl[:, o_dav:o_gq].T
        w_gla = jnp.concatenate([wl[:, o_gq:o_glr], wl[:, o_gr:o_sz]], axis=1)
        w_ssd = wl[:, o_sz:o_sdt]
        w_small = jnp.concatenate([
            jnp.pad(wl[:, o_glr:o_gr], ((0, 0), (0, LANES - GLA_GATE_RANK))),
            jnp.pad(wl[:, o_sdt:o_mg], ((0, 0), (0, LANES - SSD_HEADS)))], axis=1)
        w_mg = wl[:, o_mg:o_end]

        qk = _matmul(xb, w_qk, qk_scale, BF16, 512, 1024, "proj_attn_qk")
        vt = _matmul_nt(w_vt, xb, BF16, 512, "proj_attn_vt")
        pg = _matmul(xb, w_gla, ones(w_gla.shape[1]), F32, 512, 1024, "proj_gla")
        ps = _matmul(xb, w_ssd, ones(w_ssd.shape[1]), F32, 512, 1024, "proj_ssd")
        small = _matmul(xb, w_small, ones(2 * LANES), F32, 512, 2 * LANES, "proj_small")

        br_a = _diff_attention(qk, vt, da_lambda[l], da_norm_g[l], l, batch, seq)
        w2p = jnp.pad(gla_gate_w2[l], ((0, LANES - GLA_GATE_RANK), (0, 0))).astype(BF16)
        br_b = _gla(pg, small, w2p, gla_gate_b[l].reshape(1, gk_w), gla_norm_g[l], batch, seq)
        br_c = _ssd(ps, small, ssd_conv_w[l], ssd_conv_b[l], ssd_dt_bias[l], ssd_a_log[l], ssd_d[l],
                    ssd_norm_g[l], batch, seq)

        merged = _merge(xb, w_mg, br_a, br_b, br_c, w_branch[l].astype(BF16), 512, 512)
        xf, xb, xp = _outproj_ln(merged, w_out[l].astype(BF16), xf, ln1_g[l], ln1_b[l], alpha, 2 * TOKEN_TILE)

        eidx, wts, pos, cnt = _router(xf, router_w[l], router_bias[l], TOKEN_TILE)
        pstart, layout, n_rows = _moe_layout(cnt, t)
        dest = _slots(pstart, eidx, pos, min(t, 2048))
        xs = _dispatch(dest, xp, n_rows, TOKEN_TILE)
        ys = _experts(xs, layout, exp_w_gate, exp_w_up, exp_w_down, l)
        wgu = jnp.concatenate([sh_w_gate[l], sh_w_up[l]], axis=1).astype(BF16)
        xf, xb = _shared_ln(dest, xf, xb, wts.T, ys, wgu, sh_w_down[l].astype(BF16), ln2_g[l], ln2_b[l], alpha,
                            TOKEN_TILE)
    return xf.reshape(batch, seq, d)
```

```python
import functools
import math

import jax
import jax.numpy as jnp
import numpy as np
from jax import lax
from jax.experimental import pallas as pl
from jax.experimental.pallas import tpu as pltpu

F32 = jnp.float32
BF16 = jnp.bfloat16

DA_HEADS = 4
DA_HEAD_DIM = 128
DA_V_DIM = 2 * DA_HEAD_DIM
GLA_HEADS = 4
GLA_DK = 128
GLA_DV = 256
GLA_GATE_RANK = 16
GLA_TAU = 16.0
GLA_CHUNK = 64
SSD_HEADS = 16
SSD_HEAD_DIM = 64
SSD_STATE = 128
SSD_GROUPS = 4
SSD_CONV = 4
SSD_CHUNK = 128
SSD_WIDTH = SSD_HEADS * SSD_HEAD_DIM
N_BRANCH = 3
BRANCH_WIDTH = 1024
N_EXPERTS = 64
TOP_K = 8
N_EXPERT_GROUPS = 8
TOPK_GROUPS = 4
ROUTED_SCALE = 2.5
EPS = 1e-5

LANES = 128
SUBLANES = 8
VMEM_LIMIT = 52 * 1024 * 1024

ATTN_BLOCK = 512
ATTN_HEADS_PER_STEP = 4
SSD_CHUNKS_PER_STEP = 4
GLA_BLOCK = 512
MOE_BLOCK = 512
TOKEN_TILE = 256


def _params(sem):
    return pltpu.CompilerParams(dimension_semantics=sem, vmem_limit_bytes=VMEM_LIMIT)


def _silu(x):
    return x * (1.0 / (1.0 + jnp.exp(-x)))


def _sigmoid(x):
    return 1.0 / (1.0 + jnp.exp(-x))


def _split3(x):
    h1 = x.astype(BF16)
    r1 = x - h1.astype(F32)
    h2 = r1.astype(BF16)
    r2 = r1 - h2.astype(F32)
    return h1, h2, r2.astype(BF16)


HI_HALF = -65536


def _pack_bf16_pair(lo, hi):
    lo_bits = lax.bitcast_convert_type(lo.astype(BF16).astype(F32), jnp.int32)
    hi_bits = lax.bitcast_convert_type(hi.astype(BF16).astype(F32), jnp.int32)
    return lax.shift_right_logical(lo_bits, 16) | (hi_bits & HI_HALF)


def _unpack_bf16_pair(w):
    lo = lax.bitcast_convert_type(lax.shift_left(w, 16), F32)
    hi = lax.bitcast_convert_type(w & HI_HALF, F32)
    return lo, hi


def _dot(a, b):
    return jnp.dot(a, b, preferred_element_type=F32)


def _dot_nt(a, b):
    return lax.dot_general(a, b, (((1,), (1,)), ((), ())), preferred_element_type=F32)


def _dot_tn(a, b):
    return lax.dot_general(a, b, (((0,), (0,)), ((), ())), preferred_element_type=F32)


def _mm_kernel(a_ref, w_ref, s_ref, o_ref):
    acc = _dot(a_ref[...], w_ref[...])
    o_ref[...] = (acc * s_ref[...]).astype(o_ref.dtype)


def _matmul(a, w, col_scale, out_dtype, tm, tn, name):
    m, k = a.shape
    n = w.shape[1]
    return pl.pallas_call(
        _mm_kernel,
        out_shape=jax.ShapeDtypeStruct((m, n), out_dtype),
        grid=(n // tn, m // tm),
        in_specs=[
            pl.BlockSpec((tm, k), lambda j, i: (i, 0)),
            pl.BlockSpec((k, tn), lambda j, i: (0, j)),
            pl.BlockSpec((1, tn), lambda j, i: (0, j)),
        ],
        out_specs=pl.BlockSpec((tm, tn), lambda j, i: (i, j)),
        compiler_params=_params(("parallel", "arbitrary")),
        name=name,
    )(a, w, col_scale)


def _mm_nt_kernel(wt_ref, a_ref, o_ref):
    o_ref[...] = _dot_nt(wt_ref[...], a_ref[...]).astype(o_ref.dtype)


def _matmul_nt(wt, a, out_dtype, tm, name):
    n, k = wt.shape
    m = a.shape[0]
    return pl.pallas_call(
        _mm_nt_kernel,
        out_shape=jax.ShapeDtypeStruct((n, m), out_dtype),
        grid=(m // tm,),
        in_specs=[
            pl.BlockSpec((n, k), lambda i: (0, 0)),
            pl.BlockSpec((tm, k), lambda i: (i, 0)),
        ],
        out_specs=pl.BlockSpec((n, tm), lambda i: (0, i)),
        compiler_params=_params(("parallel",)),
        name=name,
    )(wt, a)


ATTN_KEYS_FULL, ATTN_KEYS_DIAG_LAST, ATTN_KEYS_DIAG_FIRST = 0, 1, 2


def _attn_kernel(qi_tab, kj_tab, kind_tab, q_ref, k_ref, vt_ref, koff_ref, slope_ref, lam_ref, g_ref, o_ref,
                 m_ref, l_ref, acc_ref, *, blk, lam_init):
    p = pl.program_id(2)
    qi = qi_tab[p]
    kj = kj_tab[p]
    kind = kind_tab[p]

    @pl.when(kj == 0)
    def _init():
        m_ref[...] = jnp.full(m_ref.shape, -jnp.inf, F32)
        l_ref[...] = jnp.zeros(l_ref.shape, F32)
        acc_ref[...] = jnp.zeros(acc_ref.shape, F32)

    rel = jnp.full((1, blk), (2 * kj - qi) * blk, jnp.int32).astype(F32)

    def step(kind_static):
        nk = blk if kind_static == ATTN_KEYS_DIAG_FIRST else 2 * blk
        if kind_static != ATTN_KEYS_FULL:
            kc = lax.broadcasted_iota(jnp.int32, (nk, blk), 0)
            qr = lax.broadcasted_iota(jnp.int32, (nk, blk), 1)
            keep = kc <= (qr + blk if kind_static == ATTN_KEYS_DIAG_LAST else qr)
        koff = koff_ref[0:nk, :]
        for hd in range(ATTN_HEADS_PER_STEP):
            c0 = slope_ref[hd] * rel
            vt = vt_ref[hd * DA_V_DIM:(hd + 1) * DA_V_DIM, 0:nk]
            for mp in range(2):
                sl = slice(hd * DA_V_DIM + mp * DA_HEAD_DIM, hd * DA_V_DIM + (mp + 1) * DA_HEAD_DIM)
                si = 2 * hd + mp
                s = _dot_nt(k_ref[0:nk, sl], q_ref[:, sl]) + slope_ref[hd] * koff
                if kind_static != ATTN_KEYS_FULL:
                    s = jnp.where(keep, s, -jnp.inf)
                m_old = m_ref[si]
                m_new = jnp.maximum(m_old, jnp.max(s, axis=0, keepdims=True) + c0)
                alpha = jnp.exp(m_old - m_new)
                pr = jnp.exp(s - (m_new - c0))
                l_ref[si] = alpha * l_ref[si] + jnp.sum(pr, axis=0, keepdims=True)
                acc_ref[si] = alpha * acc_ref[si] + _dot(vt, pr.astype(BF16))
                m_ref[si] = m_new

    def finish():
        lp = lam_ref[...]
        lam = (jnp.exp(jnp.sum(lp[0:1] * lp[1:2], axis=1, keepdims=True))
               - jnp.exp(jnp.sum(lp[2:3] * lp[3:4], axis=1, keepdims=True)) + lam_init)
        for hd in range(ATTN_HEADS_PER_STEP):
            ot = acc_ref[2 * hd] / l_ref[2 * hd] - lam * (acc_ref[2 * hd + 1] / l_ref[2 * hd + 1])
            o = ot.T
            o = o * lax.rsqrt(jnp.mean(o * o, axis=1, keepdims=True) + EPS)
            o_ref[:, hd * DA_V_DIM:(hd + 1) * DA_V_DIM] = (o * g_ref[...] * (1.0 - lam_init)).astype(o_ref.dtype)

    @pl.when(kind == ATTN_KEYS_FULL)
    def _full():
        step(ATTN_KEYS_FULL)

    @pl.when(kind == ATTN_KEYS_DIAG_LAST)
    def _diag_last():
        step(ATTN_KEYS_DIAG_LAST)
        finish()

    @pl.when(kind == ATTN_KEYS_DIAG_FIRST)
    def _diag_first():
        step(ATTN_KEYS_DIAG_FIRST)
        finish()


def _diff_attention(qk, vt, lam_params, norm_g, layer_idx, batch, seq):
    blk = min(ATTN_BLOCK, seq // 2)
    nq = seq // blk
    nkp = nq // 2
    steps = []
    for i in range(nq):
        for j in range(i // 2 + 1):
            last = j == i // 2
            kind = ATTN_KEYS_FULL if not last else (ATTN_KEYS_DIAG_LAST if i % 2 else ATTN_KEYS_DIAG_FIRST)
            steps.append((i, j, kind))
    qi_tab, kj_tab, kind_tab = (jnp.asarray([st[c] for st in steps], jnp.int32) for c in range(3))
    lam_init = 0.8 - 0.6 * math.exp(-0.3 * layer_idx)
    h = DA_HEADS
    slopes = np.asarray([2.0 ** (-8.0 * (i + 1) / h) for i in range(h)], np.float32)
    key_off = jnp.asarray(np.broadcast_to(np.arange(2 * blk, dtype=np.float32)[:, None], (2 * blk, blk)))
    slopes = jnp.asarray(np.broadcast_to(slopes[:, None, None], (h, 1, blk)))
    t = batch * seq
    hs = ATTN_HEADS_PER_STEP
    hg = h // hs
    kern = functools.partial(_attn_kernel, blk=blk, lam_init=lam_init)
    grid_spec = pltpu.PrefetchScalarGridSpec(
        num_scalar_prefetch=3,
        grid=(batch, hg, len(steps)),
        in_specs=[
            pl.BlockSpec((blk, hs * DA_V_DIM), lambda b, hh, p, qt, kt, kd: (b * nq + qt[p], hh)),
            pl.BlockSpec((2 * blk, hs * DA_V_DIM), lambda b, hh, p, qt, kt, kd: (b * nkp + kt[p], hg + hh)),
            pl.BlockSpec((hs * DA_V_DIM, 2 * blk), lambda b, hh, p, qt, kt, kd: (hh, b * nkp + kt[p])),
            pl.BlockSpec((2 * blk, blk), lambda b, hh, p, qt, kt, kd: (0, 0)),
            pl.BlockSpec((hs, 1, blk), lambda b, hh, p, qt, kt, kd: (hh, 0, 0)),
            pl.BlockSpec((4, DA_HEAD_DIM), lambda b, hh, p, qt, kt, kd: (0, 0)),
            pl.BlockSpec((1, DA_V_DIM), lambda b, hh, p, qt, kt, kd: (0, 0)),
        ],
        out_specs=pl.BlockSpec((blk, hs * DA_V_DIM), lambda b, hh, p, qt, kt, kd: (b * nq + qt[p], hh)),
        scratch_shapes=[
            pltpu.VMEM((2 * hs, 1, blk), F32),
            pltpu.VMEM((2 * hs, 1, blk), F32),
            pltpu.VMEM((2 * hs, DA_V_DIM, blk), F32),
        ],
    )
    return pl.pallas_call(
        kern,
        out_shape=jax.ShapeDtypeStruct((t, h * DA_V_DIM), BF16),
        grid_spec=grid_spec,
        compiler_params=_params(("parallel", "parallel", "arbitrary")),
        name="diff_attention",
    )(qi_tab, kj_tab, kind_tab, qk, qk, vt, key_off, slopes, lam_params, norm_g.reshape(1, DA_V_DIM))


def _gla_kernel(pg_ref, sm_ref, w2_ref, b2_ref, g_ref, o_ref, st_ref, *, blk):
    n = pl.program_id(1)
    hk = GLA_HEADS * GLA_DK
    hv = GLA_HEADS * GLA_DV
    c_len = GLA_CHUNK

    @pl.when(n == 0)
    def _init():
        st_ref[...] = jnp.zeros(st_ref.shape, F32)

    lr = sm_ref[:, 0:LANES].astype(BF16)
    gl = _dot(lr, w2_ref[...]) + b2_ref[...]
    gk = (jnp.minimum(gl, 0.0) - jnp.log(1.0 + jnp.exp(-jnp.abs(gl)))) * (1.0 / GLA_TAU)
    r = lax.broadcasted_iota(jnp.int32, (blk, blk), 0)
    c = lax.broadcasted_iota(jnp.int32, (blk, blk), 1)
    tri = jnp.where((c <= r) & ((r // c_len) == (c // c_len)), 1.0, 0.0).astype(BF16)
    g1, g2, g3 = _split3(gk)
    bcum = _dot(tri, g1) + _dot(tri, g2) + _dot(tri, g3)

    rr = lax.broadcasted_iota(jnp.int32, (c_len, c_len), 0)
    cc = lax.broadcasted_iota(jnp.int32, (c_len, c_len), 1)
    causal = cc <= rr
    scale = GLA_DK ** -0.5
    for ci in range(blk // c_len):
        rows = slice(ci * c_len, (ci + 1) * c_len)
        for h in range(GLA_HEADS):
            kc = slice(h * GLA_DK, (h + 1) * GLA_DK)
            vc = slice(h * GLA_DV, (h + 1) * GLA_DV)
            b = bcum[rows, kc]
            b_last = b[c_len - 1:c_len, :]
            q = pg_ref[rows, kc]
            k = pg_ref[rows, hk + h * GLA_DK: hk + (h + 1) * GLA_DK]
            v = pg_ref[rows, 2 * hk + h * GLA_DV: 2 * hk + (h + 1) * GLA_DV].astype(BF16)
            rg = pg_ref[rows, 2 * hk + hv + h * GLA_DV: 2 * hk + hv + (h + 1) * GLA_DV]
            q_e = (q * scale * jnp.exp(b)).astype(BF16)
            k_e = (k * jnp.exp(-b)).astype(BF16)
            k_d = (k * jnp.exp(b_last - b)).astype(BF16)
            att = jnp.where(causal, _dot_nt(q_e, k_e), 0.0)
            st = st_ref[h]
            o = _dot(att.astype(BF16), v) + _dot_nt(q_e, st.astype(BF16))
            st_ref[h] = st * jnp.exp(b_last) + _dot_tn(v, k_d)
            o = o * lax.rsqrt(jnp.mean(o * o, axis=1, keepdims=True) + EPS) * g_ref[...]
            o_ref[rows, vc] = (o * _silu(rg)).astype(o_ref.dtype)


def _gla(pg, small, w2p, b2, norm_g, batch, seq):
    blk = min(GLA_BLOCK, seq)
    nb = seq // blk
    t = batch * seq
    hk = GLA_HEADS * GLA_DK
    hv = GLA_HEADS * GLA_DV
    width = pg.shape[1]
    return pl.pallas_call(
        functools.partial(_gla_kernel, blk=blk),
        out_shape=jax.ShapeDtypeStruct((t, hv), BF16),
        grid=(batch, nb),
        in_specs=[
            pl.BlockSpec((blk, width), lambda b, n: (b * nb + n, 0)),
            pl.BlockSpec((blk, small.shape[1]), lambda b, n: (b * nb + n, 0)),
            pl.BlockSpec((LANES, hk), lambda b, n: (0, 0)),
            pl.BlockSpec((1, hk), lambda b, n: (0, 0)),
            pl.BlockSpec((1, GLA_DV), lambda b, n: (0, 0)),
        ],
        out_specs=pl.BlockSpec((blk, hv), lambda b, n: (b * nb + n, 0)),
        scratch_shapes=[pltpu.VMEM((GLA_HEADS, GLA_DV, GLA_DK), F32)],
        compiler_params=_params(("parallel", "arbitrary")),
        name="gla",
    )(pg, small, w2p, b2, norm_g.reshape(1, GLA_DV))


def _ssd_kernel(ps_ref, sm_ref, cw_ref, cb_ref, dtb_ref, alog_ref, dsk_ref, g_ref, o_ref,
                tail_ref, st_ref):
    n = pl.program_id(1)
    q_len = SSD_CHUNK
    w = SSD_WIDTH
    gs = SSD_GROUPS * SSD_STATE
    heads_per_group = SSD_HEADS // SSD_GROUPS
    gw = heads_per_group * SSD_HEAD_DIM

    @pl.when(n == 0)
    def _init():
        tail_ref[...] = jnp.zeros(tail_ref.shape, F32)
        st_ref[...] = jnp.zeros(st_ref.shape, F32)

    r = lax.broadcasted_iota(jnp.int32, (q_len, q_len), 0)
    c = lax.broadcasted_iota(jnp.int32, (q_len, q_len), 1)
    causal = c <= r
    tri = jnp.where(causal, 1.0, 0.0).astype(BF16)

    for ci in range(SSD_CHUNKS_PER_STEP):
        rows = slice(ci * q_len, (ci + 1) * q_len)
        cur = ps_ref[rows, w:w + w + 2 * gs]
        ext = jnp.concatenate([tail_ref[...], cur], axis=0)
        acc = cb_ref[...] + cw_ref[0:1, :] * ext[SUBLANES - 3:SUBLANES - 3 + q_len]
        for i in range(1, SSD_CONV):
            off = SUBLANES - (SSD_CONV - 1) + i
            acc = acc + cw_ref[i:i + 1, :] * ext[off:off + q_len]
        tail_ref[...] = cur[q_len - SUBLANES:q_len]
        xbc = _silu(acc)
        xs = xbc[:, 0:w]
        bm = xbc[:, w:w + gs]
        cm = xbc[:, w + gs:w + 2 * gs]

        dtr = sm_ref[rows, LANES:2 * LANES] + dtb_ref[...]
        dtv = jnp.maximum(dtr, 0.0) + jnp.log(1.0 + jnp.exp(-jnp.abs(dtr)))
        da = dtv * (-jnp.exp(alog_ref[...]))
        d1, d2, d3 = _split3(da)
        a_cs = _dot(tri, d1) + _dot(tri, d2) + _dot(tri, d3)
        a_cs_t = a_cs.T
        a_last = a_cs[q_len - 1:q_len, :]
        e_last = jnp.exp(a_last)

        for g in range(SSD_GROUPS):
            bm_g = bm[:, g * SSD_STATE:(g + 1) * SSD_STATE]
            cm_g = cm[:, g * SSD_STATE:(g + 1) * SSD_STATE]
            bm_b = bm_g.astype(BF16)
            cm_b = cm_g.astype(BF16)
            cb = _dot_nt(cm_b, bm_b)
            y_diag = []
            xdd = []
            e_col = []
            e_row = []
            for rh in range(heads_per_group):
                h = g * heads_per_group + rh
                col = a_cs[:, h:h + 1]
                row = a_cs_t[h:h + 1, :]
                lm = jnp.exp(jnp.where(causal, col - row, -jnp.inf))
                xdt = xs[:, h * SSD_HEAD_DIM:(h + 1) * SSD_HEAD_DIM] * dtv[:, h:h + 1]
                y_diag.append(_dot((cb * lm).astype(BF16), xdt.astype(BF16)))
                xdd.append(xdt * jnp.exp(a_last[:, h:h + 1] - col))
                e_col.append(jnp.broadcast_to(jnp.exp(col), (q_len, SSD_HEAD_DIM)))
                e_row.append(jnp.broadcast_to(e_last[:, h:h + 1], (1, SSD_HEAD_DIM)))
            y_diag = jnp.concatenate(y_diag, axis=1)
            xdd = jnp.concatenate(xdd, axis=1)
            e_col = jnp.concatenate(e_col, axis=1)
            e_row = jnp.concatenate(e_row, axis=1)
            s_prev = st_ref[g]
            y_off = _dot(cm_b, s_prev.astype(BF16)) * e_col
            st_ref[g] = s_prev * e_row + _dot_tn(bm_b, xdd.astype(BF16))
            lanes = slice(g * gw, (g + 1) * gw)
            y = y_diag + y_off + xs[:, lanes] * dsk_ref[:, lanes]
            y = y * _silu(ps_ref[rows, lanes])
            y = y * lax.rsqrt(jnp.mean(y * y, axis=1, keepdims=True) + EPS) * g_ref[:, lanes]
            o_ref[rows, lanes] = y.astype(o_ref.dtype)


def _ssd(ps, small, conv_w, conv_b, dt_bias, a_log, d_skip, norm_g, batch, seq):
    q_len = SSD_CHUNK * SSD_CHUNKS_PER_STEP
    nb = seq // q_len
    t = batch * seq
    conv_dim = conv_w.shape[1]
    pad = LANES - SSD_HEADS
    dtb = jnp.pad(dt_bias, (0, pad)).reshape(1, LANES)
    alog = jnp.pad(a_log, (0, pad)).reshape(1, LANES)
    dsk = jnp.repeat(d_skip, SSD_HEAD_DIM).reshape(1, SSD_WIDTH)
    const = lambda b, n: (0, 0)
    return pl.pallas_call(
        _ssd_kernel,
        out_shape=jax.ShapeDtypeStruct((t, SSD_WIDTH), BF16),
        grid=(batch, nb),
        in_specs=[
            pl.BlockSpec((q_len, ps.shape[1]), lambda b, n: (b * nb + n, 0)),
            pl.BlockSpec((q_len, small.shape[1]), lambda b, n: (b * nb + n, 0)),
            pl.BlockSpec((SSD_CONV, conv_dim), const),
            pl.BlockSpec((1, conv_dim), const),
            pl.BlockSpec((1, LANES), const),
            pl.BlockSpec((1, LANES), const),
            pl.BlockSpec((1, SSD_WIDTH), const),
            pl.BlockSpec((1, SSD_WIDTH), const),
        ],
        out_specs=pl.BlockSpec((q_len, SSD_WIDTH), lambda b, n: (b * nb + n, 0)),
        scratch_shapes=[
            pltpu.VMEM((SUBLANES, conv_dim), F32),
            pltpu.VMEM((SSD_GROUPS, SSD_STATE, SSD_WIDTH // SSD_GROUPS), F32),
        ],
        compiler_params=_params(("parallel", "arbitrary")),
        name="ssd",
    )(ps, small, conv_w, conv_b.reshape(1, conv_dim), dtb, alog, dsk, norm_g.reshape(1, SSD_WIDTH))


def _merge_kernel(x_ref, wga_ref, wgb_ref, wgc_ref, ba_ref, bb_ref, bc_ref, wb_ref, o_ref):
    x = x_ref[...]
    acc = None
    for i, (wg, br) in enumerate(((wga_ref, ba_ref), (wgb_ref, bb_ref), (wgc_ref, bc_ref))):
        gate = _sigmoid(_dot(x, wg[...]))
        term = gate * _dot(br[...], wb_ref[i])
        acc = term if acc is None else acc + term
    o_ref[...] = acc.astype(o_ref.dtype)


def _merge(xb, wgate, br_a, br_b, br_c, wbr, tm, tn):
    t, d = xb.shape
    bw = br_a.shape[1]
    nj = d // tn
    gate_spec = lambda i_br: pl.BlockSpec((d, tn), lambda i, j: (0, i_br * nj + j))
    return pl.pallas_call(
        _merge_kernel,
        out_shape=jax.ShapeDtypeStruct((t, d), BF16),
        grid=(t // tm, nj),
        in_specs=[
            pl.BlockSpec((tm, d), lambda i, j: (i, 0)),
            gate_spec(0), gate_spec(1), gate_spec(2),
            pl.BlockSpec((tm, bw), lambda i, j: (i, 0)),
            pl.BlockSpec((tm, bw), lambda i, j: (i, 0)),
            pl.BlockSpec((tm, bw), lambda i, j: (i, 0)),
            pl.BlockSpec((N_BRANCH, bw, tn), lambda i, j: (0, 0, j)),
        ],
        out_specs=pl.BlockSpec((tm, tn), lambda i, j: (i, j)),
        compiler_params=_params(("parallel", "arbitrary")),
        name="gated_merge",
    )(xb, wgate, wgate, wgate, br_a, br_b, br_c, wbr)


def _layernorm(v, g, b):
    mu = jnp.mean(v, axis=1, keepdims=True)
    d = v - mu
    var = jnp.mean(d * d, axis=1, keepdims=True)
    return d * lax.rsqrt(var + EPS) * g + b


def _outproj_ln_kernel(m_ref, w_ref, x_ref, g_ref, b_ref, o_ref, ob_ref, op_ref, *, alpha):
    mix = _dot(m_ref[...], w_ref[...])
    y = _layernorm(alpha * x_ref[...] + mix, g_ref[...], b_ref[...])
    half = y.shape[1] // 2
    o_ref[...] = y
    ob_ref[...] = y.astype(BF16)
    op_ref[...] = _pack_bf16_pair(y[:, 0:half], y[:, half:])


def _outproj_ln(merged, w_out, x, g, b, alpha, tm):
    t, d = x.shape
    const = lambda i: (0, 0)
    tok = lambda i: (i, 0)
    return pl.pallas_call(
        functools.partial(_outproj_ln_kernel, alpha=alpha),
        out_shape=(jax.ShapeDtypeStruct((t, d), F32), jax.ShapeDtypeStruct((t, d), BF16),
                   jax.ShapeDtypeStruct((t, d // 2), jnp.int32)),
        grid=(t // tm,),
        in_specs=[
            pl.BlockSpec((tm, d), tok),
            pl.BlockSpec((d, d), const, pipeline_mode=pl.Buffered(1)),
            pl.BlockSpec((tm, d), tok),
            pl.BlockSpec((1, d), const),
            pl.BlockSpec((1, d), const),
        ],
        out_specs=(pl.BlockSpec((tm, d), tok), pl.BlockSpec((tm, d), tok), pl.BlockSpec((tm, d // 2), tok)),
        compiler_params=_params(("parallel",)),
        name="outproj_layernorm",
    )(merged, w_out, x, g.reshape(1, d), b.reshape(1, d))


def _router_kernel(x_ref, w_ref, bias_ref, eidx_ref, wt_ref, pos_ref, cnt_ref, carry_ref, *, tm):
    i = pl.program_id(0)
    ne = N_EXPERTS
    ng = N_EXPERT_GROUPS
    per = ne // ng

    @pl.when(i == 0)
    def _init():
        carry_ref[...] = jnp.zeros(carry_ref.shape, F32)

    x = x_ref[...]
    xh = x.astype(BF16)
    xl = (x - xh.astype(F32)).astype(BF16)
    w = w_ref[...]
    wh = w.astype(BF16)
    wl = (w - wh.astype(F32)).astype(BF16)
    logits = _dot_nt(wh, xh) + _dot_nt(wh, xl) + _dot_nt(wl, xh)
    scores = _sigmoid(logits)
    sel = scores + bias_ref[...]

    grp = sel.reshape(ng, per, tm)
    io_p = lax.broadcasted_iota(jnp.int32, (ng, per, tm), 1)
    m1 = jnp.max(grp, axis=1, keepdims=True)
    i1 = jnp.min(jnp.where(grp == m1, io_p, per), axis=1, keepdims=True)
    m2 = jnp.max(jnp.where(io_p == i1, -jnp.inf, grp), axis=1, keepdims=True)
    gscore = (m1 + m2).reshape(ng, tm)

    io_g = lax.broadcasted_iota(jnp.int32, (ng, tm), 0)
    gsel = jnp.zeros((ng, tm), F32)
    gwork = gscore
    for _ in range(TOPK_GROUPS):
        gm = jnp.max(gwork, axis=0, keepdims=True)
        gi = jnp.min(jnp.where(gwork == gm, io_g, ng), axis=0, keepdims=True)
        hit = io_g == gi
        gsel = jnp.where(hit, 1.0, gsel)
        gwork = jnp.where(hit, -jnp.inf, gwork)
    emask = jnp.broadcast_to(gsel.reshape(ng, 1, tm), (ng, per, tm)).reshape(ne, tm) > 0.5

    io_e = lax.broadcasted_iota(jnp.int32, (ne, tm), 0)
    work = jnp.where(emask, sel, -jnp.inf)
    member = jnp.zeros((ne, tm), F32)
    idx_rows = []
    w_rows = []
    for _ in range(TOP_K):
        mx = jnp.max(work, axis=0, keepdims=True)
        ei = jnp.min(jnp.where(work == mx, io_e, ne), axis=0, keepdims=True)
        hit = io_e == ei
        idx_rows.append(ei)
        w_rows.append(jnp.sum(jnp.where(hit, scores, 0.0), axis=0, keepdims=True))
        member = jnp.where(hit, 1.0, member)
        work = jnp.where(hit, -jnp.inf, work)
    wsum = w_rows[0]
    for wr in w_rows[1:]:
        wsum = wsum + wr
    inv = ROUTED_SCALE / wsum

    r = lax.broadcasted_iota(jnp.int32, (tm, tm), 0)
    c = lax.broadcasted_iota(jnp.int32, (tm, tm), 1)
    upper = jnp.where(r < c, 1.0, 0.0).astype(BF16)
    prefix = _dot(member.astype(BF16), upper) + carry_ref[...]
    for j in range(TOP_K):
        hit = io_e == idx_rows[j]
        eidx_ref[j:j + 1, :] = idx_rows[j]
        wt_ref[j:j + 1, :] = w_rows[j] * inv
        pos_ref[j:j + 1, :] = jnp.sum(jnp.where(hit, prefix, 0.0), axis=0, keepdims=True).astype(jnp.int32)
    carry = carry_ref[...] + jnp.sum(member, axis=1, keepdims=True)
    carry_ref[...] = carry
    cnt_ref[...] = jnp.broadcast_to(carry, cnt_ref.shape)


def _router(x, router_w, router_bias, tm):
    t, d = x.shape
    ne = N_EXPERTS
    return pl.pallas_call(
        functools.partial(_router_kernel, tm=tm),
        out_shape=(
            jax.ShapeDtypeStruct((TOP_K, t), jnp.int32),
            jax.ShapeDtypeStruct((TOP_K, t), F32),
            jax.ShapeDtypeStruct((TOP_K, t), jnp.int32),
            jax.ShapeDtypeStruct((ne, LANES), F32),
        ),
        grid=(t // tm,),
        in_specs=[
            pl.BlockSpec((tm, d), lambda i: (i, 0)),
            pl.BlockSpec((ne, d), lambda i: (0, 0)),
            pl.BlockSpec((ne, 1), lambda i: (0, 0)),
        ],
        out_specs=(
            pl.BlockSpec((TOP_K, tm), lambda i: (0, i)),
            pl.BlockSpec((TOP_K, tm), lambda i: (0, i)),
            pl.BlockSpec((TOP_K, tm), lambda i: (0, i)),
            pl.BlockSpec((ne, LANES), lambda i: (0, 0)),
        ),
        scratch_shapes=[pltpu.VMEM((ne, 1), F32)],
        compiler_params=_params(("arbitrary",)),
        name="router",
    )(x, router_w.T, router_bias.reshape(ne, 1))


def _slots_kernel(pstart, eidx_ref, pos_ref, dest_ref):
    eidx = eidx_ref[...]
    base = jnp.zeros(eidx.shape, jnp.int32)
    for e in range(N_EXPERTS):
        base = jnp.where(eidx == e, pstart[e], base)
    dest_ref[...] = base + pos_ref[...]


def _slots(pstart, eidx, pos, tn):
    k, t = eidx.shape
    blk = lambda i, ps: (0, i)
    grid_spec = pltpu.PrefetchScalarGridSpec(
        num_scalar_prefetch=1,
        grid=(t // tn,),
        in_specs=[pl.BlockSpec((k, tn), blk), pl.BlockSpec((k, tn), blk)],
        out_specs=pl.BlockSpec((k, tn), blk),
    )
    return pl.pallas_call(
        _slots_kernel,
        out_shape=jax.ShapeDtypeStruct((k, t), jnp.int32),
        grid_spec=grid_spec,
        compiler_params=_params(("parallel",)),
        name="moe_slots",
    )(pstart, eidx, pos)


def _dispatch_kernel(dest_ref, x_ref, xs_hbm, sem):
    tm = x_ref.shape[0]

    def issue(r, carry):
        for j in range(TOP_K):
            pltpu.make_async_copy(x_ref.at[pl.ds(r, 1), :], xs_hbm.at[pl.ds(dest_ref[j, r], 1), :],
                                  sem).start(priority=j % 2)
        return carry

    lax.fori_loop(0, tm, issue, 0)
    for j in range(TOP_K):
        pltpu.make_async_copy(x_ref, xs_hbm.at[pl.ds(0, tm), :], sem).wait()


def _dispatch(dest, x, rows, tm):
    t, d = x.shape
    return pl.pallas_call(
        _dispatch_kernel,
        out_shape=jax.ShapeDtypeStruct((rows, d), x.dtype),
        grid=(t // tm,),
        in_specs=[
            pl.BlockSpec((TOP_K, tm), lambda i: (0, i), memory_space=pltpu.SMEM),
            pl.BlockSpec((tm, d), lambda i: (i, 0)),
        ],
        out_specs=pl.BlockSpec(memory_space=pl.ANY),
        scratch_shapes=[pltpu.SemaphoreType.DMA(())],
        compiler_params=_params(("arbitrary",)),
        name="moe_dispatch",
    )(dest, x)


def _experts_kernel(blk_e, blk_rows, nvalid, blk_first, blk_ord, seq_e, npresent,
                    xs_ref, wg_hbm, wu_hbm, wd_hbm, o_ref, wg_f, wu_f, wd_f, wgu_s, wd_s, sem, *, layer):
    i = pl.program_id(0)
    f = wg_f.shape[-1]
    half = xs_ref.shape[1]

    def weight_copies(e, slot):
        return (pltpu.make_async_copy(wg_hbm.at[layer, e], wg_f.at[slot], sem.at[slot]),
                pltpu.make_async_copy(wu_hbm.at[layer, e], wu_f.at[slot], sem.at[slot]),
                pltpu.make_async_copy(wd_hbm.at[layer, e], wd_f.at[slot], sem.at[slot]))

    @pl.when(i == 0)
    def _prime():
        for cp in weight_copies(seq_e[0], 0):
            cp.start()

    @pl.when(i < nvalid[0])
    def _():
        @pl.when(blk_first[i] == 1)
        def _next_expert():
            k = blk_ord[i]
            slot = lax.rem(k, 2)

            @pl.when(k + 1 < npresent[0])
            def _prefetch():
                for cp in weight_copies(seq_e[k + 1], 1 - slot):
                    cp.start()

            for cp in weight_copies(blk_e[i], slot):
                cp.wait()
            wgu_s[:, 0:f] = wg_f[slot].astype(BF16)
            wgu_s[:, f:2 * f] = wu_f[slot].astype(BF16)
            wd_s[...] = wd_f[slot].astype(BF16)

        row = lax.broadcasted_iota(jnp.int32, xs_ref.shape, 0)
        x_lo, x_hi = _unpack_bf16_pair(jnp.where(row < blk_rows[i], xs_ref[...], 0))
        gu = _dot(x_lo.astype(BF16), wgu_s[0:half, :]) + _dot(x_hi.astype(BF16), wgu_s[half:2 * half, :])
        hid = _silu(gu[:, 0:f]) * gu[:, f:2 * f]
        out = _dot(hid.astype(BF16), wd_s[...])
        o_ref[...] = _pack_bf16_pair(out[:, 0:half], out[:, half:])

    @pl.when(i >= nvalid[0])
    def _():
        o_ref[...] = jnp.zeros(o_ref.shape, o_ref.dtype)


def _experts(xs, layout, w_gate, w_up, w_down, layer):
    p, half = xs.shape
    d = 2 * half
    f = w_gate.shape[-1]
    bm = MOE_BLOCK
    row = lambda i, be, br, nv, *_: (jnp.minimum(i, nv[0] - 1), 0)
    grid_spec = pltpu.PrefetchScalarGridSpec(
        num_scalar_prefetch=len(layout),
        grid=(p // bm,),
        in_specs=[
            pl.BlockSpec((bm, half), row),
            pl.BlockSpec(memory_space=pl.ANY),
            pl.BlockSpec(memory_space=pl.ANY),
            pl.BlockSpec(memory_space=pl.ANY),
        ],
        out_specs=pl.BlockSpec((bm, half), lambda i, *_: (i, 0)),
        scratch_shapes=[
            pltpu.VMEM((2, d, f), F32), pltpu.VMEM((2, d, f), F32), pltpu.VMEM((2, f, d), F32),
            pltpu.VMEM((d, 2 * f), BF16), pltpu.VMEM((f, d), BF16),
            pltpu.SemaphoreType.DMA((2,)),
        ],
    )
    return pl.pallas_call(
        functools.partial(_experts_kernel, layer=layer),
        out_shape=jax.ShapeDtypeStruct((p, half), jnp.int32),
        grid_spec=grid_spec,
        compiler_params=_params(("arbitrary",)),
        name="routed_experts",
    )(*layout, xs, w_gate, w_up, w_down)


def _shared_ln_kernel(dest_ref, x_ref, xb_ref, wt_ref, ys_hbm, wgu_ref, wd_ref, g_ref, b_ref, o_ref, ob_ref,
                      buf, sem, *, alpha):
    f = wd_ref.shape[0]
    tm = x_ref.shape[0]

    def issue(r, carry):
        for j in range(TOP_K):
            pltpu.make_async_copy(ys_hbm.at[pl.ds(dest_ref[j, r], 1), :], buf.at[j, pl.ds(r, 1), :],
                                  sem).start(priority=j % 2)
        return carry

    lax.fori_loop(0, tm, issue, 0)
    gu = _dot(xb_ref[...], wgu_ref[...])
    hid = _silu(gu[:, 0:f]) * gu[:, f:2 * f]
    ffn = _dot(hid.astype(BF16), wd_ref[...])
    for j in range(TOP_K):
        pltpu.make_async_copy(ys_hbm.at[pl.ds(0, tm), :], buf.at[j], sem).wait()
    half = buf.shape[2]
    r_lo = jnp.zeros((tm, half), F32)
    r_hi = jnp.zeros((tm, half), F32)
    for j in range(TOP_K):
        y_lo, y_hi = _unpack_bf16_pair(buf[j])
        wj = wt_ref[:, j:j + 1]
        r_lo = r_lo + wj * y_lo
        r_hi = r_hi + wj * y_hi
    ffn = ffn + jnp.concatenate([r_lo, r_hi], axis=1)
    y = _layernorm(alpha * x_ref[...] + ffn, g_ref[...], b_ref[...])
    o_ref[...] = y
    ob_ref[...] = y.astype(BF16)


def _shared_ln(dest, x, xb, wts_t, ys, wgu, wd, g, b, alpha, tm):
    t, d = x.shape
    f = wd.shape[0]
    const = lambda i: (0, 0)
    tok = lambda i: (i, 0)
    return pl.pallas_call(
        functools.partial(_shared_ln_kernel, alpha=alpha),
        out_shape=(jax.ShapeDtypeStruct((t, d), F32), jax.ShapeDtypeStruct((t, d), BF16)),
        grid=(t // tm,),
        in_specs=[
            pl.BlockSpec((TOP_K, tm), lambda i: (0, i), memory_space=pltpu.SMEM),
            pl.BlockSpec((tm, d), tok),
            pl.BlockSpec((tm, d), tok),
            pl.BlockSpec((tm, TOP_K), tok),
            pl.BlockSpec(memory_space=pl.ANY),
            pl.BlockSpec((d, 2 * f), const),
            pl.BlockSpec((f, d), const),
            pl.BlockSpec((1, d), const),
            pl.BlockSpec((1, d), const),
        ],
        out_specs=(pl.BlockSpec((tm, d), tok), pl.BlockSpec((tm, d), tok)),
        scratch_shapes=[pltpu.VMEM((TOP_K, tm, d // 2), jnp.int32), pltpu.SemaphoreType.DMA(())],
        compiler_params=_params(("arbitrary",)),
        name="shared_expert_combine_layernorm",
    )(dest, x, xb, wts_t, ys, wgu, wd, g.reshape(1, d), b.reshape(1, d))


def _moe_layout(cnt, t):
    counts = cnt[:, 0].astype(jnp.int32)
    padded = (counts + MOE_BLOCK - 1) // MOE_BLOCK * MOE_BLOCK
    pend = jnp.cumsum(padded)
    pstart = pend - padded
    nblk = (t * TOP_K + N_EXPERTS * MOE_BLOCK) // MOE_BLOCK
    nvalid = pend[-1] // MOE_BLOCK
    experts = jnp.arange(N_EXPERTS, dtype=jnp.int32)
    blocks = jnp.arange(nblk, dtype=jnp.int32)
    starts = jnp.minimum(blocks, nvalid - 1) * MOE_BLOCK
    blk_e = jnp.sum((pend[None, :] <= starts[:, None]).astype(jnp.int32), axis=1)
    onehot = blk_e[:, None] == experts[None, :]
    used_end = jnp.sum(jnp.where(onehot, (pstart + counts)[None, :], 0), axis=1)
    blk_rows = jnp.clip(used_end - starts, 0, MOE_BLOCK)
    present = counts > 0
    ord_e = jnp.cumsum(present.astype(jnp.int32)) - 1
    blk_ord = jnp.sum(jnp.where(onehot, ord_e[None, :], 0), axis=1)
    seq_e = jnp.sum(jnp.where(present[None, :] & (ord_e[None, :] == experts[:, None]), experts[None, :], 0), axis=1)
    prev_e = jnp.concatenate([jnp.full((1,), -1, jnp.int32), blk_e[:-1]])
    blk_first = ((blk_e != prev_e) & (blocks < nvalid)).astype(jnp.int32)
    i32 = lambda v: v.astype(jnp.int32)
    layout = (i32(blk_e), i32(blk_rows), i32(nvalid).reshape(1), blk_first, i32(blk_ord), i32(seq_e),
              i32(jnp.sum(present)).reshape(1))
    return i32(pstart), layout, nblk * MOE_BLOCK


def kernel(x, w_in, da_lambda, da_norm_g, gla_gate_w2, gla_gate_b, gla_norm_g, ssd_conv_w, ssd_conv_b,
           ssd_dt_bias, ssd_a_log, ssd_d, ssd_norm_g, w_branch, w_out, ln1_g, ln1_b, router_w, router_bias,
           exp_w_gate, exp_w_up, exp_w_down, sh_w_gate, sh_w_up, sh_w_down, ln2_g, ln2_b):
    batch, seq, d = x.shape
    depth = w_in.shape[0]
    t = batch * seq
    alpha = (2 * depth) ** 0.25

    da_w = DA_HEADS * DA_V_DIM
    gk_w = GLA_HEADS * GLA_DK
    gv_w = GLA_HEADS * GLA_DV
    conv_dim = SSD_WIDTH + 2 * SSD_GROUPS * SSD_STATE
    sizes = (da_w, da_w, da_w, gk_w, gk_w, gv_w, GLA_GATE_RANK, gv_w, SSD_WIDTH, conv_dim, SSD_HEADS,
             N_BRANCH * d)
    offs = np.concatenate([[0], np.cumsum(sizes)]).tolist()
    (o_daq, o_dak, o_dav, o_gq, o_gk, o_gv, o_glr, o_gr, o_sz, o_sx, o_sdt, o_mg, o_end) = offs

    ones = lambda n: jnp.ones((1, n), F32)
    qk_scale = jnp.concatenate([jnp.full((1, da_w), DA_HEAD_DIM ** -0.5, F32), ones(da_w)], axis=1)

    xf = x.reshape(t, d)
    xb = xf.astype(BF16)
    for l in range(depth):
        wl = w_in[l].astype(BF16)
        w_qk = wl[:, o_daq:o_dav]
        w_vt = wl[:, o_dav:o_gq].T
        w_gla = jnp.concatenate([wl[:, o_gq:o_glr], wl[:, o_gr:o_sz]], axis=1)
        w_ssd = wl[:, o_sz:o_sdt]
        w_small = jnp.concatenate([
            jnp.pad(wl[:, o_glr:o_gr], ((0, 0), (0, LANES - GLA_GATE_RANK))),
            jnp.pad(wl[:, o_sdt:o_mg], ((0, 0), (0, LANES - SSD_HEADS)))], axis=1)
        w_mg = wl[:, o_mg:o_end]

        qk = _matmul(xb, w_qk, qk_scale, BF16, 512, 1024, "proj_attn_qk")
        vt = _matmul_nt(w_vt, xb, BF16, 512, "proj_attn_vt")
        pg = _matmul(xb, w_gla, ones(w_gla.shape[1]), F32, 512, 1024, "proj_gla")
        ps = _matmul(xb, w_ssd, ones(w_ssd.shape[1]), F32, 512, 1024, "proj_ssd")
        small = _matmul(xb, w_small, ones(2 * LANES), F32, 512, 2 * LANES, "proj_small")

        br_a = _diff_attention(qk, vt, da_lambda[l], da_norm_g[l], l, batch, seq)
        w2p = jnp.pad(gla_gate_w2[l], ((0, LANES - GLA_GATE_RANK), (0, 0))).astype(BF16)
        br_b = _gla(pg, small, w2p, gla_gate_b[l].reshape(1, gk_w), gla_norm_g[l], batch, seq)
        br_c = _ssd(ps, small, ssd_conv_w[l], ssd_conv_b[l], ssd_dt_bias[l], ssd_a_log[l], ssd_d[l],
                    ssd_norm_g[l], batch, seq)

        merged = _merge(xb, w_mg, br_a, br_b, br_c, w_branch[l].astype(BF16), 512, 512)
        xf, xb, xp = _outproj_ln(merged, w_out[l].astype(BF16), xf, ln1_g[l], ln1_b[l], alpha, 2 * TOKEN_TILE)

        eidx, wts, pos, cnt = _router(xf, router_w[l], router_bias[l], TOKEN_TILE)
        pstart, layout, n_rows = _moe_layout(cnt, t)
        dest = _slots(pstart, eidx, pos, min(t, 2048))
        xs = _dispatch(dest, xp, n_rows, TOKEN_TILE)
        ys = _experts(xs, layout, exp_w_gate, exp_w_up, exp_w_down, l)
        wgu = jnp.concatenate([sh_w_gate[l], sh_w_up[l]], axis=1).astype(BF16)
        xf, xb = _shared_ln(dest, xf, xb, wts.T, ys, wgu, sh_w_down[l].astype(BF16), ln2_g[l], ln2_b[l], alpha,
                            TOKEN_TILE)
    return xf.reshape(batch, seq, d)
```

```python
import functools
import math

import jax
import jax.numpy as jnp
import numpy as np
from jax import lax
from jax.experimental import pallas as pl
from jax.experimental.pallas import tpu as pltpu

F32 = jnp.float32
BF16 = jnp.bfloat16

DA_HEADS = 4
DA_HEAD_DIM = 128
DA_V_DIM = 2 * DA_HEAD_DIM
GLA_HEADS = 4
GLA_DK = 128
GLA_DV = 256
GLA_GATE_RANK = 16
GLA_TAU = 16.0
GLA_CHUNK = 64
SSD_HEADS = 16
SSD_HEAD_DIM = 64
SSD_STATE = 128
SSD_GROUPS = 4
SSD_CONV = 4
SSD_CHUNK = 128
SSD_WIDTH = SSD_HEADS * SSD_HEAD_DIM
N_BRANCH = 3
BRANCH_WIDTH = 1024
N_EXPERTS = 64
TOP_K = 8
N_EXPERT_GROUPS = 8
TOPK_GROUPS = 4
ROUTED_SCALE = 2.5
EPS = 1e-5

LANES = 128
SUBLANES = 8
VMEM_LIMIT = 52 * 1024 * 1024

ATTN_BLOCK = 512
ATTN_HEADS_PER_STEP = 4
SSD_CHUNKS_PER_STEP = 4
GLA_BLOCK = 512
MOE_BLOCK = 512
TOKEN_TILE = 256


def _params(sem):
    return pltpu.CompilerParams(dimension_semantics=sem, vmem_limit_bytes=VMEM_LIMIT)


def _silu(x):
    return x * (1.0 / (1.0 + jnp.exp(-x)))


def _sigmoid(x):
    return 1.0 / (1.0 + jnp.exp(-x))


def _split3(x):
    h1 = x.astype(BF16)
    r1 = x - h1.astype(F32)
    h2 = r1.astype(BF16)
    r2 = r1 - h2.astype(F32)
    return h1, h2, r2.astype(BF16)


HI_HALF = -65536


def _pack_bf16_pair(lo, hi):
    lo_bits = lax.bitcast_convert_type(lo.astype(BF16).astype(F32), jnp.int32)
    hi_bits = lax.bitcast_convert_type(hi.astype(BF16).astype(F32), jnp.int32)
    return lax.shift_right_logical(lo_bits, 16) | (hi_bits & HI_HALF)


def _unpack_bf16_pair(w):
    lo = lax.bitcast_convert_type(lax.shift_left(w, 16), F32)
    hi = lax.bitcast_convert_type(w & HI_HALF, F32)
    return lo, hi


def _dot(a, b):
    return jnp.dot(a, b, preferred_element_type=F32)


def _dot_nt(a, b):
    return lax.dot_general(a, b, (((1,), (1,)), ((), ())), preferred_element_type=F32)


def _dot_tn(a, b):
    return lax.dot_general(a, b, (((0,), (0,)), ((), ())), preferred_element_type=F32)


def _mm_kernel(a_ref, w_ref, s_ref, o_ref):
    acc = _dot(a_ref[...], w_ref[...])
    o_ref[...] = (acc * s_ref[...]).astype(o_ref.dtype)


def _matmul(a, w, col0, col_scale, out_dtype, tm, tn, name):
    m, k = a.shape
    n = col_scale.shape[1]
    j0 = col0 // tn
    return pl.pallas_call(
        _mm_kernel,
        out_shape=jax.ShapeDtypeStruct((m, n), out_dtype),
        grid=(n // tn, m // tm),
        in_specs=[
            pl.BlockSpec((tm, k), lambda j, i: (i, 0)),
            pl.BlockSpec((k, tn), lambda j, i: (0, j0 + j)),
            pl.BlockSpec((1, tn), lambda j, i: (0, j)),
        ],
        out_specs=pl.BlockSpec((tm, tn), lambda j, i: (i, j)),
        compiler_params=_params(("parallel", "arbitrary")),
        name=name,
    )(a, w, col_scale)


def _mm_nt_kernel(wt_ref, a_ref, o_ref):
    o_ref[...] = _dot_nt(wt_ref[...], a_ref[...]).astype(o_ref.dtype)


def _matmul_nt(wt, a, out_dtype, tm, name):
    n, k = wt.shape
    m = a.shape[0]
    return pl.pallas_call(
        _mm_nt_kernel,
        out_shape=jax.ShapeDtypeStruct((n, m), out_dtype),
        grid=(m // tm,),
        in_specs=[
            pl.BlockSpec((n, k), lambda i: (0, 0)),
            pl.BlockSpec((tm, k), lambda i: (i, 0)),
        ],
        out_specs=pl.BlockSpec((n, tm), lambda i: (0, i)),
        compiler_params=_params(("parallel",)),
        name=name,
    )(wt, a)


ATTN_KEYS_FULL, ATTN_KEYS_DIAG_LAST, ATTN_KEYS_DIAG_FIRST = 0, 1, 2


def _attn_kernel(qi_tab, kj_tab, kind_tab, q_ref, k_ref, vt_ref, koff_ref, slope_ref, lam_ref, g_ref, o_ref,
                 m_ref, l_ref, acc_ref, *, blk, lam_init):
    p = pl.program_id(2)
    qi = qi_tab[p]
    kj = kj_tab[p]
    kind = kind_tab[p]

    @pl.when(kj == 0)
    def _init():
        m_ref[...] = jnp.full(m_ref.shape, -jnp.inf, F32)
        l_ref[...] = jnp.zeros(l_ref.shape, F32)
        acc_ref[...] = jnp.zeros(acc_ref.shape, F32)

    rel = jnp.full((1, blk), (2 * kj - qi) * blk, jnp.int32).astype(F32)

    def step(kind_static):
        nk = blk if kind_static == ATTN_KEYS_DIAG_FIRST else 2 * blk
        if kind_static != ATTN_KEYS_FULL:
            kc = lax.broadcasted_iota(jnp.int32, (nk, blk), 0)
            qr = lax.broadcasted_iota(jnp.int32, (nk, blk), 1)
            keep = kc <= (qr + blk if kind_static == ATTN_KEYS_DIAG_LAST else qr)
        koff = koff_ref[0:nk, :]
        for hd in range(ATTN_HEADS_PER_STEP):
            c0 = slope_ref[hd] * rel
            vt = vt_ref[hd * DA_V_DIM:(hd + 1) * DA_V_DIM, 0:nk]
            for mp in range(2):
                sl = slice(hd * DA_V_DIM + mp * DA_HEAD_DIM, hd * DA_V_DIM + (mp + 1) * DA_HEAD_DIM)
                si = 2 * hd + mp
                s = _dot_nt(k_ref[0:nk, sl], q_ref[:, sl]) + slope_ref[hd] * koff
                if kind_static != ATTN_KEYS_FULL:
                    s = jnp.where(keep, s, -jnp.inf)
                m_old = m_ref[si]
                m_new = jnp.maximum(m_old, jnp.max(s, axis=0, keepdims=True) + c0)
                alpha = jnp.exp(m_old - m_new)
                pr = jnp.exp(s - (m_new - c0))
                l_ref[si] = alpha * l_ref[si] + jnp.sum(pr, axis=0, keepdims=True)
                acc_ref[si] = alpha * acc_ref[si] + _dot(vt, pr.astype(BF16))
                m_ref[si] = m_new

    def finish():
        lp = lam_ref[...]
        lam = (jnp.exp(jnp.sum(lp[0:1] * lp[1:2], axis=1, keepdims=True))
               - jnp.exp(jnp.sum(lp[2:3] * lp[3:4], axis=1, keepdims=True)) + lam_init)
        for hd in range(ATTN_HEADS_PER_STEP):
            ot = acc_ref[2 * hd] / l_ref[2 * hd] - lam * (acc_ref[2 * hd + 1] / l_ref[2 * hd + 1])
            o = ot.T
            o = o * lax.rsqrt(jnp.mean(o * o, axis=1, keepdims=True) + EPS)
            o_ref[:, hd * DA_V_DIM:(hd + 1) * DA_V_DIM] = (o * g_ref[...] * (1.0 - lam_init)).astype(o_ref.dtype)

    @pl.when(kind == ATTN_KEYS_FULL)
    def _full():
        step(ATTN_KEYS_FULL)

    @pl.when(kind == ATTN_KEYS_DIAG_LAST)
    def _diag_last():
        step(ATTN_KEYS_DIAG_LAST)
        finish()

    @pl.when(kind == ATTN_KEYS_DIAG_FIRST)
    def _diag_first():
        step(ATTN_KEYS_DIAG_FIRST)
        finish()


def _diff_attention(qk, vt, lam_params, norm_g, layer_idx, batch, seq):
    blk = min(ATTN_BLOCK, seq // 2)
    nq = seq // blk
    nkp = nq // 2
    steps = []
    for i in range(nq):
        for j in range(i // 2 + 1):
            last = j == i // 2
            kind = ATTN_KEYS_FULL if not last else (ATTN_KEYS_DIAG_LAST if i % 2 else ATTN_KEYS_DIAG_FIRST)
            steps.append((i, j, kind))
    qi_tab, kj_tab, kind_tab = (jnp.asarray([st[c] for st in steps], jnp.int32) for c in range(3))
    lam_init = 0.8 - 0.6 * math.exp(-0.3 * layer_idx)
    h = DA_HEADS
    slopes = np.asarray([2.0 ** (-8.0 * (i + 1) / h) for i in range(h)], np.float32)
    key_off = jnp.asarray(np.broadcast_to(np.arange(2 * blk, dtype=np.float32)[:, None], (2 * blk, blk)))
    slopes = jnp.asarray(np.broadcast_to(slopes[:, None, None], (h, 1, blk)))
    t = batch * seq
    hs = ATTN_HEADS_PER_STEP
    hg = h // hs
    kern = functools.partial(_attn_kernel, blk=blk, lam_init=lam_init)
    grid_spec = pltpu.PrefetchScalarGridSpec(
        num_scalar_prefetch=3,
        grid=(batch, hg, len(steps)),
        in_specs=[
            pl.BlockSpec((blk, hs * DA_V_DIM), lambda b, hh, p, qt, kt, kd: (b * nq + qt[p], hh)),
            pl.BlockSpec((2 * blk, hs * DA_V_DIM), lambda b, hh, p, qt, kt, kd: (b * nkp + kt[p], hg + hh)),
            pl.BlockSpec((hs * DA_V_DIM, 2 * blk), lambda b, hh, p, qt, kt, kd: (hh, b * nkp + kt[p])),
            pl.BlockSpec((2 * blk, blk), lambda b, hh, p, qt, kt, kd: (0, 0)),
            pl.BlockSpec((hs, 1, blk), lambda b, hh, p, qt, kt, kd: (hh, 0, 0)),
            pl.BlockSpec((4, DA_HEAD_DIM), lambda b, hh, p, qt, kt, kd: (0, 0)),
            pl.BlockSpec((1, DA_V_DIM), lambda b, hh, p, qt, kt, kd: (0, 0)),
        ],
        out_specs=pl.BlockSpec((blk, hs * DA_V_DIM), lambda b, hh, p, qt, kt, kd: (b * nq + qt[p], hh)),
        scratch_shapes=[
            pltpu.VMEM((2 * hs, 1, blk), F32),
            pltpu.VMEM((2 * hs, 1, blk), F32),
            pltpu.VMEM((2 * hs, DA_V_DIM, blk), F32),
        ],
    )
    return pl.pallas_call(
        kern,
        out_shape=jax.ShapeDtypeStruct((t, h * DA_V_DIM), BF16),
        grid_spec=grid_spec,
        compiler_params=_params(("parallel", "parallel", "arbitrary")),
        name="diff_attention",
    )(qi_tab, kj_tab, kind_tab, qk, qk, vt, key_off, slopes, lam_params, norm_g.reshape(1, DA_V_DIM))


def _gla_kernel(pg_ref, sm_ref, w2_ref, b2_ref, g_ref, o_ref, st_ref, *, blk):
    n = pl.program_id(1)
    hk = GLA_HEADS * GLA_DK
    hv = GLA_HEADS * GLA_DV
    c_len = GLA_CHUNK

    @pl.when(n == 0)
    def _init():
        st_ref[...] = jnp.zeros(st_ref.shape, F32)

    lr = sm_ref[:, 0:LANES].astype(BF16)
    gl = _dot(lr, w2_ref[...]) + b2_ref[...]
    gk = (jnp.minimum(gl, 0.0) - jnp.log(1.0 + jnp.exp(-jnp.abs(gl)))) * (1.0 / GLA_TAU)
    r = lax.broadcasted_iota(jnp.int32, (blk, blk), 0)
    c = lax.broadcasted_iota(jnp.int32, (blk, blk), 1)
    tri = jnp.where((c <= r) & ((r // c_len) == (c // c_len)), 1.0, 0.0).astype(BF16)
    g1, g2, g3 = _split3(gk)
    bcum = _dot(tri, g1) + _dot(tri, g2) + _dot(tri, g3)

    rr = lax.broadcasted_iota(jnp.int32, (c_len, c_len), 0)
    cc = lax.broadcasted_iota(jnp.int32, (c_len, c_len), 1)
    causal = cc <= rr
    scale = GLA_DK ** -0.5
    for ci in range(blk // c_len):
        rows = slice(ci * c_len, (ci + 1) * c_len)
        for h in range(GLA_HEADS):
            kc = slice(h * GLA_DK, (h + 1) * GLA_DK)
            vc = slice(h * GLA_DV, (h + 1) * GLA_DV)
            b = bcum[rows, kc]
            b_last = b[c_len - 1:c_len, :]
            q = pg_ref[rows, kc]
            k = pg_ref[rows, hk + h * GLA_DK: hk + (h + 1) * GLA_DK]
            v = pg_ref[rows, 2 * hk + h * GLA_DV: 2 * hk + (h + 1) * GLA_DV].astype(BF16)
            rg = pg_ref[rows, 2 * hk + hv + h * GLA_DV: 2 * hk + hv + (h + 1) * GLA_DV]
            q_e = (q * scale * jnp.exp(b)).astype(BF16)
            k_e = (k * jnp.exp(-b)).astype(BF16)
            k_d = (k * jnp.exp(b_last - b)).astype(BF16)
            att = jnp.where(causal, _dot_nt(q_e, k_e), 0.0)
            st = st_ref[h]
            o = _dot(att.astype(BF16), v) + _dot_nt(q_e, st.astype(BF16))
            st_ref[h] = st * jnp.exp(b_last) + _dot_tn(v, k_d)
            o = o * lax.rsqrt(jnp.mean(o * o, axis=1, keepdims=True) + EPS) * g_ref[...]
            o_ref[rows, vc] = (o * _silu(rg)).astype(o_ref.dtype)


def _gla(pg, small, w2p, b2, norm_g, batch, seq):
    blk = min(GLA_BLOCK, seq)
    nb = seq // blk
    t = batch * seq
    hk = GLA_HEADS * GLA_DK
    hv = GLA_HEADS * GLA_DV
    width = pg.shape[1]
    return pl.pallas_call(
        functools.partial(_gla_kernel, blk=blk),
        out_shape=jax.ShapeDtypeStruct((t, hv), BF16),
        grid=(batch, nb),
        in_specs=[
            pl.BlockSpec((blk, width), lambda b, n: (b * nb + n, 0)),
            pl.BlockSpec((blk, small.shape[1]), lambda b, n: (b * nb + n, 0)),
            pl.BlockSpec((LANES, hk), lambda b, n: (0, 0)),
            pl.BlockSpec((1, hk), lambda b, n: (0, 0)),
            pl.BlockSpec((1, GLA_DV), lambda b, n: (0, 0)),
        ],
        out_specs=pl.BlockSpec((blk, hv), lambda b, n: (b * nb + n, 0)),
        scratch_shapes=[pltpu.VMEM((GLA_HEADS, GLA_DV, GLA_DK), F32)],
        compiler_params=_params(("parallel", "arbitrary")),
        name="gla",
    )(pg, small, w2p, b2, norm_g.reshape(1, GLA_DV))


def _ssd_kernel(ps_ref, sm_ref, cw_ref, cb_ref, dtb_ref, alog_ref, dsk_ref, g_ref, o_ref,
                tail_ref, st_ref):
    n = pl.program_id(1)
    q_len = SSD_CHUNK
    w = SSD_WIDTH
    gs = SSD_GROUPS * SSD_STATE
    heads_per_group = SSD_HEADS // SSD_GROUPS
    gw = heads_per_group * SSD_HEAD_DIM

    @pl.when(n == 0)
    def _init():
        tail_ref[...] = jnp.zeros(tail_ref.shape, F32)
        st_ref[...] = jnp.zeros(st_ref.shape, F32)

    r = lax.broadcasted_iota(jnp.int32, (q_len, q_len), 0)
    c = lax.broadcasted_iota(jnp.int32, (q_len, q_len), 1)
    causal = c <= r
    tri = jnp.where(causal, 1.0, 0.0).astype(BF16)

    for ci in range(SSD_CHUNKS_PER_STEP):
        rows = slice(ci * q_len, (ci + 1) * q_len)
        cur = ps_ref[rows, w:w + w + 2 * gs]
        ext = jnp.concatenate([tail_ref[...], cur], axis=0)
        acc = cb_ref[...] + cw_ref[0:1, :] * ext[SUBLANES - 3:SUBLANES - 3 + q_len]
        for i in range(1, SSD_CONV):
            off = SUBLANES - (SSD_CONV - 1) + i
            acc = acc + cw_ref[i:i + 1, :] * ext[off:off + q_len]
        tail_ref[...] = cur[q_len - SUBLANES:q_len]
        xbc = _silu(acc)
        xs = xbc[:, 0:w]
        bm = xbc[:, w:w + gs]
        cm = xbc[:, w + gs:w + 2 * gs]

        dtr = sm_ref[rows, LANES:2 * LANES] + dtb_ref[...]
        dtv = jnp.maximum(dtr, 0.0) + jnp.log(1.0 + jnp.exp(-jnp.abs(dtr)))
        da = dtv * (-jnp.exp(alog_ref[...]))
        d1, d2, d3 = _split3(da)
        a_cs = _dot(tri, d1) + _dot(tri, d2) + _dot(tri, d3)
        a_cs_t = a_cs.T
        a_last = a_cs[q_len - 1:q_len, :]
        e_last = jnp.exp(a_last)

        for g in range(SSD_GROUPS):
            bm_g = bm[:, g * SSD_STATE:(g + 1) * SSD_STATE]
            cm_g = cm[:, g * SSD_STATE:(g + 1) * SSD_STATE]
            bm_b = bm_g.astype(BF16)
            cm_b = cm_g.astype(BF16)
            cb = _dot_nt(cm_b, bm_b)
            y_diag = []
            xdd = []
            e_col = []
            e_row = []
            for rh in range(heads_per_group):
                h = g * heads_per_group + rh
                col = a_cs[:, h:h + 1]
                row = a_cs_t[h:h + 1, :]
                lm = jnp.exp(jnp.where(causal, col - row, -jnp.inf))
                xdt = xs[:, h * SSD_HEAD_DIM:(h + 1) * SSD_HEAD_DIM] * dtv[:, h:h + 1]
                y_diag.append(_dot((cb * lm).astype(BF16), xdt.astype(BF16)))
                xdd.append(xdt * jnp.exp(a_last[:, h:h + 1] - col))
                e_col.append(jnp.broadcast_to(jnp.exp(col), (q_len, SSD_HEAD_DIM)))
                e_row.append(jnp.broadcast_to(e_last[:, h:h + 1], (1, SSD_HEAD_DIM)))
            y_diag = jnp.concatenate(y_diag, axis=1)
            xdd = jnp.concatenate(xdd, axis=1)
            e_col = jnp.concatenate(e_col, axis=1)
            e_row = jnp.concatenate(e_row, axis=1)
            s_prev = st_ref[g]
            y_off = _dot(cm_b, s_prev.astype(BF16)) * e_col
            st_ref[g] = s_prev * e_row + _dot_tn(bm_b, xdd.astype(BF16))
            lanes = slice(g * gw, (g + 1) * gw)
            y = y_diag + y_off + xs[:, lanes] * dsk_ref[:, lanes]
            y = y * _silu(ps_ref[rows, lanes])
            y = y * lax.rsqrt(jnp.mean(y * y, axis=1, keepdims=True) + EPS) * g_ref[:, lanes]
            o_ref[rows, lanes] = y.astype(o_ref.dtype)


def _ssd(ps, small, conv_w, conv_b, dt_bias, a_log, d_skip, norm_g, batch, seq):
    q_len = SSD_CHUNK * SSD_CHUNKS_PER_STEP
    nb = seq // q_len
    t = batch * seq
    conv_dim = conv_w.shape[1]
    pad = LANES - SSD_HEADS
    dtb = jnp.pad(dt_bias, (0, pad)).reshape(1, LANES)
    alog = jnp.pad(a_log, (0, pad)).reshape(1, LANES)
    dsk = jnp.repeat(d_skip, SSD_HEAD_DIM).reshape(1, SSD_WIDTH)
    const = lambda b, n: (0, 0)
    return pl.pallas_call(
        _ssd_kernel,
        out_shape=jax.ShapeDtypeStruct((t, SSD_WIDTH), BF16),
        grid=(batch, nb),
        in_specs=[
            pl.BlockSpec((q_len, ps.shape[1]), lambda b, n: (b * nb + n, 0)),
            pl.BlockSpec((q_len, small.shape[1]), lambda b, n: (b * nb + n, 0)),
            pl.BlockSpec((SSD_CONV, conv_dim), const),
            pl.BlockSpec((1, conv_dim), const),
            pl.BlockSpec((1, LANES), const),
            pl.BlockSpec((1, LANES), const),
            pl.BlockSpec((1, SSD_WIDTH), const),
            pl.BlockSpec((1, SSD_WIDTH), const),
        ],
        out_specs=pl.BlockSpec((q_len, SSD_WIDTH), lambda b, n: (b * nb + n, 0)),
        scratch_shapes=[
            pltpu.VMEM((SUBLANES, conv_dim), F32),
            pltpu.VMEM((SSD_GROUPS, SSD_STATE, SSD_WIDTH // SSD_GROUPS), F32),
        ],
        compiler_params=_params(("parallel", "arbitrary")),
        name="ssd",
    )(ps, small, conv_w, conv_b.reshape(1, conv_dim), dtb, alog, dsk, norm_g.reshape(1, SSD_WIDTH))


def _merge_kernel(x_ref, wga_ref, wgb_ref, wgc_ref, ba_ref, bb_ref, bc_ref, wb_ref, o_ref):
    x = x_ref[...]
    acc = None
    for i, (wg, br) in enumerate(((wga_ref, ba_ref), (wgb_ref, bb_ref), (wgc_ref, bc_ref))):
        gate = _sigmoid(_dot(x, wg[...]))
        term = gate * _dot(br[...], wb_ref[i])
        acc = term if acc is None else acc + term
    o_ref[...] = acc.astype(o_ref.dtype)


def _merge(xb, wgate, br_a, br_b, br_c, wbr, tm, tn):
    t, d = xb.shape
    bw = br_a.shape[1]
    nj = d // tn
    gate_spec = lambda i_br: pl.BlockSpec((d, tn), lambda i, j: (0, i_br * nj + j))
    return pl.pallas_call(
        _merge_kernel,
        out_shape=jax.ShapeDtypeStruct((t, d), BF16),
        grid=(t // tm, nj),
        in_specs=[
            pl.BlockSpec((tm, d), lambda i, j: (i, 0)),
            gate_spec(0), gate_spec(1), gate_spec(2),
            pl.BlockSpec((tm, bw), lambda i, j: (i, 0)),
            pl.BlockSpec((tm, bw), lambda i, j: (i, 0)),
            pl.BlockSpec((tm, bw), lambda i, j: (i, 0)),
            pl.BlockSpec((N_BRANCH, bw, tn), lambda i, j: (0, 0, j)),
        ],
        out_specs=pl.BlockSpec((tm, tn), lambda i, j: (i, j)),
        compiler_params=_params(("parallel", "arbitrary")),
        name="gated_merge",
    )(xb, wgate, wgate, wgate, br_a, br_b, br_c, wbr)


def _layernorm(v, g, b):
    mu = jnp.mean(v, axis=1, keepdims=True)
    d = v - mu
    var = jnp.mean(d * d, axis=1, keepdims=True)
    return d * lax.rsqrt(var + EPS) * g + b


def _outproj_ln_kernel(m_ref, w_ref, x_ref, g_ref, b_ref, o_ref, ob_ref, op_ref, *, alpha):
    mix = _dot(m_ref[...], w_ref[...])
    y = _layernorm(alpha * x_ref[...] + mix, g_ref[...], b_ref[...])
    half = y.shape[1] // 2
    o_ref[...] = y
    ob_ref[...] = y.astype(BF16)
    op_ref[...] = _pack_bf16_pair(y[:, 0:half], y[:, half:])


def _outproj_ln(merged, w_out, x, g, b, alpha, tm):
    t, d = x.shape
    const = lambda i: (0, 0)
    tok = lambda i: (i, 0)
    return pl.pallas_call(
        functools.partial(_outproj_ln_kernel, alpha=alpha),
        out_shape=(jax.ShapeDtypeStruct((t, d), F32), jax.ShapeDtypeStruct((t, d), BF16),
                   jax.ShapeDtypeStruct((t, d // 2), jnp.int32)),
        grid=(t // tm,),
        in_specs=[
            pl.BlockSpec((tm, d), tok),
            pl.BlockSpec((d, d), const, pipeline_mode=pl.Buffered(1)),
            pl.BlockSpec((tm, d), tok),
            pl.BlockSpec((1, d), const),
            pl.BlockSpec((1, d), const),
        ],
        out_specs=(pl.BlockSpec((tm, d), tok), pl.BlockSpec((tm, d), tok), pl.BlockSpec((tm, d // 2), tok)),
        compiler_params=_params(("parallel",)),
        name="outproj_layernorm",
    )(merged, w_out, x, g.reshape(1, d), b.reshape(1, d))


def _router_kernel(x_ref, w_ref, bias_ref, eidx_ref, wt_ref, pos_ref, cnt_ref, carry_ref, *, tm):
    i = pl.program_id(0)
    ne = N_EXPERTS
    ng = N_EXPERT_GROUPS
    per = ne // ng

    @pl.when(i == 0)
    def _init():
        carry_ref[...] = jnp.zeros(carry_ref.shape, F32)

    x = x_ref[...]
    xh = x.astype(BF16)
    xl = (x - xh.astype(F32)).astype(BF16)
    w = w_ref[...]
    wh = w.astype(BF16)
    wl = (w - wh.astype(F32)).astype(BF16)
    logits = _dot_nt(wh, xh) + _dot_nt(wh, xl) + _dot_nt(wl, xh)
    scores = _sigmoid(logits)
    sel = scores + bias_ref[...]

    grp = sel.reshape(ng, per, tm)
    io_p = lax.broadcasted_iota(jnp.int32, (ng, per, tm), 1)
    m1 = jnp.max(grp, axis=1, keepdims=True)
    i1 = jnp.min(jnp.where(grp == m1, io_p, per), axis=1, keepdims=True)
    m2 = jnp.max(jnp.where(io_p == i1, -jnp.inf, grp), axis=1, keepdims=True)
    gscore = (m1 + m2).reshape(ng, tm)

    io_g = lax.broadcasted_iota(jnp.int32, (ng, tm), 0)
    gsel = jnp.zeros((ng, tm), F32)
    gwork = gscore
    for _ in range(TOPK_GROUPS):
        gm = jnp.max(gwork, axis=0, keepdims=True)
        gi = jnp.min(jnp.where(gwork == gm, io_g, ng), axis=0, keepdims=True)
        hit = io_g == gi
        gsel = jnp.where(hit, 1.0, gsel)
        gwork = jnp.where(hit, -jnp.inf, gwork)
    emask = jnp.broadcast_to(gsel.reshape(ng, 1, tm), (ng, per, tm)).reshape(ne, tm) > 0.5

    io_e = lax.broadcasted_iota(jnp.int32, (ne, tm), 0)
    work = jnp.where(emask, sel, -jnp.inf)
    member = jnp.zeros((ne, tm), F32)
    idx_rows = []
    w_rows = []
    for _ in range(TOP_K):
        mx = jnp.max(work, axis=0, keepdims=True)
        ei = jnp.min(jnp.where(work == mx, io_e, ne), axis=0, keepdims=True)
        hit = io_e == ei
        idx_rows.append(ei)
        w_rows.append(jnp.sum(jnp.where(hit, scores, 0.0), axis=0, keepdims=True))
        member = jnp.where(hit, 1.0, member)
        work = jnp.where(hit, -jnp.inf, work)
    wsum = w_rows[0]
    for wr in w_rows[1:]:
        wsum = wsum + wr
    inv = ROUTED_SCALE / wsum

    r = lax.broadcasted_iota(jnp.int32, (tm, tm), 0)
    c = lax.broadcasted_iota(jnp.int32, (tm, tm), 1)
    upper = jnp.where(r < c, 1.0, 0.0).astype(BF16)
    prefix = _dot(member.astype(BF16), upper) + carry_ref[...]
    for j in range(TOP_K):
        hit = io_e == idx_rows[j]
        eidx_ref[j:j + 1, :] = idx_rows[j]
        wt_ref[j:j + 1, :] = w_rows[j] * inv
        pos_ref[j:j + 1, :] = jnp.sum(jnp.where(hit, prefix, 0.0), axis=0, keepdims=True).astype(jnp.int32)
    carry = carry_ref[...] + jnp.sum(member, axis=1, keepdims=True)
    carry_ref[...] = carry
    cnt_ref[...] = jnp.broadcast_to(carry, cnt_ref.shape)


def _router(x, router_w, router_bias, tm):
    t, d = x.shape
    ne = N_EXPERTS
    return pl.pallas_call(
        functools.partial(_router_kernel, tm=tm),
        out_shape=(
            jax.ShapeDtypeStruct((TOP_K, t), jnp.int32),
            jax.ShapeDtypeStruct((TOP_K, t), F32),
            jax.ShapeDtypeStruct((TOP_K, t), jnp.int32),
            jax.ShapeDtypeStruct((ne, LANES), F32),
        ),
        grid=(t // tm,),
        in_specs=[
            pl.BlockSpec((tm, d), lambda i: (i, 0)),
            pl.BlockSpec((ne, d), lambda i: (0, 0)),
            pl.BlockSpec((ne, 1), lambda i: (0, 0)),
        ],
        out_specs=(
            pl.BlockSpec((TOP_K, tm), lambda i: (0, i)),
            pl.BlockSpec((TOP_K, tm), lambda i: (0, i)),
            pl.BlockSpec((TOP_K, tm), lambda i: (0, i)),
            pl.BlockSpec((ne, LANES), lambda i: (0, 0)),
        ),
        scratch_shapes=[pltpu.VMEM((ne, 1), F32)],
        compiler_params=_params(("arbitrary",)),
        name="router",
    )(x, router_w.T, router_bias.reshape(ne, 1))


def _slots_kernel(pstart, eidx_ref, pos_ref, dest_ref):
    eidx = eidx_ref[...]
    base = jnp.zeros(eidx.shape, jnp.int32)
    for e in range(N_EXPERTS):
        base = jnp.where(eidx == e, pstart[e], base)
    dest_ref[...] = base + pos_ref[...]


def _slots(pstart, eidx, pos, tn):
    k, t = eidx.shape
    blk = lambda i, ps: (0, i)
    grid_spec = pltpu.PrefetchScalarGridSpec(
        num_scalar_prefetch=1,
        grid=(t // tn,),
        in_specs=[pl.BlockSpec((k, tn), blk), pl.BlockSpec((k, tn), blk)],
        out_specs=pl.BlockSpec((k, tn), blk),
    )
    return pl.pallas_call(
        _slots_kernel,
        out_shape=jax.ShapeDtypeStruct((k, t), jnp.int32),
        grid_spec=grid_spec,
        compiler_params=_params(("parallel",)),
        name="moe_slots",
    )(pstart, eidx, pos)


def _dispatch_kernel(dest_ref, x_ref, xs_hbm, sem):
    tm = x_ref.shape[0]

    def issue(r, carry):
        for j in range(TOP_K):
            pltpu.make_async_copy(x_ref.at[pl.ds(r, 1), :], xs_hbm.at[pl.ds(dest_ref[j, r], 1), :],
                                  sem).start(priority=j % 2)
        return carry

    lax.fori_loop(0, tm, issue, 0)
    for j in range(TOP_K):
        pltpu.make_async_copy(x_ref, xs_hbm.at[pl.ds(0, tm), :], sem).wait()


def _dispatch(dest, x, rows, tm):
    t, d = x.shape
    return pl.pallas_call(
        _dispatch_kernel,
        out_shape=jax.ShapeDtypeStruct((rows, d), x.dtype),
        grid=(t // tm,),
        in_specs=[
            pl.BlockSpec((TOP_K, tm), lambda i: (0, i), memory_space=pltpu.SMEM),
            pl.BlockSpec((tm, d), lambda i: (i, 0)),
        ],
        out_specs=pl.BlockSpec(memory_space=pl.ANY),
        scratch_shapes=[pltpu.SemaphoreType.DMA(())],
        compiler_params=_params(("arbitrary",)),
        name="moe_dispatch",
    )(dest, x)


def _experts_kernel(blk_e, blk_rows, nvalid, blk_first, blk_ord, seq_e, npresent,
                    xs_ref, wg_hbm, wu_hbm, wd_hbm, o_ref, wg_f, wu_f, wd_f, wgu_s, wd_s, sem, *, layer):
    i = pl.program_id(0)
    f = wg_f.shape[-1]
    half = xs_ref.shape[1]

    def weight_copies(e, slot):
        return (pltpu.make_async_copy(wg_hbm.at[layer, e], wg_f.at[slot], sem.at[slot]),
                pltpu.make_async_copy(wu_hbm.at[layer, e], wu_f.at[slot], sem.at[slot]),
                pltpu.make_async_copy(wd_hbm.at[layer, e], wd_f.at[slot], sem.at[slot]))

    @pl.when(i == 0)
    def _prime():
        for cp in weight_copies(seq_e[0], 0):
            cp.start()

    @pl.when(i < nvalid[0])
    def _():
        @pl.when(blk_first[i] == 1)
        def _next_expert():
            k = blk_ord[i]
            slot = lax.rem(k, 2)

            @pl.when(k + 1 < npresent[0])
            def _prefetch():
                for cp in weight_copies(seq_e[k + 1], 1 - slot):
                    cp.start()

            for cp in weight_copies(blk_e[i], slot):
                cp.wait()
            wgu_s[:, 0:f] = wg_f[slot].astype(BF16)
            wgu_s[:, f:2 * f] = wu_f[slot].astype(BF16)
            wd_s[...] = wd_f[slot].astype(BF16)

        row = lax.broadcasted_iota(jnp.int32, xs_ref.shape, 0)
        x_lo, x_hi = _unpack_bf16_pair(jnp.where(row < blk_rows[i], xs_ref[...], 0))
        gu = _dot(x_lo.astype(BF16), wgu_s[0:half, :]) + _dot(x_hi.astype(BF16), wgu_s[half:2 * half, :])
        hid = _silu(gu[:, 0:f]) * gu[:, f:2 * f]
        out = _dot(hid.astype(BF16), wd_s[...])
        o_ref[...] = _pack_bf16_pair(out[:, 0:half], out[:, half:])

    @pl.when(i >= nvalid[0])
    def _():
        o_ref[...] = jnp.zeros(o_ref.shape, o_ref.dtype)


def _experts(xs, layout, w_gate, w_up, w_down, layer):
    p, half = xs.shape
    d = 2 * half
    f = w_gate.shape[-1]
    bm = MOE_BLOCK
    row = lambda i, be, br, nv, *_: (jnp.minimum(i, nv[0] - 1), 0)
    grid_spec = pltpu.PrefetchScalarGridSpec(
        num_scalar_prefetch=len(layout),
        grid=(p // bm,),
        in_specs=[
            pl.BlockSpec((bm, half), row),
            pl.BlockSpec(memory_space=pl.ANY),
            pl.BlockSpec(memory_space=pl.ANY),
            pl.BlockSpec(memory_space=pl.ANY),
        ],
        out_specs=pl.BlockSpec((bm, half), lambda i, *_: (i, 0)),
        scratch_shapes=[
            pltpu.VMEM((2, d, f), F32), pltpu.VMEM((2, d, f), F32), pltpu.VMEM((2, f, d), F32),
            pltpu.VMEM((d, 2 * f), BF16), pltpu.VMEM((f, d), BF16),
            pltpu.SemaphoreType.DMA((2,)),
        ],
    )
    return pl.pallas_call(
        functools.partial(_experts_kernel, layer=layer),
        out_shape=jax.ShapeDtypeStruct((p, half), jnp.int32),
        grid_spec=grid_spec,
        compiler_params=_params(("arbitrary",)),
        name="routed_experts",
    )(*layout, xs, w_gate, w_up, w_down)


def _shared_ln_kernel(dest_ref, x_ref, xb_ref, wt_ref, ys_hbm, wgu_ref, wd_ref, g_ref, b_ref, o_ref, ob_ref,
                      buf, sem, *, alpha):
    f = wd_ref.shape[0]
    tm = x_ref.shape[0]

    def issue(r, carry):
        for j in range(TOP_K):
            pltpu.make_async_copy(ys_hbm.at[pl.ds(dest_ref[j, r], 1), :], buf.at[j, pl.ds(r, 1), :],
                                  sem).start(priority=j % 2)
        return carry

    lax.fori_loop(0, tm, issue, 0)
    gu = _dot(xb_ref[...], wgu_ref[...])
    hid = _silu(gu[:, 0:f]) * gu[:, f:2 * f]
    ffn = _dot(hid.astype(BF16), wd_ref[...])
    for j in range(TOP_K):
        pltpu.make_async_copy(ys_hbm.at[pl.ds(0, tm), :], buf.at[j], sem).wait()
    half = buf.shape[2]
    r_lo = jnp.zeros((tm, half), F32)
    r_hi = jnp.zeros((tm, half), F32)
    for j in range(TOP_K):
        y_lo, y_hi = _unpack_bf16_pair(buf[j])
        wj = wt_ref[:, j:j + 1]
        r_lo = r_lo + wj * y_lo
        r_hi = r_hi + wj * y_hi
    ffn = ffn + jnp.concatenate([r_lo, r_hi], axis=1)
    y = _layernorm(alpha * x_ref[...] + ffn, g_ref[...], b_ref[...])
    o_ref[...] = y
    ob_ref[...] = y.astype(BF16)


def _shared_ln(dest, x, xb, wts_t, ys, wgu, wd, g, b, alpha, tm):
    t, d = x.shape
    f = wd.shape[0]
    const = lambda i: (0, 0)
    tok = lambda i: (i, 0)
    return pl.pallas_call(
        functools.partial(_shared_ln_kernel, alpha=alpha),
        out_shape=(jax.ShapeDtypeStruct((t, d), F32), jax.ShapeDtypeStruct((t, d), BF16)),
        grid=(t // tm,),
        in_specs=[
            pl.BlockSpec((TOP_K, tm), lambda i: (0, i), memory_space=pltpu.SMEM),
            pl.BlockSpec((tm, d), tok),
            pl.BlockSpec((tm, d), tok),
            pl.BlockSpec((tm, TOP_K), tok),
            pl.BlockSpec(memory_space=pl.ANY),
            pl.BlockSpec((d, 2 * f), const),
            pl.BlockSpec((f, d), const),
            pl.BlockSpec((1, d), const),
            pl.BlockSpec((1, d), const),
        ],
        out_specs=(pl.BlockSpec((tm, d), tok), pl.BlockSpec((tm, d), tok)),
        scratch_shapes=[pltpu.VMEM((TOP_K, tm, d // 2), jnp.int32), pltpu.SemaphoreType.DMA(())],
        compiler_params=_params(("arbitrary",)),
        name="shared_expert_combine_layernorm",
    )(dest, x, xb, wts_t, ys, wgu, wd, g.reshape(1, d), b.reshape(1, d))


def _moe_layout(cnt, t):
    counts = cnt[:, 0].astype(jnp.int32)
    padded = (counts + MOE_BLOCK - 1) // MOE_BLOCK * MOE_BLOCK
    pend = jnp.cumsum(padded)
    pstart = pend - padded
    nblk = (t * TOP_K + N_EXPERTS * MOE_BLOCK) // MOE_BLOCK
    nvalid = pend[-1] // MOE_BLOCK
    experts = jnp.arange(N_EXPERTS, dtype=jnp.int32)
    blocks = jnp.arange(nblk, dtype=jnp.int32)
    starts = jnp.minimum(blocks, nvalid - 1) * MOE_BLOCK
    blk_e = jnp.sum((pend[None, :] <= starts[:, None]).astype(jnp.int32), axis=1)
    onehot = blk_e[:, None] == experts[None, :]
    used_end = jnp.sum(jnp.where(onehot, (pstart + counts)[None, :], 0), axis=1)
    blk_rows = jnp.clip(used_end - starts, 0, MOE_BLOCK)
    present = counts > 0
    ord_e = jnp.cumsum(present.astype(jnp.int32)) - 1
    blk_ord = jnp.sum(jnp.where(onehot, ord_e[None, :], 0), axis=1)
    seq_e = jnp.sum(jnp.where(present[None, :] & (ord_e[None, :] == experts[:, None]), experts[None, :], 0), axis=1)
    prev_e = jnp.concatenate([jnp.full((1,), -1, jnp.int32), blk_e[:-1]])
    blk_first = ((blk_e != prev_e) & (blocks < nvalid)).astype(jnp.int32)
    i32 = lambda v: v.astype(jnp.int32)
    layout = (i32(blk_e), i32(blk_rows), i32(nvalid).reshape(1), blk_first, i32(blk_ord), i32(seq_e),
              i32(jnp.sum(present)).reshape(1))
    return i32(pstart), layout, nblk * MOE_BLOCK


def kernel(x, w_in, da_lambda, da_norm_g, gla_gate_w2, gla_gate_b, gla_norm_g, ssd_conv_w, ssd_conv_b,
           ssd_dt_bias, ssd_a_log, ssd_d, ssd_norm_g, w_branch, w_out, ln1_g, ln1_b, router_w, router_bias,
           exp_w_gate, exp_w_up, exp_w_down, sh_w_gate, sh_w_up, sh_w_down, ln2_g, ln2_b):
    batch, seq, d = x.shape
    depth = w_in.shape[0]
    t = batch * seq
    alpha = (2 * depth) ** 0.25

    da_w = DA_HEADS * DA_V_DIM
    gk_w = GLA_HEADS * GLA_DK
    gv_w = GLA_HEADS * GLA_DV
    conv_dim = SSD_WIDTH + 2 * SSD_GROUPS * SSD_STATE
    sizes = (da_w, da_w, da_w, gk_w, gk_w, gv_w, GLA_GATE_RANK, gv_w, SSD_WIDTH, conv_dim, SSD_HEADS,
             N_BRANCH * d)
    offs = np.concatenate([[0], np.cumsum(sizes)]).tolist()
    (o_daq, o_dak, o_dav, o_gq, o_gk, o_gv, o_glr, o_gr, o_sz, o_sx, o_sdt, o_mg, o_end) = offs

    ones = lambda n: jnp.ones((1, n), F32)
    qk_scale = jnp.concatenate([jnp.full((1, da_w), DA_HEAD_DIM ** -0.5, F32), ones(da_w)], axis=1)

    xf = x.reshape(t, d)
    xb = xf.astype(BF16)
    for l in range(depth):
        wl = w_in[l]
        w_all = jnp.concatenate([
            wl[:, o_mg:o_end], wl[:, o_daq:o_dav], wl[:, o_gq:o_glr], wl[:, o_gr:o_sz], wl[:, o_sz:o_sdt],
            jnp.pad(wl[:, o_glr:o_gr], ((0, 0), (0, LANES - GLA_GATE_RANK))),
            jnp.pad(wl[:, o_sdt:o_mg], ((0, 0), (0, LANES - SSD_HEADS)))], axis=1).astype(BF16)
        c_mg = 0
        c_qk = c_mg + N_BRANCH * d
        c_gla = c_qk + 2 * da_w
        c_ssd = c_gla + 2 * gk_w + 2 * gv_w
        c_small = c_ssd + SSD_WIDTH + conv_dim
        w_vt = wl[:, o_dav:o_gq].T.astype(BF16)

        qk = _matmul(xb, w_all, c_qk, qk_scale, BF16, 512, 1024, "proj_attn_qk")
        vt = _matmul_nt(w_vt, xb, BF16, 512, "proj_attn_vt")
        pg = _matmul(xb, w_all, c_gla, ones(2 * gk_w + 2 * gv_w), F32, 512, 1024, "proj_gla")
        ps = _matmul(xb, w_all, c_ssd, ones(SSD_WIDTH + conv_dim), F32, 512, 1024, "proj_ssd")
        small = _matmul(xb, w_all, c_small, ones(2 * LANES), F32, 512, 2 * LANES, "proj_small")

        br_a = _diff_attention(qk, vt, da_lambda[l], da_norm_g[l], l, batch, seq)
        w2p = jnp.pad(gla_gate_w2[l], ((0, LANES - GLA_GATE_RANK), (0, 0))).astype(BF16)
        br_b = _gla(pg, small, w2p, gla_gate_b[l].reshape(1, gk_w), gla_norm_g[l], batch, seq)
        br_c = _ssd(ps, small, ssd_conv_w[l], ssd_conv_b[l], ssd_dt_bias[l], ssd_a_log[l], ssd_d[l],
                    ssd_norm_g[l], batch, seq)

        merged = _merge(xb, w_all, br_a, br_b, br_c, w_branch[l].astype(BF16), 512, 512)
        xf, xb, xp = _outproj_ln(merged, w_out[l].astype(BF16), xf, ln1_g[l], ln1_b[l], alpha, 2 * TOKEN_TILE)

        eidx, wts, pos, cnt = _router(xf, router_w[l], router_bias[l], TOKEN_TILE)
        pstart, layout, n_rows = _moe_layout(cnt, t)
        dest = _slots(pstart, eidx, pos, min(t, 2048))
        xs = _dispatch(dest, xp, n_rows, TOKEN_TILE)
        ys = _experts(xs, layout, exp_w_gate, exp_w_up, exp_w_down, l)
        wgu = jnp.concatenate([sh_w_gate[l], sh_w_up[l]], axis=1).astype(BF16)
        xf, xb = _shared_ln(dest, xf, xb, wts.T, ys, wgu, sh_w_down[l].astype(BF16), ln2_g[l], ln2_b[l], alpha,
                            TOKEN_TILE)
    return xf.reshape(batch, seq, d)
```

```python
import functools
import math

import jax
import jax.numpy as jnp
import numpy as np
from jax import lax
from jax.experimental import pallas as pl
from jax.experimental.pallas import tpu as pltpu

F32 = jnp.float32
BF16 = jnp.bfloat16

DA_HEADS = 4
DA_HEAD_DIM = 128
DA_V_DIM = 2 * DA_HEAD_DIM
GLA_HEADS = 4
GLA_DK = 128
GLA_DV = 256
GLA_GATE_RANK = 16
GLA_TAU = 16.0
GLA_CHUNK = 64
SSD_HEADS = 16
SSD_HEAD_DIM = 64
SSD_STATE = 128
SSD_GROUPS = 4
SSD_CONV = 4
SSD_CHUNK = 128
SSD_WIDTH = SSD_HEADS * SSD_HEAD_DIM
N_BRANCH = 3
BRANCH_WIDTH = 1024
N_EXPERTS = 64
TOP_K = 8
N_EXPERT_GROUPS = 8
TOPK_GROUPS = 4
ROUTED_SCALE = 2.5
EPS = 1e-5

LANES = 128
SUBLANES = 8
VMEM_LIMIT = 52 * 1024 * 1024

ATTN_BLOCK = 512
ATTN_HEADS_PER_STEP = 4
SSD_CHUNKS_PER_STEP = 4
GLA_BLOCK = 512
MOE_BLOCK = 512
TOKEN_TILE = 256


def _params(sem):
    return pltpu.CompilerParams(dimension_semantics=sem, vmem_limit_bytes=VMEM_LIMIT)


def _silu(x):
    return x * (1.0 / (1.0 + jnp.exp(-x)))


def _sigmoid(x):
    return 1.0 / (1.0 + jnp.exp(-x))


def _split3(x):
    h1 = x.astype(BF16)
    r1 = x - h1.astype(F32)
    h2 = r1.astype(BF16)
    r2 = r1 - h2.astype(F32)
    return h1, h2, r2.astype(BF16)


HI_HALF = -65536


def _pack_bf16_pair(lo, hi):
    lo_bits = lax.bitcast_convert_type(lo.astype(BF16).astype(F32), jnp.int32)
    hi_bits = lax.bitcast_convert_type(hi.astype(BF16).astype(F32), jnp.int32)
    return lax.shift_right_logical(lo_bits, 16) | (hi_bits & HI_HALF)


def _unpack_bf16_pair(w):
    lo = lax.bitcast_convert_type(lax.shift_left(w, 16), F32)
    hi = lax.bitcast_convert_type(w & HI_HALF, F32)
    return lo, hi


def _dot(a, b):
    return jnp.dot(a, b, preferred_element_type=F32)


def _dot_nt(a, b):
    return lax.dot_general(a, b, (((1,), (1,)), ((), ())), preferred_element_type=F32)


def _dot_tn(a, b):
    return lax.dot_general(a, b, (((0,), (0,)), ((), ())), preferred_element_type=F32)


def _mm_kernel(a_ref, w_ref, s_ref, o_ref):
    acc = _dot(a_ref[...], w_ref[...])
    o_ref[...] = (acc * s_ref[...]).astype(o_ref.dtype)


def _matmul(a, w, col_scale, out_dtype, tm, tn, name):
    m, k = a.shape
    n = w.shape[1]
    return pl.pallas_call(
        _mm_kernel,
        out_shape=jax.ShapeDtypeStruct((m, n), out_dtype),
        grid=(n // tn, m // tm),
        in_specs=[
            pl.BlockSpec((tm, k), lambda j, i: (i, 0)),
            pl.BlockSpec((k, tn), lambda j, i: (0, j)),
            pl.BlockSpec((1, tn), lambda j, i: (0, j)),
        ],
        out_specs=pl.BlockSpec((tm, tn), lambda j, i: (i, j)),
        compiler_params=_params(("parallel", "arbitrary")),
        name=name,
    )(a, w, col_scale)


def _mm_nt_kernel(wt_ref, a_ref, o_ref):
    o_ref[...] = _dot_nt(wt_ref[...], a_ref[...]).astype(o_ref.dtype)


def _matmul_nt(wt, a, out_dtype, tm, name):
    n, k = wt.shape
    m = a.shape[0]
    return pl.pallas_call(
        _mm_nt_kernel,
        out_shape=jax.ShapeDtypeStruct((n, m), out_dtype),
        grid=(m // tm,),
        in_specs=[
            pl.BlockSpec((n, k), lambda i: (0, 0)),
            pl.BlockSpec((tm, k), lambda i: (i, 0)),
        ],
        out_specs=pl.BlockSpec((n, tm), lambda i: (0, i)),
        compiler_params=_params(("parallel",)),
        name=name,
    )(wt, a)


ATTN_KEYS_FULL, ATTN_KEYS_DIAG_LAST, ATTN_KEYS_DIAG_FIRST = 0, 1, 2


def _attn_kernel(qi_tab, kj_tab, kind_tab, q_ref, k_ref, vt_ref, koff_ref, slope_ref, lam_ref, g_ref, o_ref,
                 m_ref, l_ref, acc_ref, *, blk, lam_init):
    p = pl.program_id(2)
    qi = qi_tab[p]
    kj = kj_tab[p]
    kind = kind_tab[p]

    @pl.when(kj == 0)
    def _init():
        m_ref[...] = jnp.full(m_ref.shape, -jnp.inf, F32)
        l_ref[...] = jnp.zeros(l_ref.shape, F32)
        acc_ref[...] = jnp.zeros(acc_ref.shape, F32)

    rel = jnp.full((1, blk), (2 * kj - qi) * blk, jnp.int32).astype(F32)

    def step(kind_static):
        nk = blk if kind_static == ATTN_KEYS_DIAG_FIRST else 2 * blk
        if kind_static != ATTN_KEYS_FULL:
            kc = lax.broadcasted_iota(jnp.int32, (nk, blk), 0)
            qr = lax.broadcasted_iota(jnp.int32, (nk, blk), 1)
            keep = kc <= (qr + blk if kind_static == ATTN_KEYS_DIAG_LAST else qr)
        koff = koff_ref[0:nk, :]
        for hd in range(ATTN_HEADS_PER_STEP):
            c0 = slope_ref[hd] * rel
            vt = vt_ref[hd * DA_V_DIM:(hd + 1) * DA_V_DIM, 0:nk]
            for mp in range(2):
                sl = slice(hd * DA_V_DIM + mp * DA_HEAD_DIM, hd * DA_V_DIM + (mp + 1) * DA_HEAD_DIM)
                si = 2 * hd + mp
                s = _dot_nt(k_ref[0:nk, sl], q_ref[:, sl]) + slope_ref[hd] * koff
                if kind_static != ATTN_KEYS_FULL:
                    s = jnp.where(keep, s, -jnp.inf)
                m_old = m_ref[si]
                m_new = jnp.maximum(m_old, jnp.max(s, axis=0, keepdims=True) + c0)
                alpha = jnp.exp(m_old - m_new)
                pr = jnp.exp(s - (m_new - c0))
                l_ref[si] = alpha * l_ref[si] + jnp.sum(pr, axis=0, keepdims=True)
                acc_ref[si] = alpha * acc_ref[si] + _dot(vt, pr.astype(BF16))
                m_ref[si] = m_new

    def finish():
        lp = lam_ref[...]
        lam = (jnp.exp(jnp.sum(lp[0:1] * lp[1:2], axis=1, keepdims=True))
               - jnp.exp(jnp.sum(lp[2:3] * lp[3:4], axis=1, keepdims=True)) + lam_init)
        for hd in range(ATTN_HEADS_PER_STEP):
            ot = acc_ref[2 * hd] / l_ref[2 * hd] - lam * (acc_ref[2 * hd + 1] / l_ref[2 * hd + 1])
            o = ot.T
            o = o * lax.rsqrt(jnp.mean(o * o, axis=1, keepdims=True) + EPS)
            o_ref[:, hd * DA_V_DIM:(hd + 1) * DA_V_DIM] = (o * g_ref[...] * (1.0 - lam_init)).astype(o_ref.dtype)

    @pl.when(kind == ATTN_KEYS_FULL)
    def _full():
        step(ATTN_KEYS_FULL)

    @pl.when(kind == ATTN_KEYS_DIAG_LAST)
    def _diag_last():
        step(ATTN_KEYS_DIAG_LAST)
        finish()

    @pl.when(kind == ATTN_KEYS_DIAG_FIRST)
    def _diag_first():
        step(ATTN_KEYS_DIAG_FIRST)
        finish()


def _diff_attention(qk, vt, lam_params, norm_g, layer_idx, batch, seq):
    blk = min(ATTN_BLOCK, seq // 2)
    nq = seq // blk
    nkp = nq // 2
    steps = []
    for i in range(nq):
        for j in range(i // 2 + 1):
            last = j == i // 2
            kind = ATTN_KEYS_FULL if not last else (ATTN_KEYS_DIAG_LAST if i % 2 else ATTN_KEYS_DIAG_FIRST)
            steps.append((i, j, kind))
    qi_tab, kj_tab, kind_tab = (jnp.asarray([st[c] for st in steps], jnp.int32) for c in range(3))
    lam_init = 0.8 - 0.6 * math.exp(-0.3 * layer_idx)
    h = DA_HEADS
    slopes = np.asarray([2.0 ** (-8.0 * (i + 1) / h) for i in range(h)], np.float32)
    key_off = jnp.asarray(np.broadcast_to(np.arange(2 * blk, dtype=np.float32)[:, None], (2 * blk, blk)))
    slopes = jnp.asarray(np.broadcast_to(slopes[:, None, None], (h, 1, blk)))
    t = batch * seq
    hs = ATTN_HEADS_PER_STEP
    hg = h // hs
    kern = functools.partial(_attn_kernel, blk=blk, lam_init=lam_init)
    grid_spec = pltpu.PrefetchScalarGridSpec(
        num_scalar_prefetch=3,
        grid=(batch, hg, len(steps)),
        in_specs=[
            pl.BlockSpec((blk, hs * DA_V_DIM), lambda b, hh, p, qt, kt, kd: (b * nq + qt[p], hh)),
            pl.BlockSpec((2 * blk, hs * DA_V_DIM), lambda b, hh, p, qt, kt, kd: (b * nkp + kt[p], hg + hh)),
            pl.BlockSpec((hs * DA_V_DIM, 2 * blk), lambda b, hh, p, qt, kt, kd: (hh, b * nkp + kt[p])),
            pl.BlockSpec((2 * blk, blk), lambda b, hh, p, qt, kt, kd: (0, 0)),
            pl.BlockSpec((hs, 1, blk), lambda b, hh, p, qt, kt, kd: (hh, 0, 0)),
            pl.BlockSpec((4, DA_HEAD_DIM), lambda b, hh, p, qt, kt, kd: (0, 0)),
            pl.BlockSpec((1, DA_V_DIM), lambda b, hh, p, qt, kt, kd: (0, 0)),
        ],
        out_specs=pl.BlockSpec((blk, hs * DA_V_DIM), lambda b, hh, p, qt, kt, kd: (b * nq + qt[p], hh)),
        scratch_shapes=[
            pltpu.VMEM((2 * hs, 1, blk), F32),
            pltpu.VMEM((2 * hs, 1, blk), F32),
            pltpu.VMEM((2 * hs, DA_V_DIM, blk), F32),
        ],
    )
    return pl.pallas_call(
        kern,
        out_shape=jax.ShapeDtypeStruct((t, h * DA_V_DIM), BF16),
        grid_spec=grid_spec,
        compiler_params=_params(("parallel", "parallel", "arbitrary")),
        name="diff_attention",
    )(qi_tab, kj_tab, kind_tab, qk, qk, vt, key_off, slopes, lam_params, norm_g.reshape(1, DA_V_DIM))


def _gla_kernel(pg_ref, sm_ref, w2_ref, b2_ref, g_ref, o_ref, st_ref, *, blk):
    n = pl.program_id(1)
    hk = GLA_HEADS * GLA_DK
    hv = GLA_HEADS * GLA_DV
    c_len = GLA_CHUNK

    @pl.when(n == 0)
    def _init():
        st_ref[...] = jnp.zeros(st_ref.shape, F32)

    lr = sm_ref[:, 0:LANES].astype(BF16)
    gl = _dot(lr, w2_ref[...]) + b2_ref[...]
    gk = (jnp.minimum(gl, 0.0) - jnp.log(1.0 + jnp.exp(-jnp.abs(gl)))) * (1.0 / GLA_TAU)
    r = lax.broadcasted_iota(jnp.int32, (blk, blk), 0)
    c = lax.broadcasted_iota(jnp.int32, (blk, blk), 1)
    tri = jnp.where((c <= r) & ((r // c_len) == (c // c_len)), 1.0, 0.0).astype(BF16)
    g1, g2, g3 = _split3(gk)
    bcum = _dot(tri, g1) + _dot(tri, g2) + _dot(tri, g3)

    rr = lax.broadcasted_iota(jnp.int32, (c_len, c_len), 0)
    cc = lax.broadcasted_iota(jnp.int32, (c_len, c_len), 1)
    causal = cc <= rr
    scale = GLA_DK ** -0.5
    for ci in range(blk // c_len):
        rows = slice(ci * c_len, (ci + 1) * c_len)
        for h in range(GLA_HEADS):
            kc = slice(h * GLA_DK, (h + 1) * GLA_DK)
            vc = slice(h * GLA_DV, (h + 1) * GLA_DV)
            b = bcum[rows, kc]
            b_last = b[c_len - 1:c_len, :]
            q = pg_ref[rows, kc]
            k = pg_ref[rows, hk + h * GLA_DK: hk + (h + 1) * GLA_DK]
            v = pg_ref[rows, 2 * hk + h * GLA_DV: 2 * hk + (h + 1) * GLA_DV].astype(BF16)
            rg = pg_ref[rows, 2 * hk + hv + h * GLA_DV: 2 * hk + hv + (h + 1) * GLA_DV]
            q_e = (q * scale * jnp.exp(b)).astype(BF16)
            k_e = (k * jnp.exp(-b)).astype(BF16)
            k_d = (k * jnp.exp(b_last - b)).astype(BF16)
            att = jnp.where(causal, _dot_nt(q_e, k_e), 0.0)
            st = st_ref[h]
            o = _dot(att.astype(BF16), v) + _dot_nt(q_e, st.astype(BF16))
            st_ref[h] = st * jnp.exp(b_last) + _dot_tn(v, k_d)
            o = o * lax.rsqrt(jnp.mean(o * o, axis=1, keepdims=True) + EPS) * g_ref[...]
            o_ref[rows, vc] = (o * _silu(rg)).astype(o_ref.dtype)


def _gla(pg, small, w2p, b2, norm_g, batch, seq):
    blk = min(GLA_BLOCK, seq)
    nb = seq // blk
    t = batch * seq
    hk = GLA_HEADS * GLA_DK
    hv = GLA_HEADS * GLA_DV
    width = pg.shape[1]
    return pl.pallas_call(
        functools.partial(_gla_kernel, blk=blk),
        out_shape=jax.ShapeDtypeStruct((t, hv), BF16),
        grid=(batch, nb),
        in_specs=[
            pl.BlockSpec((blk, width), lambda b, n: (b * nb + n, 0)),
            pl.BlockSpec((blk, small.shape[1]), lambda b, n: (b * nb + n, 0)),
            pl.BlockSpec((LANES, hk), lambda b, n: (0, 0)),
            pl.BlockSpec((1, hk), lambda b, n: (0, 0)),
            pl.BlockSpec((1, GLA_DV), lambda b, n: (0, 0)),
        ],
        out_specs=pl.BlockSpec((blk, hv), lambda b, n: (b * nb + n, 0)),
        scratch_shapes=[pltpu.VMEM((GLA_HEADS, GLA_DV, GLA_DK), F32)],
        compiler_params=_params(("parallel", "arbitrary")),
        name="gla",
    )(pg, small, w2p, b2, norm_g.reshape(1, GLA_DV))


def _ssd_kernel(ps_ref, sm_ref, cw_ref, cb_ref, dtb_ref, alog_ref, dsk_ref, g_ref, o_ref,
                tail_ref, st_ref):
    n = pl.program_id(1)
    q_len = SSD_CHUNK
    w = SSD_WIDTH
    gs = SSD_GROUPS * SSD_STATE
    heads_per_group = SSD_HEADS // SSD_GROUPS
    gw = heads_per_group * SSD_HEAD_DIM

    @pl.when(n == 0)
    def _init():
        tail_ref[...] = jnp.zeros(tail_ref.shape, F32)
        st_ref[...] = jnp.zeros(st_ref.shape, F32)

    r = lax.broadcasted_iota(jnp.int32, (q_len, q_len), 0)
    c = lax.broadcasted_iota(jnp.int32, (q_len, q_len), 1)
    causal = c <= r
    tri = jnp.where(causal, 1.0, 0.0).astype(BF16)

    for ci in range(SSD_CHUNKS_PER_STEP):
        rows = slice(ci * q_len, (ci + 1) * q_len)
        cur = ps_ref[rows, w:w + w + 2 * gs]
        ext = jnp.concatenate([tail_ref[...], cur], axis=0)
        acc = cb_ref[...] + cw_ref[0:1, :] * ext[SUBLANES - 3:SUBLANES - 3 + q_len]
        for i in range(1, SSD_CONV):
            off = SUBLANES - (SSD_CONV - 1) + i
            acc = acc + cw_ref[i:i + 1, :] * ext[off:off + q_len]
        tail_ref[...] = cur[q_len - SUBLANES:q_len]
        xbc = _silu(acc)
        xs = xbc[:, 0:w]
        bm = xbc[:, w:w + gs]
        cm = xbc[:, w + gs:w + 2 * gs]

        dtr = sm_ref[rows, LANES:2 * LANES] + dtb_ref[...]
        dtv = jnp.maximum(dtr, 0.0) + jnp.log(1.0 + jnp.exp(-jnp.abs(dtr)))
        da = dtv * (-jnp.exp(alog_ref[...]))
        d1, d2, d3 = _split3(da)
        a_cs = _dot(tri, d1) + _dot(tri, d2) + _dot(tri, d3)
        a_cs_t = a_cs.T
        a_last = a_cs[q_len - 1:q_len, :]
        e_last = jnp.exp(a_last)

        for g in range(SSD_GROUPS):
            bm_g = bm[:, g * SSD_STATE:(g + 1) * SSD_STATE]
            cm_g = cm[:, g * SSD_STATE:(g + 1) * SSD_STATE]
            bm_b = bm_g.astype(BF16)
            cm_b = cm_g.astype(BF16)
            cb = _dot_nt(cm_b, bm_b)
            y_diag = []
            xdd = []
            e_col = []
            e_row = []
            for rh in range(heads_per_group):
                h = g * heads_per_group + rh
                col = a_cs[:, h:h + 1]
                row = a_cs_t[h:h + 1, :]
                lm = jnp.exp(jnp.where(causal, col - row, -jnp.inf))
                xdt = xs[:, h * SSD_HEAD_DIM:(h + 1) * SSD_HEAD_DIM] * dtv[:, h:h + 1]
                y_diag.append(_dot((cb * lm).astype(BF16), xdt.astype(BF16)))
                xdd.append(xdt * jnp.exp(a_last[:, h:h + 1] - col))
                e_col.append(jnp.broadcast_to(jnp.exp(col), (q_len, SSD_HEAD_DIM)))
                e_row.append(jnp.broadcast_to(e_last[:, h:h + 1], (1, SSD_HEAD_DIM)))
            y_diag = jnp.concatenate(y_diag, axis=1)
            xdd = jnp.concatenate(xdd, axis=1)
            e_col = jnp.concatenate(e_col, axis=1)
            e_row = jnp.concatenate(e_row, axis=1)
            s_prev = st_ref[g]
            y_off = _dot(cm_b, s_prev.astype(BF16)) * e_col
            st_ref[g] = s_prev * e_row + _dot_tn(bm_b, xdd.astype(BF16))
            lanes = slice(g * gw, (g + 1) * gw)
            y = y_diag + y_off + xs[:, lanes] * dsk_ref[:, lanes]
            y = y * _silu(ps_ref[rows, lanes])
            y = y * lax.rsqrt(jnp.mean(y * y, axis=1, keepdims=True) + EPS) * g_ref[:, lanes]
            o_ref[rows, lanes] = y.astype(o_ref.dtype)


def _ssd(ps, small, conv_w, conv_b, dt_bias, a_log, d_skip, norm_g, batch, seq):
    q_len = SSD_CHUNK * SSD_CHUNKS_PER_STEP
    nb = seq // q_len
    t = batch * seq
    conv_dim = conv_w.shape[1]
    pad = LANES - SSD_HEADS
    dtb = jnp.pad(dt_bias, (0, pad)).reshape(1, LANES)
    alog = jnp.pad(a_log, (0, pad)).reshape(1, LANES)
    dsk = jnp.repeat(d_skip, SSD_HEAD_DIM).reshape(1, SSD_WIDTH)
    const = lambda b, n: (0, 0)
    return pl.pallas_call(
        _ssd_kernel,
        out_shape=jax.ShapeDtypeStruct((t, SSD_WIDTH), BF16),
        grid=(batch, nb),
        in_specs=[
            pl.BlockSpec((q_len, ps.shape[1]), lambda b, n: (b * nb + n, 0)),
            pl.BlockSpec((q_len, small.shape[1]), lambda b, n: (b * nb + n, 0)),
            pl.BlockSpec((SSD_CONV, conv_dim), const),
            pl.BlockSpec((1, conv_dim), const),
            pl.BlockSpec((1, LANES), const),
            pl.BlockSpec((1, LANES), const),
            pl.BlockSpec((1, SSD_WIDTH), const),
            pl.BlockSpec((1, SSD_WIDTH), const),
        ],
        out_specs=pl.BlockSpec((q_len, SSD_WIDTH), lambda b, n: (b * nb + n, 0)),
        scratch_shapes=[
            pltpu.VMEM((SUBLANES, conv_dim), F32),
            pltpu.VMEM((SSD_GROUPS, SSD_STATE, SSD_WIDTH // SSD_GROUPS), F32),
        ],
        compiler_params=_params(("parallel", "arbitrary")),
        name="ssd",
    )(ps, small, conv_w, conv_b.reshape(1, conv_dim), dtb, alog, dsk, norm_g.reshape(1, SSD_WIDTH))


def _merge_kernel(x_ref, wga_ref, wgb_ref, wgc_ref, ba_ref, bb_ref, bc_ref, wb_ref, o_ref):
    x = x_ref[...]
    acc = None
    for i, (wg, br) in enumerate(((wga_ref, ba_ref), (wgb_ref, bb_ref), (wgc_ref, bc_ref))):
        gate = _sigmoid(_dot(x, wg[...]))
        term = gate * _dot(br[...], wb_ref[i])
        acc = term if acc is None else acc + term
    o_ref[...] = acc.astype(o_ref.dtype)


def _merge(xb, wgate, br_a, br_b, br_c, wbr, tm, tn):
    t, d = xb.shape
    bw = br_a.shape[1]
    nj = d // tn
    gate_spec = lambda i_br: pl.BlockSpec((d, tn), lambda i, j: (0, i_br * nj + j))
    return pl.pallas_call(
        _merge_kernel,
        out_shape=jax.ShapeDtypeStruct((t, d), BF16),
        grid=(t // tm, nj),
        in_specs=[
            pl.BlockSpec((tm, d), lambda i, j: (i, 0)),
            gate_spec(0), gate_spec(1), gate_spec(2),
            pl.BlockSpec((tm, bw), lambda i, j: (i, 0)),
            pl.BlockSpec((tm, bw), lambda i, j: (i, 0)),
            pl.BlockSpec((tm, bw), lambda i, j: (i, 0)),
            pl.BlockSpec((N_BRANCH, bw, tn), lambda i, j: (0, 0, j)),
        ],
        out_specs=pl.BlockSpec((tm, tn), lambda i, j: (i, j)),
        compiler_params=_params(("parallel", "arbitrary")),
        name="gated_merge",
    )(xb, wgate, wgate, wgate, br_a, br_b, br_c, wbr)


def _layernorm(v, g, b):
    mu = jnp.mean(v, axis=1, keepdims=True)
    d = v - mu
    var = jnp.mean(d * d, axis=1, keepdims=True)
    return d * lax.rsqrt(var + EPS) * g + b


def _outproj_ln_kernel(m_ref, w_ref, x_ref, g_ref, b_ref, o_ref, op_ref, *, alpha):
    mix = _dot(m_ref[...], w_ref[...])
    y = _layernorm(alpha * x_ref[...] + mix, g_ref[...], b_ref[...])
    half = y.shape[1] // 2
    o_ref[...] = y
    op_ref[...] = _pack_bf16_pair(y[:, 0:half], y[:, half:])


def _outproj_ln(merged, w_out, x, g, b, alpha, tm):
    t, d = x.shape
    const = lambda i: (0, 0)
    tok = lambda i: (i, 0)
    return pl.pallas_call(
        functools.partial(_outproj_ln_kernel, alpha=alpha),
        out_shape=(jax.ShapeDtypeStruct((t, d), F32), jax.ShapeDtypeStruct((t, d // 2), jnp.int32)),
        grid=(t // tm,),
        in_specs=[
            pl.BlockSpec((tm, d), tok),
            pl.BlockSpec((d, d), const, pipeline_mode=pl.Buffered(1)),
            pl.BlockSpec((tm, d), tok),
            pl.BlockSpec((1, d), const),
            pl.BlockSpec((1, d), const),
        ],
        out_specs=(pl.BlockSpec((tm, d), tok), pl.BlockSpec((tm, d // 2), tok)),
        compiler_params=_params(("parallel",)),
        name="outproj_layernorm",
    )(merged, w_out, x, g.reshape(1, d), b.reshape(1, d))


def _router_kernel(x_ref, w_ref, bias_ref, eidx_ref, wt_ref, pos_ref, cnt_ref, carry_ref, *, tm):
    i = pl.program_id(0)
    ne = N_EXPERTS
    ng = N_EXPERT_GROUPS
    per = ne // ng

    @pl.when(i == 0)
    def _init():
        carry_ref[...] = jnp.zeros(carry_ref.shape, F32)

    x = x_ref[...]
    xh = x.astype(BF16)
    xl = (x - xh.astype(F32)).astype(BF16)
    w = w_ref[...]
    wh = w.astype(BF16)
    wl = (w - wh.astype(F32)).astype(BF16)
    logits = _dot_nt(wh, xh) + _dot_nt(wh, xl) + _dot_nt(wl, xh)
    scores = _sigmoid(logits)
    sel = scores + bias_ref[...]

    grp = sel.reshape(ng, per, tm)
    io_p = lax.broadcasted_iota(jnp.int32, (ng, per, tm), 1)
    m1 = jnp.max(grp, axis=1, keepdims=True)
    i1 = jnp.min(jnp.where(grp == m1, io_p, per), axis=1, keepdims=True)
    m2 = jnp.max(jnp.where(io_p == i1, -jnp.inf, grp), axis=1, keepdims=True)
    gscore = (m1 + m2).reshape(ng, tm)

    io_g = lax.broadcasted_iota(jnp.int32, (ng, tm), 0)
    gsel = jnp.zeros((ng, tm), F32)
    gwork = gscore
    for _ in range(TOPK_GROUPS):
        gm = jnp.max(gwork, axis=0, keepdims=True)
        gi = jnp.min(jnp.where(gwork == gm, io_g, ng), axis=0, keepdims=True)
        hit = io_g == gi
        gsel = jnp.where(hit, 1.0, gsel)
        gwork = jnp.where(hit, -jnp.inf, gwork)
    emask = jnp.broadcast_to(gsel.reshape(ng, 1, tm), (ng, per, tm)).reshape(ne, tm) > 0.5

    io_e = lax.broadcasted_iota(jnp.int32, (ne, tm), 0)
    work = jnp.where(emask, sel, -jnp.inf)
    member = jnp.zeros((ne, tm), F32)
    idx_rows = []
    w_rows = []
    for _ in range(TOP_K):
        mx = jnp.max(work, axis=0, keepdims=True)
        ei = jnp.min(jnp.where(work == mx, io_e, ne), axis=0, keepdims=True)
        hit = io_e == ei
        idx_rows.append(ei)
        w_rows.append(jnp.sum(jnp.where(hit, scores, 0.0), axis=0, keepdims=True))
        member = jnp.where(hit, 1.0, member)
        work = jnp.where(hit, -jnp.inf, work)
    wsum = w_rows[0]
    for wr in w_rows[1:]:
        wsum = wsum + wr
    inv = ROUTED_SCALE / wsum

    r = lax.broadcasted_iota(jnp.int32, (tm, tm), 0)
    c = lax.broadcasted_iota(jnp.int32, (tm, tm), 1)
    upper = jnp.where(r < c, 1.0, 0.0).astype(BF16)
    prefix = _dot(member.astype(BF16), upper) + carry_ref[...]
    for j in range(TOP_K):
        hit = io_e == idx_rows[j]
        eidx_ref[j:j + 1, :] = idx_rows[j]
        wt_ref[j:j + 1, :] = w_rows[j] * inv
        pos_ref[j:j + 1, :] = jnp.sum(jnp.where(hit, prefix, 0.0), axis=0, keepdims=True).astype(jnp.int32)
    carry = carry_ref[...] + jnp.sum(member, axis=1, keepdims=True)
    carry_ref[...] = carry
    cnt_ref[...] = jnp.broadcast_to(carry, cnt_ref.shape)


def _router(x, router_w, router_bias, tm):
    t, d = x.shape
    ne = N_EXPERTS
    return pl.pallas_call(
        functools.partial(_router_kernel, tm=tm),
        out_shape=(
            jax.ShapeDtypeStruct((TOP_K, t), jnp.int32),
            jax.ShapeDtypeStruct((TOP_K, t), F32),
            jax.ShapeDtypeStruct((TOP_K, t), jnp.int32),
            jax.ShapeDtypeStruct((ne, LANES), F32),
        ),
        grid=(t // tm,),
        in_specs=[
            pl.BlockSpec((tm, d), lambda i: (i, 0)),
            pl.BlockSpec((ne, d), lambda i: (0, 0)),
            pl.BlockSpec((ne, 1), lambda i: (0, 0)),
        ],
        out_specs=(
            pl.BlockSpec((TOP_K, tm), lambda i: (0, i)),
            pl.BlockSpec((TOP_K, tm), lambda i: (0, i)),
            pl.BlockSpec((TOP_K, tm), lambda i: (0, i)),
            pl.BlockSpec((ne, LANES), lambda i: (0, 0)),
        ),
        scratch_shapes=[pltpu.VMEM((ne, 1), F32)],
        compiler_params=_params(("arbitrary",)),
        name="router",
    )(x, router_w.T, router_bias.reshape(ne, 1))


def _slots_kernel(pstart, eidx_ref, pos_ref, dest_ref):
    eidx = eidx_ref[...]
    base = jnp.zeros(eidx.shape, jnp.int32)
    for e in range(N_EXPERTS):
        base = jnp.where(eidx == e, pstart[e], base)
    dest_ref[...] = base + pos_ref[...]


def _slots(pstart, eidx, pos, tn):
    k, t = eidx.shape
    blk = lambda i, ps: (0, i)
    grid_spec = pltpu.PrefetchScalarGridSpec(
        num_scalar_prefetch=1,
        grid=(t // tn,),
        in_specs=[pl.BlockSpec((k, tn), blk), pl.BlockSpec((k, tn), blk)],
        out_specs=pl.BlockSpec((k, tn), blk),
    )
    return pl.pallas_call(
        _slots_kernel,
        out_shape=jax.ShapeDtypeStruct((k, t), jnp.int32),
        grid_spec=grid_spec,
        compiler_params=_params(("parallel",)),
        name="moe_slots",
    )(pstart, eidx, pos)


def _dispatch_kernel(dest_ref, x_ref, xs_hbm, sem):
    tm = x_ref.shape[0]

    def issue(r, carry):
        for j in range(TOP_K):
            pltpu.make_async_copy(x_ref.at[pl.ds(r, 1), :], xs_hbm.at[pl.ds(dest_ref[j, r], 1), :],
                                  sem).start(priority=j % 2)
        return carry

    lax.fori_loop(0, tm, issue, 0)
    for j in range(TOP_K):
        pltpu.make_async_copy(x_ref, xs_hbm.at[pl.ds(0, tm), :], sem).wait()


def _dispatch(dest, x, rows, tm):
    t, d = x.shape
    return pl.pallas_call(
        _dispatch_kernel,
        out_shape=jax.ShapeDtypeStruct((rows, d), x.dtype),
        grid=(t // tm,),
        in_specs=[
            pl.BlockSpec((TOP_K, tm), lambda i: (0, i), memory_space=pltpu.SMEM),
            pl.BlockSpec((tm, d), lambda i: (i, 0)),
        ],
        out_specs=pl.BlockSpec(memory_space=pl.ANY),
        scratch_shapes=[pltpu.SemaphoreType.DMA(())],
        compiler_params=_params(("arbitrary",)),
        name="moe_dispatch",
    )(dest, x)


def _experts_kernel(blk_e, blk_rows, nvalid, blk_first, blk_ord, seq_e, npresent,
                    xs_ref, wg_hbm, wu_hbm, wd_hbm, o_ref, wg_f, wu_f, wd_f, wgu_s, wd_s, sem, *, layer):
    i = pl.program_id(0)
    f = wg_f.shape[-1]
    half = xs_ref.shape[1]

    def weight_copies(e, slot):
        return (pltpu.make_async_copy(wg_hbm.at[layer, e], wg_f.at[slot], sem.at[slot]),
                pltpu.make_async_copy(wu_hbm.at[layer, e], wu_f.at[slot], sem.at[slot]),
                pltpu.make_async_copy(wd_hbm.at[layer, e], wd_f.at[slot], sem.at[slot]))

    @pl.when(i == 0)
    def _prime():
        for cp in weight_copies(seq_e[0], 0):
            cp.start()

    @pl.when(i < nvalid[0])
    def _():
        @pl.when(blk_first[i] == 1)
        def _next_expert():
            k = blk_ord[i]
            slot = lax.rem(k, 2)

            @pl.when(k + 1 < npresent[0])
            def _prefetch():
                for cp in weight_copies(seq_e[k + 1], 1 - slot):
                    cp.start()

            for cp in weight_copies(blk_e[i], slot):
                cp.wait()
            wgu_s[:, 0:f] = wg_f[slot].astype(BF16)
            wgu_s[:, f:2 * f] = wu_f[slot].astype(BF16)
            wd_s[...] = wd_f[slot].astype(BF16)

        bm = xs_ref.shape[0]

        def swiglu(rows):
            row = lax.broadcasted_iota(jnp.int32, (rows, half), 0)
            x_lo, x_hi = _unpack_bf16_pair(jnp.where(row < blk_rows[i], xs_ref[0:rows, :], 0))
            gu = _dot(x_lo.astype(BF16), wgu_s[0:half, :]) + _dot(x_hi.astype(BF16), wgu_s[half:2 * half, :])
            hid = _silu(gu[:, 0:f]) * gu[:, f:2 * f]
            out = _dot(hid.astype(BF16), wd_s[...])
            o_ref[0:rows, :] = _pack_bf16_pair(out[:, 0:half], out[:, half:])
            if rows < bm:
                o_ref[rows:bm, :] = jnp.zeros((bm - rows, half), o_ref.dtype)

        @pl.when(blk_rows[i] > bm // 2)
        def _whole_block():
            swiglu(bm)

        @pl.when(blk_rows[i] <= bm // 2)
        def _half_block():
            swiglu(bm // 2)

    @pl.when(i >= nvalid[0])
    def _():
        o_ref[...] = jnp.zeros(o_ref.shape, o_ref.dtype)


def _experts(xs, layout, w_gate, w_up, w_down, layer):
    p, half = xs.shape
    d = 2 * half
    f = w_gate.shape[-1]
    bm = MOE_BLOCK
    row = lambda i, be, br, nv, *_: (jnp.minimum(i, nv[0] - 1), 0)
    grid_spec = pltpu.PrefetchScalarGridSpec(
        num_scalar_prefetch=len(layout),
        grid=(p // bm,),
        in_specs=[
            pl.BlockSpec((bm, half), row),
            pl.BlockSpec(memory_space=pl.ANY),
            pl.BlockSpec(memory_space=pl.ANY),
            pl.BlockSpec(memory_space=pl.ANY),
        ],
        out_specs=pl.BlockSpec((bm, half), lambda i, *_: (i, 0)),
        scratch_shapes=[
            pltpu.VMEM((2, d, f), F32), pltpu.VMEM((2, d, f), F32), pltpu.VMEM((2, f, d), F32),
            pltpu.VMEM((d, 2 * f), BF16), pltpu.VMEM((f, d), BF16),
            pltpu.SemaphoreType.DMA((2,)),
        ],
    )
    return pl.pallas_call(
        functools.partial(_experts_kernel, layer=layer),
        out_shape=jax.ShapeDtypeStruct((p, half), jnp.int32),
        grid_spec=grid_spec,
        compiler_params=_params(("arbitrary",)),
        name="routed_experts",
    )(*layout, xs, w_gate, w_up, w_down)


def _shared_ln_kernel(dest_ref, x_ref, wt_ref, ys_hbm, wgu_ref, wd_ref, g_ref, b_ref, o_ref, ob_ref,
                      buf, sem, *, alpha):
    f = wd_ref.shape[0]
    tm = x_ref.shape[0]

    def issue(r, carry):
        for j in range(TOP_K):
            pltpu.make_async_copy(ys_hbm.at[pl.ds(dest_ref[j, r], 1), :], buf.at[j, pl.ds(r, 1), :],
                                  sem).start(priority=j % 2)
        return carry

    lax.fori_loop(0, tm, issue, 0)
    gu = _dot(x_ref[...].astype(BF16), wgu_ref[...])
    hid = _silu(gu[:, 0:f]) * gu[:, f:2 * f]
    ffn = _dot(hid.astype(BF16), wd_ref[...])
    for j in range(TOP_K):
        pltpu.make_async_copy(ys_hbm.at[pl.ds(0, tm), :], buf.at[j], sem).wait()
    half = buf.shape[2]
    r_lo = jnp.zeros((tm, half), F32)
    r_hi = jnp.zeros((tm, half), F32)
    for j in range(TOP_K):
        y_lo, y_hi = _unpack_bf16_pair(buf[j])
        wj = wt_ref[:, j:j + 1]
        r_lo = r_lo + wj * y_lo
        r_hi = r_hi + wj * y_hi
    ffn = ffn + jnp.concatenate([r_lo, r_hi], axis=1)
    y = _layernorm(alpha * x_ref[...] + ffn, g_ref[...], b_ref[...])
    o_ref[...] = y
    ob_ref[...] = y.astype(BF16)


def _shared_ln(dest, x, wts_t, ys, wgu, wd, g, b, alpha, tm):
    t, d = x.shape
    f = wd.shape[0]
    const = lambda i: (0, 0)
    tok = lambda i: (i, 0)
    return pl.pallas_call(
        functools.partial(_shared_ln_kernel, alpha=alpha),
        out_shape=(jax.ShapeDtypeStruct((t, d), F32), jax.ShapeDtypeStruct((t, d), BF16)),
        grid=(t // tm,),
        in_specs=[
            pl.BlockSpec((TOP_K, tm), lambda i: (0, i), memory_space=pltpu.SMEM),
            pl.BlockSpec((tm, d), tok),
            pl.BlockSpec((tm, TOP_K), tok),
            pl.BlockSpec(memory_space=pl.ANY),
            pl.BlockSpec((d, 2 * f), const),
            pl.BlockSpec((f, d), const),
            pl.BlockSpec((1, d), const),
            pl.BlockSpec((1, d), const),
        ],
        out_specs=(pl.BlockSpec((tm, d), tok), pl.BlockSpec((tm, d), tok)),
        scratch_shapes=[pltpu.VMEM((TOP_K, tm, d // 2), jnp.int32), pltpu.SemaphoreType.DMA(())],
        compiler_params=_params(("arbitrary",)),
        name="shared_expert_combine_layernorm",
    )(dest, x, wts_t, ys, wgu, wd, g.reshape(1, d), b.reshape(1, d))


def _moe_layout(cnt, t):
    counts = cnt[:, 0].astype(jnp.int32)
    padded = (counts + MOE_BLOCK - 1) // MOE_BLOCK * MOE_BLOCK
    pend = jnp.cumsum(padded)
    pstart = pend - padded
    nblk = (t * TOP_K + N_EXPERTS * MOE_BLOCK) // MOE_BLOCK
    nvalid = pend[-1] // MOE_BLOCK
    experts = jnp.arange(N_EXPERTS, dtype=jnp.int32)
    blocks = jnp.arange(nblk, dtype=jnp.int32)
    starts = jnp.minimum(blocks, nvalid - 1) * MOE_BLOCK
    blk_e = jnp.sum((pend[None, :] <= starts[:, None]).astype(jnp.int32), axis=1)
    onehot = blk_e[:, None] == experts[None, :]
    used_end = jnp.sum(jnp.where(onehot, (pstart + counts)[None, :], 0), axis=1)
    blk_rows = jnp.clip(used_end - starts, 0, MOE_BLOCK)
    present = counts > 0
    ord_e = jnp.cumsum(present.astype(jnp.int32)) - 1
    blk_ord = jnp.sum(jnp.where(onehot, ord_e[None, :], 0), axis=1)
    seq_e = jnp.sum(jnp.where(present[None, :] & (ord_e[None, :] == experts[:, None]), experts[None, :], 0), axis=1)
    prev_e = jnp.concatenate([jnp.full((1,), -1, jnp.int32), blk_e[:-1]])
    blk_first = ((blk_e != prev_e) & (blocks < nvalid)).astype(jnp.int32)
    i32 = lambda v: v.astype(jnp.int32)
    layout = (i32(blk_e), i32(blk_rows), i32(nvalid).reshape(1), blk_first, i32(blk_ord), i32(seq_e),
              i32(jnp.sum(present)).reshape(1))
    return i32(pstart), layout, nblk * MOE_BLOCK


def kernel(x, w_in, da_lambda, da_norm_g, gla_gate_w2, gla_gate_b, gla_norm_g, ssd_conv_w, ssd_conv_b,
           ssd_dt_bias, ssd_a_log, ssd_d, ssd_norm_g, w_branch, w_out, ln1_g, ln1_b, router_w, router_bias,
           exp_w_gate, exp_w_up, exp_w_down, sh_w_gate, sh_w_up, sh_w_down, ln2_g, ln2_b):
    batch, seq, d = x.shape
    depth = w_in.shape[0]
    t = batch * seq
    alpha = (2 * depth) ** 0.25

    da_w = DA_HEADS * DA_V_DIM
    gk_w = GLA_HEADS * GLA_DK
    gv_w = GLA_HEADS * GLA_DV
    conv_dim = SSD_WIDTH + 2 * SSD_GROUPS * SSD_STATE
    sizes = (da_w, da_w, da_w, gk_w, gk_w, gv_w, GLA_GATE_RANK, gv_w, SSD_WIDTH, conv_dim, SSD_HEADS,
             N_BRANCH * d)
    offs = np.concatenate([[0], np.cumsum(sizes)]).tolist()
    (o_daq, o_dak, o_dav, o_gq, o_gk, o_gv, o_glr, o_gr, o_sz, o_sx, o_sdt, o_mg, o_end) = offs

    ones = lambda n: jnp.ones((1, n), F32)
    qk_scale = jnp.concatenate([jnp.full((1, da_w), DA_HEAD_DIM ** -0.5, F32), ones(da_w)], axis=1)

    xf = x.reshape(t, d)
    xb = xf.astype(BF16)
    for l in range(depth):
        wl = w_in[l].astype(BF16)
        w_qk = wl[:, o_daq:o_dav]
        w_vt = wl[:, o_dav:o_gq].T
        w_gla = jnp.concatenate([wl[:, o_gq:o_glr], wl[:, o_gr:o_sz]], axis=1)
        w_ssd = wl[:, o_sz:o_sdt]
        w_small = jnp.concatenate([
            jnp.pad(wl[:, o_glr:o_gr], ((0, 0), (0, LANES - GLA_GATE_RANK))),
            jnp.pad(wl[:, o_sdt:o_mg], ((0, 0), (0, LANES - SSD_HEADS)))], axis=1)
        w_mg = wl[:, o_mg:o_end]

        qk = _matmul(xb, w_qk, qk_scale, BF16, 512, 1024, "proj_attn_qk")
        vt = _matmul_nt(w_vt, xb, BF16, 512, "proj_attn_vt")
        pg = _matmul(xb, w_gla, ones(w_gla.shape[1]), F32, 512, 1024, "proj_gla")
        ps = _matmul(xb, w_ssd, ones(w_ssd.shape[1]), F32, 512, 1024, "proj_ssd")
        small = _matmul(xb, w_small, ones(2 * LANES), F32, 512, 2 * LANES, "proj_small")

        br_a = _diff_attention(qk, vt, da_lambda[l], da_norm_g[l], l, batch, seq)
        w2p = jnp.pad(gla_gate_w2[l], ((0, LANES - GLA_GATE_RANK), (0, 0))).astype(BF16)
        br_b = _gla(pg, small, w2p, gla_gate_b[l].reshape(1, gk_w), gla_norm_g[l], batch, seq)
        br_c = _ssd(ps, small, ssd_conv_w[l], ssd_conv_b[l], ssd_dt_bias[l], ssd_a_log[l], ssd_d[l],
                    ssd_norm_g[l], batch, seq)

        merged = _merge(xb, w_mg, br_a, br_b, br_c, w_branch[l].astype(BF16), 512, 512)
        xf, xp = _outproj_ln(merged, w_out[l].astype(BF16), xf, ln1_g[l], ln1_b[l], alpha, 2 * TOKEN_TILE)

        eidx, wts, pos, cnt = _router(xf, router_w[l], router_bias[l], TOKEN_TILE)
        pstart, layout, n_rows = _moe_layout(cnt, t)
        dest = _slots(pstart, eidx, pos, min(t, 2048))
        xs = _dispatch(dest, xp, n_rows, TOKEN_TILE)
        ys = _experts(xs, layout, exp_w_gate, exp_w_up, exp_w_down, l)
        wgu = jnp.concatenate([sh_w_gate[l], sh_w_up[l]], axis=1).astype(BF16)
        xf, xb = _shared_ln(dest, xf, wts.T, ys, wgu, sh_w_down[l].astype(BF16), ln2_g[l], ln2_b[l], alpha,
                            TOKEN_TILE)
    return xf.reshape(batch, seq, d)
```

```python
import functools
import math

import jax
import jax.numpy as jnp
import numpy as np
from jax import lax
from jax.experimental import pallas as pl
from jax.experimental.pallas import tpu as pltpu

F32 = jnp.float32
BF16 = jnp.bfloat16

DA_HEADS = 4
DA_HEAD_DIM = 128
DA_V_DIM = 2 * DA_HEAD_DIM
GLA_HEADS = 4
GLA_DK = 128
GLA_DV = 256
GLA_GATE_RANK = 16
GLA_TAU = 16.0
GLA_CHUNK = 64
SSD_HEADS = 16
SSD_HEAD_DIM = 64
SSD_STATE = 128
SSD_GROUPS = 4
SSD_CONV = 4
SSD_CHUNK = 128
SSD_WIDTH = SSD_HEADS * SSD_HEAD_DIM
N_BRANCH = 3
BRANCH_WIDTH = 1024
N_EXPERTS = 64
TOP_K = 8
N_EXPERT_GROUPS = 8
TOPK_GROUPS = 4
ROUTED_SCALE = 2.5
EPS = 1e-5

LANES = 128
SUBLANES = 8
VMEM_LIMIT = 52 * 1024 * 1024

ATTN_BLOCK = 512
ATTN_HEADS_PER_STEP = 4
SSD_CHUNKS_PER_STEP = 4
GLA_BLOCK = 512
MOE_BLOCK = 512
TOKEN_TILE = 256
PROJ_ROWS = 512
PROJ_COLS = 1024
MERGE_COLS = 512
SLOTS_TILE = 2048


def _params(sem):
    return pltpu.CompilerParams(dimension_semantics=sem, vmem_limit_bytes=VMEM_LIMIT)


def _silu(x):
    return x * (1.0 / (1.0 + jnp.exp(-x)))


def _sigmoid(x):
    return 1.0 / (1.0 + jnp.exp(-x))


def _split3(x):
    h1 = x.astype(BF16)
    r1 = x - h1.astype(F32)
    h2 = r1.astype(BF16)
    r2 = r1 - h2.astype(F32)
    return h1, h2, r2.astype(BF16)


HI_HALF = -65536


def _pack_bf16_pair(lo, hi):
    lo_bits = lax.bitcast_convert_type(lo.astype(BF16).astype(F32), jnp.int32)
    hi_bits = lax.bitcast_convert_type(hi.astype(BF16).astype(F32), jnp.int32)
    return lax.shift_right_logical(lo_bits, 16) | (hi_bits & HI_HALF)


def _unpack_bf16_pair(w):
    lo = lax.bitcast_convert_type(lax.shift_left(w, 16), F32)
    hi = lax.bitcast_convert_type(w & HI_HALF, F32)
    return lo, hi


def _dot(a, b):
    return jnp.dot(a, b, preferred_element_type=F32)


def _dot_nt(a, b):
    return lax.dot_general(a, b, (((1,), (1,)), ((), ())), preferred_element_type=F32)


def _dot_tn(a, b):
    return lax.dot_general(a, b, (((0,), (0,)), ((), ())), preferred_element_type=F32)


def _mm_kernel(a_ref, w_ref, s_ref, o_ref):
    acc = _dot(a_ref[...], w_ref[...])
    o_ref[...] = (acc * s_ref[...]).astype(o_ref.dtype)


def _matmul(a, w, col_scale, out_dtype, tm, tn, name):
    m, k = a.shape
    n = w.shape[1]
    return pl.pallas_call(
        _mm_kernel,
        out_shape=jax.ShapeDtypeStruct((m, n), out_dtype),
        grid=(n // tn, m // tm),
        in_specs=[
            pl.BlockSpec((tm, k), lambda j, i: (i, 0)),
            pl.BlockSpec((k, tn), lambda j, i: (0, j)),
            pl.BlockSpec((1, tn), lambda j, i: (0, j)),
        ],
        out_specs=pl.BlockSpec((tm, tn), lambda j, i: (i, j)),
        compiler_params=_params(("parallel", "arbitrary")),
        name=name,
    )(a, w, col_scale)


def _mm_nt_kernel(wt_ref, a_ref, o_ref):
    o_ref[...] = _dot_nt(wt_ref[...], a_ref[...]).astype(o_ref.dtype)


def _matmul_nt(wt, a, out_dtype, tm, name):
    n, k = wt.shape
    m = a.shape[0]
    return pl.pallas_call(
        _mm_nt_kernel,
        out_shape=jax.ShapeDtypeStruct((n, m), out_dtype),
        grid=(m // tm,),
        in_specs=[
            pl.BlockSpec((n, k), lambda i: (0, 0)),
            pl.BlockSpec((tm, k), lambda i: (i, 0)),
        ],
        out_specs=pl.BlockSpec((n, tm), lambda i: (0, i)),
        compiler_params=_params(("parallel",)),
        name=name,
    )(wt, a)


ATTN_KEYS_FULL, ATTN_KEYS_DIAG_LAST, ATTN_KEYS_DIAG_FIRST = 0, 1, 2


def _attn_kernel(qi_tab, kj_tab, kind_tab, q_ref, k_ref, vt_ref, koff_ref, slope_ref, lam_ref, g_ref, o_ref,
                 m_ref, l_ref, acc_ref, *, blk, lam_init):
    p = pl.program_id(2)
    qi = qi_tab[p]
    kj = kj_tab[p]
    kind = kind_tab[p]

    @pl.when(kj == 0)
    def _init():
        m_ref[...] = jnp.full(m_ref.shape, -jnp.inf, F32)
        l_ref[...] = jnp.zeros(l_ref.shape, F32)
        acc_ref[...] = jnp.zeros(acc_ref.shape, F32)

    rel = jnp.full((1, blk), (2 * kj - qi) * blk, jnp.int32).astype(F32)

    def step(kind_static):
        nk = blk if kind_static == ATTN_KEYS_DIAG_FIRST else 2 * blk
        if kind_static != ATTN_KEYS_FULL:
            kc = lax.broadcasted_iota(jnp.int32, (nk, blk), 0)
            qr = lax.broadcasted_iota(jnp.int32, (nk, blk), 1)
            keep = kc <= (qr + blk if kind_static == ATTN_KEYS_DIAG_LAST else qr)
        koff = koff_ref[0:nk, :]
        for hd in range(ATTN_HEADS_PER_STEP):
            c0 = slope_ref[hd] * rel
            vt = vt_ref[hd * DA_V_DIM:(hd + 1) * DA_V_DIM, 0:nk]
            for mp in range(2):
                sl = slice(hd * DA_V_DIM + mp * DA_HEAD_DIM, hd * DA_V_DIM + (mp + 1) * DA_HEAD_DIM)
                si = 2 * hd + mp
                s = _dot_nt(k_ref[0:nk, sl], q_ref[:, sl]) + slope_ref[hd] * koff
                if kind_static != ATTN_KEYS_FULL:
                    s = jnp.where(keep, s, -jnp.inf)
                m_old = m_ref[si]
                m_new = jnp.maximum(m_old, jnp.max(s, axis=0, keepdims=True) + c0)
                alpha = jnp.exp(m_old - m_new)
                pr = jnp.exp(s - (m_new - c0))
                l_ref[si] = alpha * l_ref[si] + jnp.sum(pr, axis=0, keepdims=True)
                acc_ref[si] = alpha * acc_ref[si] + _dot(vt, pr.astype(BF16))
                m_ref[si] = m_new

    def finish():
        lp = lam_ref[...]
        lam = (jnp.exp(jnp.sum(lp[0:1] * lp[1:2], axis=1, keepdims=True))
               - jnp.exp(jnp.sum(lp[2:3] * lp[3:4], axis=1, keepdims=True)) + lam_init)
        for hd in range(ATTN_HEADS_PER_STEP):
            ot = acc_ref[2 * hd] / l_ref[2 * hd] - lam * (acc_ref[2 * hd + 1] / l_ref[2 * hd + 1])
            o = ot.T
            o = o * lax.rsqrt(jnp.mean(o * o, axis=1, keepdims=True) + EPS)
            o_ref[:, hd * DA_V_DIM:(hd + 1) * DA_V_DIM] = (o * g_ref[...] * (1.0 - lam_init)).astype(o_ref.dtype)

    @pl.when(kind == ATTN_KEYS_FULL)
    def _full():
        step(ATTN_KEYS_FULL)

    @pl.when(kind == ATTN_KEYS_DIAG_LAST)
    def _diag_last():
        step(ATTN_KEYS_DIAG_LAST)
        finish()

    @pl.when(kind == ATTN_KEYS_DIAG_FIRST)
    def _diag_first():
        step(ATTN_KEYS_DIAG_FIRST)
        finish()


def _diff_attention(qk, vt, lam_params, norm_g, layer_idx, batch, seq):
    blk = min(ATTN_BLOCK, seq // 2)
    nq = seq // blk
    nkp = nq // 2
    steps = []
    for i in range(nq):
        for j in range(i // 2 + 1):
            last = j == i // 2
            kind = ATTN_KEYS_FULL if not last else (ATTN_KEYS_DIAG_LAST if i % 2 else ATTN_KEYS_DIAG_FIRST)
            steps.append((i, j, kind))
    qi_tab, kj_tab, kind_tab = (jnp.asarray([st[c] for st in steps], jnp.int32) for c in range(3))
    lam_init = 0.8 - 0.6 * math.exp(-0.3 * layer_idx)
    h = DA_HEADS
    slopes = np.asarray([2.0 ** (-8.0 * (i + 1) / h) for i in range(h)], np.float32)
    key_off = jnp.asarray(np.broadcast_to(np.arange(2 * blk, dtype=np.float32)[:, None], (2 * blk, blk)))
    slopes = jnp.asarray(np.broadcast_to(slopes[:, None, None], (h, 1, blk)))
    t = batch * seq
    hs = ATTN_HEADS_PER_STEP
    hg = h // hs
    kern = functools.partial(_attn_kernel, blk=blk, lam_init=lam_init)
    grid_spec = pltpu.PrefetchScalarGridSpec(
        num_scalar_prefetch=3,
        grid=(batch, hg, len(steps)),
        in_specs=[
            pl.BlockSpec((blk, hs * DA_V_DIM), lambda b, hh, p, qt, kt, kd: (b * nq + qt[p], hh)),
            pl.BlockSpec((2 * blk, hs * DA_V_DIM), lambda b, hh, p, qt, kt, kd: (b * nkp + kt[p], hg + hh)),
            pl.BlockSpec((hs * DA_V_DIM, 2 * blk), lambda b, hh, p, qt, kt, kd: (hh, b * nkp + kt[p])),
            pl.BlockSpec((2 * blk, blk), lambda b, hh, p, qt, kt, kd: (0, 0)),
            pl.BlockSpec((hs, 1, blk), lambda b, hh, p, qt, kt, kd: (hh, 0, 0)),
            pl.BlockSpec((4, DA_HEAD_DIM), lambda b, hh, p, qt, kt, kd: (0, 0)),
            pl.BlockSpec((1, DA_V_DIM), lambda b, hh, p, qt, kt, kd: (0, 0)),
        ],
        out_specs=pl.BlockSpec((blk, hs * DA_V_DIM), lambda b, hh, p, qt, kt, kd: (b * nq + qt[p], hh)),
        scratch_shapes=[
            pltpu.VMEM((2 * hs, 1, blk), F32),
            pltpu.VMEM((2 * hs, 1, blk), F32),
            pltpu.VMEM((2 * hs, DA_V_DIM, blk), F32),
        ],
    )
    return pl.pallas_call(
        kern,
        out_shape=jax.ShapeDtypeStruct((t, h * DA_V_DIM), BF16),
        grid_spec=grid_spec,
        compiler_params=_params(("parallel", "parallel", "arbitrary")),
        name="diff_attention",
    )(qi_tab, kj_tab, kind_tab, qk, qk, vt, key_off, slopes, lam_params, norm_g.reshape(1, DA_V_DIM))


def _gla_kernel(pg_ref, sm_ref, w2_ref, b2_ref, g_ref, o_ref, st_ref, *, blk):
    n = pl.program_id(1)
    hk = GLA_HEADS * GLA_DK
    hv = GLA_HEADS * GLA_DV
    c_len = GLA_CHUNK

    @pl.when(n == 0)
    def _init():
        st_ref[...] = jnp.zeros(st_ref.shape, F32)

    lr = sm_ref[:, 0:LANES].astype(BF16)
    gl = _dot(lr, w2_ref[...]) + b2_ref[...]
    gk = (jnp.minimum(gl, 0.0) - jnp.log(1.0 + jnp.exp(-jnp.abs(gl)))) * (1.0 / GLA_TAU)
    r = lax.broadcasted_iota(jnp.int32, (blk, blk), 0)
    c = lax.broadcasted_iota(jnp.int32, (blk, blk), 1)
    tri = jnp.where((c <= r) & ((r // c_len) == (c // c_len)), 1.0, 0.0).astype(BF16)
    g1, g2, g3 = _split3(gk)
    bcum = _dot(tri, g1) + _dot(tri, g2) + _dot(tri, g3)

    rr = lax.broadcasted_iota(jnp.int32, (c_len, c_len), 0)
    cc = lax.broadcasted_iota(jnp.int32, (c_len, c_len), 1)
    causal = cc <= rr
    scale = GLA_DK ** -0.5
    for ci in range(blk // c_len):
        rows = slice(ci * c_len, (ci + 1) * c_len)
        for h in range(GLA_HEADS):
            kc = slice(h * GLA_DK, (h + 1) * GLA_DK)
            vc = slice(h * GLA_DV, (h + 1) * GLA_DV)
            b = bcum[rows, kc]
            b_last = b[c_len - 1:c_len, :]
            q = pg_ref[rows, kc]
            k = pg_ref[rows, hk + h * GLA_DK: hk + (h + 1) * GLA_DK]
            v = pg_ref[rows, 2 * hk + h * GLA_DV: 2 * hk + (h + 1) * GLA_DV].astype(BF16)
            rg = pg_ref[rows, 2 * hk + hv + h * GLA_DV: 2 * hk + hv + (h + 1) * GLA_DV]
            q_e = (q * scale * jnp.exp(b)).astype(BF16)
            k_e = (k * jnp.exp(-b)).astype(BF16)
            k_d = (k * jnp.exp(b_last - b)).astype(BF16)
            att = jnp.where(causal, _dot_nt(q_e, k_e), 0.0)
            st = st_ref[h]
            o = _dot(att.astype(BF16), v) + _dot_nt(q_e, st.astype(BF16))
            st_ref[h] = st * jnp.exp(b_last) + _dot_tn(v, k_d)
            o = o * lax.rsqrt(jnp.mean(o * o, axis=1, keepdims=True) + EPS) * g_ref[...]
            o_ref[rows, vc] = (o * _silu(rg)).astype(o_ref.dtype)


def _gla(pg, small, w2p, b2, norm_g, batch, seq):
    blk = min(GLA_BLOCK, seq)
    nb = seq // blk
    t = batch * seq
    hk = GLA_HEADS * GLA_DK
    hv = GLA_HEADS * GLA_DV
    width = pg.shape[1]
    return pl.pallas_call(
        functools.partial(_gla_kernel, blk=blk),
        out_shape=jax.ShapeDtypeStruct((t, hv), BF16),
        grid=(batch, nb),
        in_specs=[
            pl.BlockSpec((blk, width), lambda b, n: (b * nb + n, 0)),
            pl.BlockSpec((blk, small.shape[1]), lambda b, n: (b * nb + n, 0)),
            pl.BlockSpec((LANES, hk), lambda b, n: (0, 0)),
            pl.BlockSpec((1, hk), lambda b, n: (0, 0)),
            pl.BlockSpec((1, GLA_DV), lambda b, n: (0, 0)),
        ],
        out_specs=pl.BlockSpec((blk, hv), lambda b, n: (b * nb + n, 0)),
        scratch_shapes=[pltpu.VMEM((GLA_HEADS, GLA_DV, GLA_DK), F32)],
        compiler_params=_params(("parallel", "arbitrary")),
        name="gla",
    )(pg, small, w2p, b2, norm_g.reshape(1, GLA_DV))


def _ssd_kernel(ps_ref, sm_ref, cw_ref, cb_ref, dtb_ref, alog_ref, dsk_ref, g_ref, o_ref,
                tail_ref, st_ref):
    n = pl.program_id(1)
    q_len = SSD_CHUNK
    w = SSD_WIDTH
    gs = SSD_GROUPS * SSD_STATE
    heads_per_group = SSD_HEADS // SSD_GROUPS
    gw = heads_per_group * SSD_HEAD_DIM

    @pl.when(n == 0)
    def _init():
        tail_ref[...] = jnp.zeros(tail_ref.shape, F32)
        st_ref[...] = jnp.zeros(st_ref.shape, F32)

    r = lax.broadcasted_iota(jnp.int32, (q_len, q_len), 0)
    c = lax.broadcasted_iota(jnp.int32, (q_len, q_len), 1)
    causal = c <= r
    tri = jnp.where(causal, 1.0, 0.0).astype(BF16)

    for ci in range(SSD_CHUNKS_PER_STEP):
        rows = slice(ci * q_len, (ci + 1) * q_len)
        cur = ps_ref[rows, w:w + w + 2 * gs]
        ext = jnp.concatenate([tail_ref[...], cur], axis=0)
        acc = cb_ref[...] + cw_ref[0:1, :] * ext[SUBLANES - 3:SUBLANES - 3 + q_len]
        for i in range(1, SSD_CONV):
            off = SUBLANES - (SSD_CONV - 1) + i
            acc = acc + cw_ref[i:i + 1, :] * ext[off:off + q_len]
        tail_ref[...] = cur[q_len - SUBLANES:q_len]
        xbc = _silu(acc)
        xs = xbc[:, 0:w]
        bm = xbc[:, w:w + gs]
        cm = xbc[:, w + gs:w + 2 * gs]

        dtr = sm_ref[rows, LANES:2 * LANES] + dtb_ref[...]
        dtv = jnp.maximum(dtr, 0.0) + jnp.log(1.0 + jnp.exp(-jnp.abs(dtr)))
        da = dtv * (-jnp.exp(alog_ref[...]))
        d1, d2, d3 = _split3(da)
        a_cs = _dot(tri, d1) + _dot(tri, d2) + _dot(tri, d3)
        a_cs_t = a_cs.T
        a_last = a_cs[q_len - 1:q_len, :]
        e_last = jnp.exp(a_last)

        for g in range(SSD_GROUPS):
            bm_g = bm[:, g * SSD_STATE:(g + 1) * SSD_STATE]
            cm_g = cm[:, g * SSD_STATE:(g + 1) * SSD_STATE]
            bm_b = bm_g.astype(BF16)
            cm_b = cm_g.astype(BF16)
            cb = _dot_nt(cm_b, bm_b)
            y_diag = []
            xdd = []
            e_col = []
            e_row = []
            for rh in range(heads_per_group):
                h = g * heads_per_group + rh
                col = a_cs[:, h:h + 1]
                row = a_cs_t[h:h + 1, :]
                lm = jnp.exp(jnp.where(causal, col - row, -jnp.inf))
                xdt = xs[:, h * SSD_HEAD_DIM:(h + 1) * SSD_HEAD_DIM] * dtv[:, h:h + 1]
                y_diag.append(_dot((cb * lm).astype(BF16), xdt.astype(BF16)))
                xdd.append(xdt * jnp.exp(a_last[:, h:h + 1] - col))
                e_col.append(jnp.broadcast_to(jnp.exp(col), (q_len, SSD_HEAD_DIM)))
                e_row.append(jnp.broadcast_to(e_last[:, h:h + 1], (1, SSD_HEAD_DIM)))
            y_diag = jnp.concatenate(y_diag, axis=1)
            xdd = jnp.concatenate(xdd, axis=1)
            e_col = jnp.concatenate(e_col, axis=1)
            e_row = jnp.concatenate(e_row, axis=1)
            s_prev = st_ref[g]
            y_off = _dot(cm_b, s_prev.astype(BF16)) * e_col
            st_ref[g] = s_prev * e_row + _dot_tn(bm_b, xdd.astype(BF16))
            lanes = slice(g * gw, (g + 1) * gw)
            y = y_diag + y_off + xs[:, lanes] * dsk_ref[:, lanes]
            y = y * _silu(ps_ref[rows, lanes])
            y = y * lax.rsqrt(jnp.mean(y * y, axis=1, keepdims=True) + EPS) * g_ref[:, lanes]
            o_ref[rows, lanes] = y.astype(o_ref.dtype)


def _ssd(ps, small, conv_w, conv_b, dt_bias, a_log, d_skip, norm_g, batch, seq):
    q_len = SSD_CHUNK * SSD_CHUNKS_PER_STEP
    nb = seq // q_len
    t = batch * seq
    conv_dim = conv_w.shape[1]
    pad = LANES - SSD_HEADS
    dtb = jnp.pad(dt_bias, (0, pad)).reshape(1, LANES)
    alog = jnp.pad(a_log, (0, pad)).reshape(1, LANES)
    dsk = jnp.repeat(d_skip, SSD_HEAD_DIM).reshape(1, SSD_WIDTH)
    const = lambda b, n: (0, 0)
    return pl.pallas_call(
        _ssd_kernel,
        out_shape=jax.ShapeDtypeStruct((t, SSD_WIDTH), BF16),
        grid=(batch, nb),
        in_specs=[
            pl.BlockSpec((q_len, ps.shape[1]), lambda b, n: (b * nb + n, 0)),
            pl.BlockSpec((q_len, small.shape[1]), lambda b, n: (b * nb + n, 0)),
            pl.BlockSpec((SSD_CONV, conv_dim), const),
            pl.BlockSpec((1, conv_dim), const),
            pl.BlockSpec((1, LANES), const),
            pl.BlockSpec((1, LANES), const),
            pl.BlockSpec((1, SSD_WIDTH), const),
            pl.BlockSpec((1, SSD_WIDTH), const),
        ],
        out_specs=pl.BlockSpec((q_len, SSD_WIDTH), lambda b, n: (b * nb + n, 0)),
        scratch_shapes=[
            pltpu.VMEM((SUBLANES, conv_dim), F32),
            pltpu.VMEM((SSD_GROUPS, SSD_STATE, SSD_WIDTH // SSD_GROUPS), F32),
        ],
        compiler_params=_params(("parallel", "arbitrary")),
        name="ssd",
    )(ps, small, conv_w, conv_b.reshape(1, conv_dim), dtb, alog, dsk, norm_g.reshape(1, SSD_WIDTH))


def _merge_kernel(x_ref, wga_ref, wgb_ref, wgc_ref, ba_ref, bb_ref, bc_ref, wb_ref, o_ref):
    x = x_ref[...]
    acc = None
    for i, (wg, br) in enumerate(((wga_ref, ba_ref), (wgb_ref, bb_ref), (wgc_ref, bc_ref))):
        gate = _sigmoid(_dot(x, wg[...]))
        term = gate * _dot(br[...], wb_ref[i])
        acc = term if acc is None else acc + term
    o_ref[...] = acc.astype(o_ref.dtype)


def _merge(xb, wgate, br_a, br_b, br_c, wbr, tm, tn):
    t, d = xb.shape
    bw = br_a.shape[1]
    nj = d // tn
    gate_spec = lambda i_br: pl.BlockSpec((d, tn), lambda i, j: (0, i_br * nj + j))
    return pl.pallas_call(
        _merge_kernel,
        out_shape=jax.ShapeDtypeStruct((t, d), BF16),
        grid=(t // tm, nj),
        in_specs=[
            pl.BlockSpec((tm, d), lambda i, j: (i, 0)),
            gate_spec(0), gate_spec(1), gate_spec(2),
            pl.BlockSpec((tm, bw), lambda i, j: (i, 0)),
            pl.BlockSpec((tm, bw), lambda i, j: (i, 0)),
            pl.BlockSpec((tm, bw), lambda i, j: (i, 0)),
            pl.BlockSpec((N_BRANCH, bw, tn), lambda i, j: (0, 0, j)),
        ],
        out_specs=pl.BlockSpec((tm, tn), lambda i, j: (i, j)),
        compiler_params=_params(("parallel", "arbitrary")),
        name="gated_merge",
    )(xb, wgate, wgate, wgate, br_a, br_b, br_c, wbr)


def _layernorm(v, g, b):
    mu = jnp.mean(v, axis=1, keepdims=True)
    d = v - mu
    var = jnp.mean(d * d, axis=1, keepdims=True)
    return d * lax.rsqrt(var + EPS) * g + b


def _outproj_ln_kernel(m_ref, w_ref, x_ref, g_ref, b_ref, o_ref, op_ref, *, alpha):
    mix = _dot(m_ref[...], w_ref[...])
    y = _layernorm(alpha * x_ref[...] + mix, g_ref[...], b_ref[...])
    half = y.shape[1] // 2
    o_ref[...] = y
    op_ref[...] = _pack_bf16_pair(y[:, 0:half], y[:, half:])


def _outproj_ln(merged, w_out, x, g, b, alpha, tm):
    t, d = x.shape
    const = lambda i: (0, 0)
    tok = lambda i: (i, 0)
    return pl.pallas_call(
        functools.partial(_outproj_ln_kernel, alpha=alpha),
        out_shape=(jax.ShapeDtypeStruct((t, d), F32), jax.ShapeDtypeStruct((t, d // 2), jnp.int32)),
        grid=(t // tm,),
        in_specs=[
            pl.BlockSpec((tm, d), tok),
            pl.BlockSpec((d, d), const, pipeline_mode=pl.Buffered(1)),
            pl.BlockSpec((tm, d), tok),
            pl.BlockSpec((1, d), const),
            pl.BlockSpec((1, d), const),
        ],
        out_specs=(pl.BlockSpec((tm, d), tok), pl.BlockSpec((tm, d // 2), tok)),
        compiler_params=_params(("parallel",)),
        name="outproj_layernorm",
    )(merged, w_out, x, g.reshape(1, d), b.reshape(1, d))


def _router_kernel(x_ref, w_ref, bias_ref, eidx_ref, wt_ref, pos_ref, cnt_ref, carry_ref, *, tm):
    i = pl.program_id(0)
    ne = N_EXPERTS
    ng = N_EXPERT_GROUPS
    per = ne // ng

    @pl.when(i == 0)
    def _init():
        carry_ref[...] = jnp.zeros(carry_ref.shape, F32)

    x = x_ref[...]
    xh = x.astype(BF16)
    xl = (x - xh.astype(F32)).astype(BF16)
    w = w_ref[...]
    wh = w.astype(BF16)
    wl = (w - wh.astype(F32)).astype(BF16)
    logits = _dot_nt(wh, xh) + _dot_nt(wh, xl) + _dot_nt(wl, xh)
    scores = _sigmoid(logits)
    sel = scores + bias_ref[...]

    grp = sel.reshape(ng, per, tm)
    io_p = lax.broadcasted_iota(jnp.int32, (ng, per, tm), 1)
    m1 = jnp.max(grp, axis=1, keepdims=True)
    i1 = jnp.min(jnp.where(grp == m1, io_p, per), axis=1, keepdims=True)
    m2 = jnp.max(jnp.where(io_p == i1, -jnp.inf, grp), axis=1, keepdims=True)
    gscore = (m1 + m2).reshape(ng, tm)

    io_g = lax.broadcasted_iota(jnp.int32, (ng, tm), 0)
    gsel = jnp.zeros((ng, tm), F32)
    gwork = gscore
    for _ in range(TOPK_GROUPS):
        gm = jnp.max(gwork, axis=0, keepdims=True)
        gi = jnp.min(jnp.where(gwork == gm, io_g, ng), axis=0, keepdims=True)
        hit = io_g == gi
        gsel = jnp.where(hit, 1.0, gsel)
        gwork = jnp.where(hit, -jnp.inf, gwork)
    emask = jnp.broadcast_to(gsel.reshape(ng, 1, tm), (ng, per, tm)).reshape(ne, tm) > 0.5

    io_e = lax.broadcasted_iota(jnp.int32, (ne, tm), 0)
    work = jnp.where(emask, sel, -jnp.inf)
    member = jnp.zeros((ne, tm), F32)
    idx_rows = []
    w_rows = []
    for _ in range(TOP_K):
        mx = jnp.max(work, axis=0, keepdims=True)
        ei = jnp.min(jnp.where(work == mx, io_e, ne), axis=0, keepdims=True)
        hit = io_e == ei
        idx_rows.append(ei)
        w_rows.append(jnp.sum(jnp.where(hit, scores, 0.0), axis=0, keepdims=True))
        member = jnp.where(hit, 1.0, member)
        work = jnp.where(hit, -jnp.inf, work)
    wsum = w_rows[0]
    for wr in w_rows[1:]:
        wsum = wsum + wr
    inv = ROUTED_SCALE / wsum

    r = lax.broadcasted_iota(jnp.int32, (tm, tm), 0)
    c = lax.broadcasted_iota(jnp.int32, (tm, tm), 1)
    upper = jnp.where(r < c, 1.0, 0.0).astype(BF16)
    prefix = _dot(member.astype(BF16), upper) + carry_ref[...]
    for j in range(TOP_K):
        hit = io_e == idx_rows[j]
        eidx_ref[j:j + 1, :] = idx_rows[j]
        wt_ref[j:j + 1, :] = w_rows[j] * inv
        pos_ref[j:j + 1, :] = jnp.sum(jnp.where(hit, prefix, 0.0), axis=0, keepdims=True).astype(jnp.int32)
    carry = carry_ref[...] + jnp.sum(member, axis=1, keepdims=True)
    carry_ref[...] = carry
    cnt_ref[...] = jnp.broadcast_to(carry, cnt_ref.shape)


def _router(x, router_w, router_bias, tm):
    t, d = x.shape
    ne = N_EXPERTS
    return pl.pallas_call(
        functools.partial(_router_kernel, tm=tm),
        out_shape=(
            jax.ShapeDtypeStruct((TOP_K, t), jnp.int32),
            jax.ShapeDtypeStruct((TOP_K, t), F32),
            jax.ShapeDtypeStruct((TOP_K, t), jnp.int32),
            jax.ShapeDtypeStruct((ne, LANES), F32),
        ),
        grid=(t // tm,),
        in_specs=[
            pl.BlockSpec((tm, d), lambda i: (i, 0)),
            pl.BlockSpec((ne, d), lambda i: (0, 0)),
            pl.BlockSpec((ne, 1), lambda i: (0, 0)),
        ],
        out_specs=(
            pl.BlockSpec((TOP_K, tm), lambda i: (0, i)),
            pl.BlockSpec((TOP_K, tm), lambda i: (0, i)),
            pl.BlockSpec((TOP_K, tm), lambda i: (0, i)),
            pl.BlockSpec((ne, LANES), lambda i: (0, 0)),
        ),
        scratch_shapes=[pltpu.VMEM((ne, 1), F32)],
        compiler_params=_params(("arbitrary",)),
        name="router",
    )(x, router_w.T, router_bias.reshape(ne, 1))


def _slots_kernel(pstart, eidx_ref, pos_ref, dest_ref):
    eidx = eidx_ref[...]
    base = jnp.zeros(eidx.shape, jnp.int32)
    for e in range(N_EXPERTS):
        base = jnp.where(eidx == e, pstart[e], base)
    dest_ref[...] = base + pos_ref[...]


def _slots(pstart, eidx, pos, tn):
    k, t = eidx.shape
    blk = lambda i, ps: (0, i)
    grid_spec = pltpu.PrefetchScalarGridSpec(
        num_scalar_prefetch=1,
        grid=(t // tn,),
        in_specs=[pl.BlockSpec((k, tn), blk), pl.BlockSpec((k, tn), blk)],
        out_specs=pl.BlockSpec((k, tn), blk),
    )
    return pl.pallas_call(
        _slots_kernel,
        out_shape=jax.ShapeDtypeStruct((k, t), jnp.int32),
        grid_spec=grid_spec,
        compiler_params=_params(("parallel",)),
        name="moe_slots",
    )(pstart, eidx, pos)


def _dispatch_kernel(dest_ref, x_ref, xs_hbm, sem):
    tm = x_ref.shape[0]

    def issue(r, carry):
        for j in range(TOP_K):
            pltpu.make_async_copy(x_ref.at[pl.ds(r, 1), :], xs_hbm.at[pl.ds(dest_ref[j, r], 1), :],
                                  sem).start(priority=j % 2)
        return carry

    lax.fori_loop(0, tm, issue, 0)
    for j in range(TOP_K):
        pltpu.make_async_copy(x_ref, xs_hbm.at[pl.ds(0, tm), :], sem).wait()


def _dispatch(dest, x, rows, tm):
    t, d = x.shape
    return pl.pallas_call(
        _dispatch_kernel,
        out_shape=jax.ShapeDtypeStruct((rows, d), x.dtype),
        grid=(t // tm,),
        in_specs=[
            pl.BlockSpec((TOP_K, tm), lambda i: (0, i), memory_space=pltpu.SMEM),
            pl.BlockSpec((tm, d), lambda i: (i, 0)),
        ],
        out_specs=pl.BlockSpec(memory_space=pl.ANY),
        scratch_shapes=[pltpu.SemaphoreType.DMA(())],
        compiler_params=_params(("arbitrary",)),
        name="moe_dispatch",
    )(dest, x)


def _experts_kernel(blk_e, blk_rows, nvalid, blk_first, blk_ord, seq_e, npresent,
                    xs_ref, wg_hbm, wu_hbm, wd_hbm, o_ref, wg_f, wu_f, wd_f, wgu_s, wd_s, sem, *, layer):
    i = pl.program_id(0)
    f = wg_f.shape[-1]
    half = xs_ref.shape[1]

    def weight_copies(e, slot):
        return (pltpu.make_async_copy(wg_hbm.at[layer, e], wg_f.at[slot], sem.at[slot]),
                pltpu.make_async_copy(wu_hbm.at[layer, e], wu_f.at[slot], sem.at[slot]),
                pltpu.make_async_copy(wd_hbm.at[layer, e], wd_f.at[slot], sem.at[slot]))

    @pl.when(i == 0)
    def _prime():
        for cp in weight_copies(seq_e[0], 0):
            cp.start()

    @pl.when(i < nvalid[0])
    def _():
        @pl.when(blk_first[i] == 1)
        def _next_expert():
            k = blk_ord[i]
            slot = lax.rem(k, 2)

            @pl.when(k + 1 < npresent[0])
            def _prefetch():
                for cp in weight_copies(seq_e[k + 1], 1 - slot):
                    cp.start()

            for cp in weight_copies(blk_e[i], slot):
                cp.wait()
            wgu_s[:, 0:f] = wg_f[slot].astype(BF16)
            wgu_s[:, f:2 * f] = wu_f[slot].astype(BF16)
            wd_s[...] = wd_f[slot].astype(BF16)

        bm = xs_ref.shape[0]

        def swiglu(rows):
            row = lax.broadcasted_iota(jnp.int32, (rows, half), 0)
            x_lo, x_hi = _unpack_bf16_pair(jnp.where(row < blk_rows[i], xs_ref[0:rows, :], 0))
            gu = _dot(x_lo.astype(BF16), wgu_s[0:half, :]) + _dot(x_hi.astype(BF16), wgu_s[half:2 * half, :])
            hid = _silu(gu[:, 0:f]) * gu[:, f:2 * f]
            out = _dot(hid.astype(BF16), wd_s[...])
            o_ref[0:rows, :] = _pack_bf16_pair(out[:, 0:half], out[:, half:])
            if rows < bm:
                o_ref[rows:bm, :] = jnp.zeros((bm - rows, half), o_ref.dtype)

        @pl.when(blk_rows[i] > bm // 2)
        def _whole_block():
            swiglu(bm)

        @pl.when(blk_rows[i] <= bm // 2)
        def _half_block():
            swiglu(bm // 2)

    @pl.when(i >= nvalid[0])
    def _():
        o_ref[...] = jnp.zeros(o_ref.shape, o_ref.dtype)


def _experts(xs, layout, w_gate, w_up, w_down, layer):
    p, half = xs.shape
    d = 2 * half
    f = w_gate.shape[-1]
    bm = MOE_BLOCK
    row = lambda i, be, br, nv, *_: (jnp.minimum(i, nv[0] - 1), 0)
    grid_spec = pltpu.PrefetchScalarGridSpec(
        num_scalar_prefetch=len(layout),
        grid=(p // bm,),
        in_specs=[
            pl.BlockSpec((bm, half), row),
            pl.BlockSpec(memory_space=pl.ANY),
            pl.BlockSpec(memory_space=pl.ANY),
            pl.BlockSpec(memory_space=pl.ANY),
        ],
        out_specs=pl.BlockSpec((bm, half), lambda i, *_: (i, 0)),
        scratch_shapes=[
            pltpu.VMEM((2, d, f), F32), pltpu.VMEM((2, d, f), F32), pltpu.VMEM((2, f, d), F32),
            pltpu.VMEM((d, 2 * f), BF16), pltpu.VMEM((f, d), BF16),
            pltpu.SemaphoreType.DMA((2,)),
        ],
    )
    return pl.pallas_call(
        functools.partial(_experts_kernel, layer=layer),
        out_shape=jax.ShapeDtypeStruct((p, half), jnp.int32),
        grid_spec=grid_spec,
        compiler_params=_params(("arbitrary",)),
        name="routed_experts",
    )(*layout, xs, w_gate, w_up, w_down)


def _shared_ln_kernel(dest_ref, x_ref, wt_ref, ys_hbm, wgu_ref, wd_ref, g_ref, b_ref, o_ref, ob_ref,
                      buf, sem, *, alpha):
    f = wd_ref.shape[0]
    tm = x_ref.shape[0]

    def issue(r, carry):
        for j in range(TOP_K):
            pltpu.make_async_copy(ys_hbm.at[pl.ds(dest_ref[j, r], 1), :], buf.at[j, pl.ds(r, 1), :],
                                  sem).start(priority=j % 2)
        return carry

    lax.fori_loop(0, tm, issue, 0)
    gu = _dot(x_ref[...].astype(BF16), wgu_ref[...])
    hid = _silu(gu[:, 0:f]) * gu[:, f:2 * f]
    ffn = _dot(hid.astype(BF16), wd_ref[...])
    for j in range(TOP_K):
        pltpu.make_async_copy(ys_hbm.at[pl.ds(0, tm), :], buf.at[j], sem).wait()
    half = buf.shape[2]
    r_lo = jnp.zeros((tm, half), F32)
    r_hi = jnp.zeros((tm, half), F32)
    for j in range(TOP_K):
        y_lo, y_hi = _unpack_bf16_pair(buf[j])
        wj = wt_ref[:, j:j + 1]
        r_lo = r_lo + wj * y_lo
        r_hi = r_hi + wj * y_hi
    ffn = ffn + jnp.concatenate([r_lo, r_hi], axis=1)
    y = _layernorm(alpha * x_ref[...] + ffn, g_ref[...], b_ref[...])
    o_ref[...] = y
    ob_ref[...] = y.astype(BF16)


def _shared_ln(dest, x, wts_t, ys, wgu, wd, g, b, alpha, tm):
    t, d = x.shape
    f = wd.shape[0]
    const = lambda i: (0, 0)
    tok = lambda i: (i, 0)
    return pl.pallas_call(
        functools.partial(_shared_ln_kernel, alpha=alpha),
        out_shape=(jax.ShapeDtypeStruct((t, d), F32), jax.ShapeDtypeStruct((t, d), BF16)),
        grid=(t // tm,),
        in_specs=[
            pl.BlockSpec((TOP_K, tm), lambda i: (0, i), memory_space=pltpu.SMEM),
            pl.BlockSpec((tm, d), tok),
            pl.BlockSpec((tm, TOP_K), tok),
            pl.BlockSpec(memory_space=pl.ANY),
            pl.BlockSpec((d, 2 * f), const),
            pl.BlockSpec((f, d), const),
            pl.BlockSpec((1, d), const),
            pl.BlockSpec((1, d), const),
        ],
        out_specs=(pl.BlockSpec((tm, d), tok), pl.BlockSpec((tm, d), tok)),
        scratch_shapes=[pltpu.VMEM((TOP_K, tm, d // 2), jnp.int32), pltpu.SemaphoreType.DMA(())],
        compiler_params=_params(("arbitrary",)),
        name="shared_expert_combine_layernorm",
    )(dest, x, wts_t, ys, wgu, wd, g.reshape(1, d), b.reshape(1, d))


def _moe_layout(cnt, t):
    counts = cnt[:, 0].astype(jnp.int32)
    padded = (counts + MOE_BLOCK - 1) // MOE_BLOCK * MOE_BLOCK
    pend = jnp.cumsum(padded)
    pstart = pend - padded
    nblk = (t * TOP_K + N_EXPERTS * MOE_BLOCK) // MOE_BLOCK
    nvalid = pend[-1] // MOE_BLOCK
    experts = jnp.arange(N_EXPERTS, dtype=jnp.int32)
    blocks = jnp.arange(nblk, dtype=jnp.int32)
    starts = jnp.minimum(blocks, nvalid - 1) * MOE_BLOCK
    blk_e = jnp.sum((pend[None, :] <= starts[:, None]).astype(jnp.int32), axis=1)
    onehot = blk_e[:, None] == experts[None, :]
    used_end = jnp.sum(jnp.where(onehot, (pstart + counts)[None, :], 0), axis=1)
    blk_rows = jnp.clip(used_end - starts, 0, MOE_BLOCK)
    present = counts > 0
    ord_e = jnp.cumsum(present.astype(jnp.int32)) - 1
    blk_ord = jnp.sum(jnp.where(onehot, ord_e[None, :], 0), axis=1)
    seq_e = jnp.sum(jnp.where(present[None, :] & (ord_e[None, :] == experts[:, None]), experts[None, :], 0), axis=1)
    prev_e = jnp.concatenate([jnp.full((1,), -1, jnp.int32), blk_e[:-1]])
    blk_first = ((blk_e != prev_e) & (blocks < nvalid)).astype(jnp.int32)
    i32 = lambda v: v.astype(jnp.int32)
    layout = (i32(blk_e), i32(blk_rows), i32(nvalid).reshape(1), blk_first, i32(blk_ord), i32(seq_e),
              i32(jnp.sum(present)).reshape(1))
    return i32(pstart), layout, nblk * MOE_BLOCK


def kernel(x, w_in, da_lambda, da_norm_g, gla_gate_w2, gla_gate_b, gla_norm_g, ssd_conv_w, ssd_conv_b,
           ssd_dt_bias, ssd_a_log, ssd_d, ssd_norm_g, w_branch, w_out, ln1_g, ln1_b, router_w, router_bias,
           exp_w_gate, exp_w_up, exp_w_down, sh_w_gate, sh_w_up, sh_w_down, ln2_g, ln2_b):
    batch, seq, d = x.shape
    depth = w_in.shape[0]
    t = batch * seq
    alpha = (2 * depth) ** 0.25

    da_w = DA_HEADS * DA_V_DIM
    gk_w = GLA_HEADS * GLA_DK
    gv_w = GLA_HEADS * GLA_DV
    conv_dim = SSD_WIDTH + 2 * SSD_GROUPS * SSD_STATE
    sizes = (da_w, da_w, da_w, gk_w, gk_w, gv_w, GLA_GATE_RANK, gv_w, SSD_WIDTH, conv_dim, SSD_HEADS,
             N_BRANCH * d)
    offs = np.concatenate([[0], np.cumsum(sizes)]).tolist()
    (o_daq, o_dak, o_dav, o_gq, o_gk, o_gv, o_glr, o_gr, o_sz, o_sx, o_sdt, o_mg, o_end) = offs

    ones = lambda n: jnp.ones((1, n), F32)
    qk_scale = jnp.concatenate([jnp.full((1, da_w), DA_HEAD_DIM ** -0.5, F32), ones(da_w)], axis=1)

    xf = x.reshape(t, d)
    xb = xf.astype(BF16)
    for l in range(depth):
        wl = w_in[l].astype(BF16)
        w_qk = wl[:, o_daq:o_dav]
        w_vt = wl[:, o_dav:o_gq].T
        w_gla = jnp.concatenate([wl[:, o_gq:o_glr], wl[:, o_gr:o_sz]], axis=1)
        w_ssd = wl[:, o_sz:o_sdt]
        w_small = jnp.concatenate([
            jnp.pad(wl[:, o_glr:o_gr], ((0, 0), (0, LANES - GLA_GATE_RANK))),
            jnp.pad(wl[:, o_sdt:o_mg], ((0, 0), (0, LANES - SSD_HEADS)))], axis=1)
        w_mg = wl[:, o_mg:o_end]

        rows = min(PROJ_ROWS, t)
        qk = _matmul(xb, w_qk, qk_scale, BF16, rows, PROJ_COLS, "proj_attn_qk")
        vt = _matmul_nt(w_vt, xb, BF16, rows, "proj_attn_vt")
        pg = _matmul(xb, w_gla, ones(w_gla.shape[1]), F32, rows, PROJ_COLS, "proj_gla")
        ps = _matmul(xb, w_ssd, ones(w_ssd.shape[1]), F32, rows, PROJ_COLS, "proj_ssd")
        small = _matmul(xb, w_small, ones(2 * LANES), F32, rows, 2 * LANES, "proj_small")

        br_a = _diff_attention(qk, vt, da_lambda[l], da_norm_g[l], l, batch, seq)
        w2p = jnp.pad(gla_gate_w2[l], ((0, LANES - GLA_GATE_RANK), (0, 0))).astype(BF16)
        br_b = _gla(pg, small, w2p, gla_gate_b[l].reshape(1, gk_w), gla_norm_g[l], batch, seq)
        br_c = _ssd(ps, small, ssd_conv_w[l], ssd_conv_b[l], ssd_dt_bias[l], ssd_a_log[l], ssd_d[l],
                    ssd_norm_g[l], batch, seq)

        merged = _merge(xb, w_mg, br_a, br_b, br_c, w_branch[l].astype(BF16), rows, MERGE_COLS)
        xf, xp = _outproj_ln(merged, w_out[l].astype(BF16), xf, ln1_g[l], ln1_b[l], alpha, rows)

        eidx, wts, pos, cnt = _router(xf, router_w[l], router_bias[l], TOKEN_TILE)
        pstart, layout, n_rows = _moe_layout(cnt, t)
        dest = _slots(pstart, eidx, pos, min(t, SLOTS_TILE))
        xs = _dispatch(dest, xp, n_rows, TOKEN_TILE)
        ys = _experts(xs, layout, exp_w_gate, exp_w_up, exp_w_down, l)
        wgu = jnp.concatenate([sh_w_gate[l], sh_w_up[l]], axis=1).astype(BF16)
        xf, xb = _shared_ln(dest, xf, wts.T, ys, wgu, sh_w_down[l].astype(BF16), ln2_g[l], ln2_b[l], alpha,
                            TOKEN_TILE)
    return xf.reshape(batch, seq, d)
```
